```python
import math
import jax
import jax.numpy as jnp
from jax import lax
import numpy as np

D_MODEL = 1024
BATCH = 8
SEQ = 4096
DEPTH = 1

CTX_LEN = 256
GRID_W = 64
N_MOD = 6
EPS = 1e-6
DA_HEADS = 8
DA_QK = 64
DA_V = 2 * DA_QK
DA_QK_COLS = DA_HEADS * 2 * DA_QK
DA_WIDTH = DA_HEADS * DA_V
ROPE_THETA = 10000.0
ROPE_FREQS = DA_QK // 4
Q_BLOCK = 128
RW_HEADS = 16
RW_HEAD = 64
RW_WIDTH = RW_HEADS * RW_HEAD
RW_W_LORA = 64
RW_A_LORA = 64
RW_G_LORA = 128
RW_SHIFT_CH = 3 * RW_WIDTH + RW_W_LORA + RW_A_LORA + RW_G_LORA
RW_LN_EPS = 64e-5
L2_EPS = 1e-12
RW_SPLIT_IDX = (RW_WIDTH, 2 * RW_WIDTH, 3 * RW_WIDTH, 3 * RW_WIDTH + RW_W_LORA, 3 * RW_WIDTH + RW_W_LORA + RW_A_LORA)
N_BRANCH = 2
IN_SPLIT_IDX = (DA_QK_COLS, 2 * DA_QK_COLS, 2 * DA_QK_COLS + DA_WIDTH, 2 * DA_QK_COLS + DA_WIDTH + RW_SHIFT_CH)
IN_WIDTH = 2 * DA_QK_COLS + DA_WIDTH + RW_SHIFT_CH + N_BRANCH * D_MODEL
PEER_HEADS = 8
N_KEYS = 128
N_EXPERTS = N_KEYS * N_KEYS
PEER_TOPK = 16
PEER_QDIM = 256
PEER_CHUNK = 128

kernel_name = 'diffattn_rwkv7_peer_hybrid_dit'


def rmsnorm(x, g):
    xf = x.astype(jnp.float32)
    y = xf * lax.rsqrt(jnp.mean(xf * xf, axis=-1, keepdims=True) + EPS)
    return (y * g.astype(jnp.float32)).astype(x.dtype)


def modulate(h, shift, scale):
    return h * (1 + scale) + shift


def axial_angles(n_tokens):
    rows = n_tokens // GRID_W
    row = jnp.repeat(jnp.arange(rows, dtype=jnp.float32), GRID_W)
    col = jnp.tile(jnp.arange(GRID_W, dtype=jnp.float32), rows)
    inv = ROPE_THETA ** (-jnp.arange(ROPE_FREQS, dtype=jnp.float32) / ROPE_FREQS)
    return jnp.stack([row[:, None] * inv, col[:, None] * inv], axis=1)


def apply_axial_rope(x, ang):
    xs = x.reshape(x.shape[:-1] + (2, 2, ROPE_FREQS))
    cos = jnp.cos(ang)[:, None, None, :, None, :].astype(x.dtype)
    sin = jnp.sin(ang)[:, None, None, :, None, :].astype(x.dtype)
    x1, x2 = xs[..., 0:1, :], xs[..., 1:2, :]
    out = jnp.concatenate([x1 * cos - x2 * sin, x2 * cos + x1 * sin], axis=-2)
    return out.reshape(x.shape)


def diff_heads(q, k, v, q_g, k_g, ang):
    B, T, _ = q.shape
    q = rmsnorm(q.reshape(B, T, DA_HEADS, 2, DA_QK), q_g)
    k = rmsnorm(k.reshape(B, T, DA_HEADS, 2, DA_QK), k_g)
    if ang is not None:
        q = apply_axial_rope(q, ang)
        k = apply_axial_rope(k, ang)
    q = q.transpose(0, 3, 2, 1, 4)
    k = k.transpose(0, 3, 2, 1, 4)
    v = v.reshape(B, T, DA_HEADS, DA_V).transpose(0, 2, 1, 3)
    return q, k, v


def diff_attention(q, k, v, lam):
    B, _, H, T, d = q.shape
    nb = T // Q_BLOCK
    qb = q.reshape(B, 2, H, nb, Q_BLOCK, d).transpose(3, 0, 1, 2, 4, 5)
    scale = d ** -0.5

    def one_block(q_blk):
        s = jnp.einsum('bmhqd,bmhkd->bmhqk', q_blk, k).astype(jnp.float32) * scale
        p = jax.nn.softmax(s, axis=-1)
        pd = (p[:, 0] - lam * p[:, 1]).astype(v.dtype)
        return jnp.einsum('bhqk,bhkd->bhqd', pd, v)

    o = lax.map(one_block, qb)
    return o.transpose(1, 0, 3, 2, 4).reshape(B, T, H * v.shape[-1])


def diff_post(o, out_g, lam_init):
    B, T, _ = o.shape
    o = rmsnorm(o.reshape(B, T, DA_HEADS, DA_V), out_g) * (1 - lam_init)
    return o.reshape(B, T, DA_WIDTH)


def centred_shift(z, w):
    zp = jnp.pad(z, ((0, 0), (1, 1), (0, 0)))
    return w[0] * zp[:, :-2] + w[1] * zp[:, 1:-1] + w[2] * zp[:, 2:]


def head_l2norm(t):
    th = t.reshape(t.shape[:-1] + (RW_HEADS, RW_HEAD)).astype(jnp.float32)
    th = th * lax.rsqrt(jnp.sum(th * th, axis=-1, keepdims=True) + L2_EPS)
    return th.reshape(t.shape).astype(t.dtype)


def dual(t):
    return jnp.stack([t, t])


def to_scan(t):
    t = jnp.stack([t[0], t[1][:, ::-1]]).astype(jnp.float32)
    return t.reshape(t.shape[:3] + (RW_HEADS, RW_HEAD)).transpose(2, 0, 1, 3, 4)


def from_scan(y):
    y = y.transpose(1, 2, 0, 3, 4)
    return y[0] + y[1][:, ::-1]


def rwkv_prepare(z, shift_w, w0, w_up, a0, a_up, k_k, k_a):
    z = centred_shift(z, shift_w)
    r, k, v, wd, ad, gd = jnp.split(z, RW_SPLIT_IDX, axis=-1)
    w_log = -jax.nn.softplus(-(w0[:, None, None, :] + jnp.einsum('btr,drc->dbtc', jnp.tanh(wd), w_up))) - 0.5
    decay = jnp.exp(-jnp.exp(w_log.astype(jnp.float32)))
    a = jax.nn.sigmoid(a0[:, None, None, :] + jnp.einsum('btr,drc->dbtc', ad, a_up))
    kk = head_l2norm(k * k_k)
    k_eff = k[None] * (1 + (a - 1) * k_a)
    scan_in = (to_scan(decay), to_scan(dual(kk)), to_scan(a), to_scan(k_eff), to_scan(dual(v)), to_scan(dual(r)))
    return scan_in, (r, k, v, gd)


def wkv_scan(s0, scan_in):
    def step(s, inp):
        w, kk, a, k, v, r = inp
        sa = jnp.einsum('dbhij,dbhj->dbhi', s, kk)
        s = s * w[..., None, :] - sa[..., :, None] * (kk * a)[..., None, :] + v[..., :, None] * k[..., None, :]
        return s, jnp.einsum('dbhij,dbhj->dbhi', s, r)
    return lax.scan(step, s0, scan_in)


def rwkv_post(y, r, k, v, gd, g_up, r_k, ln_g, ln_b):
    B, T, C = r.shape
    mu = jnp.mean(y, axis=-1, keepdims=True)
    var = jnp.mean(jnp.square(y - mu), axis=-1, keepdims=True)
    yn = ((y - mu) * lax.rsqrt(var + RW_LN_EPS)).reshape(B, T, C) * ln_g + ln_b
    hs = lambda t: t.reshape(B, T, RW_HEADS, RW_HEAD)
    bonus = jnp.sum(hs(r) * hs(k) * r_k, axis=-1, keepdims=True) * hs(v)
    g = jax.nn.sigmoid(gd) @ g_up
    return (yn.astype(r.dtype) + bonus.reshape(B, T, C)) * g


def merge_branches(o_a, o_b, z_gate, w_branch_a, w_branch_b, w_out):
    g_a, g_b = jnp.split(jax.nn.sigmoid(z_gate), N_BRANCH, axis=-1)
    return (g_a * (o_a @ w_branch_a) + g_b * (o_b @ w_branch_b)) @ w_out


def peer_ffn(h, peer_wq, peer_keys, peer_u, peer_v):
    B, T, D = h.shape
    tok = h.reshape(-1, PEER_CHUNK, D)

    def one_chunk(xc):
        q = (xc @ peer_wq).reshape(PEER_CHUNK, PEER_HEADS, 2, PEER_QDIM // 2)
        s = jnp.einsum('chpd,hpkd->chpk', q, peer_keys).astype(jnp.float32)
        sv, si = lax.top_k(s, PEER_TOPK)
        cand = (sv[..., 0, :, None] + sv[..., 1, None, :]).reshape(PEER_CHUNK, PEER_HEADS, PEER_TOPK * PEER_TOPK)
        cidx = (si[..., 0, :, None] * N_KEYS + si[..., 1, None, :]).reshape(PEER_CHUNK, PEER_HEADS, PEER_TOPK * PEER_TOPK)
        top_s, pos = lax.top_k(cand, PEER_TOPK)
        eidx = jnp.take_along_axis(cidx, pos, axis=-1)
        gate = jax.nn.softmax(top_s, axis=-1)
        act = jax.nn.gelu(jnp.einsum('chkd,cd->chk', peer_u[eidx], xc))
        return jnp.einsum('chk,chkd->cd', (gate * act).astype(xc.dtype), peer_v[eidx])

    return lax.map(one_chunk, tok).reshape(B, T, D)


def hybrid_layer(x, ctx, c, c_ctx, w_mod, b_mod, norm1_g, w_in, q_norm_g, k_norm_g, diff_lambda, diff_out_g,
                 rw_shift, rw_w0, rw_w_up, rw_a0, rw_a_up, rw_g_up, rw_k_k, rw_k_a, rw_r_k, rw_ln_g, rw_ln_b,
                 w_branch_a, w_branch_b, w_out, norm2_g, peer_wq, peer_keys, peer_u, peer_v, layer_idx, update_ctx):
    B, S, _ = x.shape
    lam_init = 0.8 - 0.6 * math.exp(-0.3 * layer_idx)
    mod_x = jnp.split((jax.nn.silu(c) @ w_mod + b_mod)[:, None, :], N_MOD, axis=-1)
    mod_c = jnp.split((jax.nn.silu(c_ctx) @ w_mod + b_mod)[None, None, :], N_MOD, axis=-1)
    zx = jnp.split(modulate(rmsnorm(x, norm1_g), mod_x[0], mod_x[1]) @ w_in, IN_SPLIT_IDX, axis=-1)
    zc = jnp.split(modulate(rmsnorm(ctx, norm1_g), mod_c[0], mod_c[1]) @ w_in, IN_SPLIT_IDX, axis=-1)

    qx, kx, vx = diff_heads(zx[0], zx[1], zx[2], q_norm_g, k_norm_g, axial_angles(S))
    qc, kc, vc = diff_heads(zc[0], zc[1], zc[2], q_norm_g, k_norm_g, None)
    lam = jnp.exp(jnp.sum(diff_lambda[0] * diff_lambda[1])) - jnp.exp(jnp.sum(diff_lambda[2] * diff_lambda[3])) + lam_init
    k_all = jnp.concatenate([kc, kx], axis=3)
    v_all = jnp.concatenate([vc, vx], axis=2)
    o_ax = diff_post(diff_attention(qx, k_all, v_all, lam), diff_out_g, lam_init)

    scan_c, aux_c = rwkv_prepare(zc[3], rw_shift, rw_w0, rw_w_up, rw_a0, rw_a_up, rw_k_k, rw_k_a)
    scan_x, aux_x = rwkv_prepare(zx[3], rw_shift, rw_w0, rw_w_up, rw_a0, rw_a_up, rw_k_k, rw_k_a)
    s0 = jnp.zeros((2, B, RW_HEADS, RW_HEAD, RW_HEAD), jnp.float32)
    s_ctx, y_c = wkv_scan(s0, scan_c)
    _, y_x = wkv_scan(s_ctx, scan_x)
    o_bx = rwkv_post(from_scan(y_x), *aux_x, rw_g_up, rw_r_k, rw_ln_g, rw_ln_b)

    x = x + mod_x[2] * merge_branches(o_ax, o_bx, zx[4], w_branch_a, w_branch_b, w_out)
    x = x + mod_x[5] * peer_ffn(modulate(rmsnorm(x, norm2_g), mod_x[3], mod_x[4]), peer_wq, peer_keys, peer_u, peer_v)

    if update_ctx:
        o_ac = diff_post(diff_attention(qc, kc, vc, lam), diff_out_g, lam_init)
        o_bc = rwkv_post(from_scan(y_c), *aux_c, rw_g_up, rw_r_k, rw_ln_g, rw_ln_b)
        ctx = ctx + mod_c[2] * merge_branches(o_ac, o_bc, zc[4], w_branch_a, w_branch_b, w_out)
        ctx = ctx + mod_c[5] * peer_ffn(modulate(rmsnorm(ctx, norm2_g), mod_c[3], mod_c[4]), peer_wq, peer_keys, peer_u, peer_v)
    return x, ctx


def setup_inputs(seed: int = 0) -> dict:
    key = jax.random.key(seed)
    ks = iter(jax.random.split(key, 40))
    nrm = lambda shape, s: jax.random.normal(next(ks), shape, jnp.float32) * s
    L, D = DEPTH, D_MODEL
    return {
        'x': nrm((BATCH, SEQ, D), 1.0),
        'c': nrm((BATCH, D), 1.0),
        'ctx': nrm((BATCH, CTX_LEN, D), 1.0),
        'c_ctx': nrm((D,), 1.0),
        'w_mod': nrm((L, D, N_MOD * D), D ** -0.5),
        'b_mod': nrm((L, N_MOD * D), 0.01),
        'norm1_g': 1.0 + nrm((L, D), 0.02),
        'w_in': nrm((L, D, IN_WIDTH), D ** -0.5),
        'q_norm_g': 1.0 + nrm((L, DA_QK), 0.02),
        'k_norm_g': 1.0 + nrm((L, DA_QK), 0.02),
        'diff_lambda': nrm((L, 4, DA_QK), 0.1),
        'diff_out_g': 1.0 + nrm((L, DA_V), 0.02),
        'rw_shift': jnp.array([0.25, 0.5, 0.25], jnp.float32)[None, :, None] + nrm((L, 3, RW_SHIFT_CH), 0.1),
        'rw_w0': -6.0 + 5.0 * jax.random.uniform(next(ks), (L, 2, RW_WIDTH), jnp.float32),
        'rw_w_up': nrm((L, 2, RW_W_LORA, RW_WIDTH), 0.1 * RW_W_LORA ** -0.5),
        'rw_a0': nrm((L, 2, RW_WIDTH), 0.5),
        'rw_a_up': nrm((L, 2, RW_A_LORA, RW_WIDTH), 0.1 * RW_A_LORA ** -0.5),
        'rw_g_up': nrm((L, RW_G_LORA, RW_WIDTH), RW_G_LORA ** -0.5),
        'rw_k_k': 0.85 + nrm((L, RW_WIDTH), 0.1),
        'rw_k_a': 1.0 + nrm((L, RW_WIDTH), 0.05),
        'rw_r_k': nrm((L, RW_HEADS, RW_HEAD), 0.1),
        'rw_ln_g': 1.0 + nrm((L, RW_WIDTH), 0.02),
        'rw_ln_b': nrm((L, RW_WIDTH), 0.01),
        'w_branch_a': nrm((L, DA_WIDTH, D), DA_WIDTH ** -0.5),
        'w_branch_b': nrm((L, RW_WIDTH, D), RW_WIDTH ** -0.5),
        'w_out': nrm((L, D, D), D ** -0.5),
        'norm2_g': 1.0 + nrm((L, D), 0.02),
        'peer_wq': nrm((L, D, PEER_HEADS * PEER_QDIM), D ** -0.5),
        'peer_keys': nrm((L, PEER_HEADS, 2, N_KEYS, PEER_QDIM // 2), (PEER_QDIM // 2) ** -0.5),
        'peer_u': nrm((L, N_EXPERTS, D), D ** -0.5),
        'peer_v': nrm((L, N_EXPERTS, D), PEER_HEADS ** -0.5),
    }


def reference(x, c, ctx, c_ctx, w_mod, b_mod, norm1_g, w_in, q_norm_g, k_norm_g, diff_lambda, diff_out_g,
              rw_shift, rw_w0, rw_w_up, rw_a0, rw_a_up, rw_g_up, rw_k_k, rw_k_a, rw_r_k, rw_ln_g, rw_ln_b,
              w_branch_a, w_branch_b, w_out, norm2_g, peer_wq, peer_keys, peer_u, peer_v):
    for l in range(DEPTH):
        x, ctx = hybrid_layer(
            x, ctx, c, c_ctx, w_mod[l], b_mod[l], norm1_g[l], w_in[l], q_norm_g[l], k_norm_g[l], diff_lambda[l],
            diff_out_g[l], rw_shift[l], rw_w0[l], rw_w_up[l], rw_a0[l], rw_a_up[l], rw_g_up[l], rw_k_k[l], rw_k_a[l],
            rw_r_k[l], rw_ln_g[l], rw_ln_b[l], w_branch_a[l], w_branch_b[l], w_out[l], norm2_g[l], peer_wq[l],
            peer_keys[l], peer_u[l], peer_v[l], layer_idx=l, update_ctx=l < DEPTH - 1)
    return x
```

```python
import functools
import math

import jax
import jax.numpy as jnp
from jax import lax
from jax.experimental import pallas as pl
from jax.experimental.pallas import tpu as pltpu

D_MODEL = 1024
N_MOD = 6
EPS = 1e-6
GRID_W = 64
DA_HEADS = 8
DA_QK = 64
DA_V = 2 * DA_QK
ROPE_THETA = 10000.0
ROPE_FREQS = DA_QK // 4
RW_HEADS = 16
RW_HEAD = 64
RW_WIDTH = RW_HEADS * RW_HEAD
RW_W_LORA = 64
RW_A_LORA = 64
RW_G_LORA = 128
RW_LN_EPS = 64e-5
L2_EPS = 1e-12
PEER_HEADS = 8
N_KEYS = 128
PEER_TOPK = 16
PEER_QDIM = 256
PEER_CHUNK = 128

COL_Q = 0
COL_K = 1024
COL_V = 2048
COL_RW = 3072
COL_GATE = 6144
COL_LORA = 8192
IN_WIDTH = 8448

LANE = 128
VMEM_LIMIT = 56 * 1024 * 1024


def _cparams(sem):
    return pltpu.CompilerParams(dimension_semantics=sem, vmem_limit_bytes=VMEM_LIMIT)


def _norm_matmul_kernel(x_ref, sc_ref, sh_ref, w_ref, z_ref, hn_ref):
    @pl.when(pl.program_id(2) == 0)
    def _():
        x = x_ref[0]
        ms = jnp.mean(x * x, axis=-1, keepdims=True)
        h = x * lax.rsqrt(ms + EPS) * sc_ref[0] + sh_ref[0]
        hn_ref[...] = h.astype(jnp.bfloat16)

    z_ref[0] = jnp.dot(hn_ref[...], w_ref[...], preferred_element_type=jnp.float32)


def _norm_matmul(x, sc, sh, w, tm, tn):
    B, T, D = x.shape
    N = w.shape[1]
    bm = (lambda b, i, j: (b, 0, 0)) if sc.shape[0] == B else (lambda b, i, j: (0, 0, 0))
    return pl.pallas_call(
        _norm_matmul_kernel,
        grid=(B, T // tm, N // tn),
        in_specs=[
            pl.BlockSpec((1, tm, D), lambda b, i, j: (b, i, 0)),
            pl.BlockSpec((1, 1, D), bm),
            pl.BlockSpec((1, 1, D), bm),
            pl.BlockSpec((D, tn), lambda b, i, j: (0, j)),
        ],
        out_specs=pl.BlockSpec((1, tm, tn), lambda b, i, j: (b, i, j)),
        out_shape=jax.ShapeDtypeStruct((B, T, N), jnp.float32),
        scratch_shapes=[pltpu.VMEM((tm, D), jnp.bfloat16)],
        compiler_params=_cparams(("parallel", "parallel", "arbitrary")),
        name="norm_matmul",
    )(x, sc, sh, w)


def _group_mean_sq(x, avg):
    sq = x * x
    hi = sq.astype(jnp.bfloat16)
    lo = (sq - hi.astype(jnp.float32)).astype(jnp.bfloat16)
    return (jnp.dot(hi, avg, preferred_element_type=jnp.float32)
            + jnp.dot(lo, avg, preferred_element_type=jnp.float32))


def _rope(x, cos, sin_signed, first_half):
    partner = jnp.where(first_half, pltpu.roll(x, LANE - ROPE_FREQS, axis=1), pltpu.roll(x, ROPE_FREQS, axis=1))
    return x * cos + partner * sin_signed


def _diff_attn_kernel(lam_ref, q_ref, kx_ref, vx_ref, kc_ref, vc_ref, cosq_ref, sinq_ref, cosk_ref, sink_ref,
                      qg_ref, kg_ref, og_ref, avg64_ref, o_ref, k_s, v_s, *, n_ctx, out_scale):
    lane = lax.broadcasted_iota(jnp.int32, (1, LANE), 1)
    first_half = (lane % (2 * ROPE_FREQS)) < ROPE_FREQS
    avg64 = avg64_ref[...]

    @pl.when(pl.program_id(2) == 0)
    def _():
        kc = kc_ref[0]
        kc = kc * lax.rsqrt(_group_mean_sq(kc, avg64) + EPS) * kg_ref[...]
        k_s[0:n_ctx, :] = kc.astype(jnp.bfloat16)
        kx = kx_ref[0]
        kx = kx * lax.rsqrt(_group_mean_sq(kx, avg64) + EPS) * kg_ref[...]
        kx = _rope(kx, cosk_ref[...], sink_ref[...], first_half)
        k_s[n_ctx:, :] = kx.astype(jnp.bfloat16)
        v_s[0:n_ctx, :] = vc_ref[0].astype(jnp.bfloat16)
        v_s[n_ctx:, :] = vx_ref[0].astype(jnp.bfloat16)

    q = q_ref[0]
    q = q * lax.rsqrt(_group_mean_sq(q, avg64) + EPS) * qg_ref[...]
    q = _rope(q, cosq_ref[...], sinq_ref[...], first_half) * (DA_QK ** -0.5)
    k = k_s[...]
    ps = []
    for m in range(2):
        in_map = (lane // DA_QK) == m
        qm = jnp.where(in_map, q, 0.0).astype(jnp.bfloat16)
        s = lax.dot_general(qm, k, (((1,), (1,)), ((), ())), preferred_element_type=jnp.float32)
        s = s - jnp.max(s, axis=-1, keepdims=True)
        p = jnp.exp(s)
        ps.append(p * (1.0 / jnp.sum(p, axis=-1, keepdims=True)))
    pd = (ps[0] - lam_ref[0] * ps[1]).astype(jnp.bfloat16)
    o = jnp.dot(pd, v_s[...], preferred_element_type=jnp.float32)
    o = o * lax.rsqrt(jnp.mean(o * o, axis=-1, keepdims=True) + EPS) * og_ref[...] * out_scale
    o_ref[0] = o.astype(o_ref.dtype)


def _diff_attention(lam, zx, zc, cos, sin_signed, qg, kg, og, avg64, tq, out_scale):
    B, S, _ = zx.shape
    C = zc.shape[1]
    qb, kb, vb = COL_Q // LANE, COL_K // LANE, COL_V // LANE
    kern = functools.partial(_diff_attn_kernel, n_ctx=C, out_scale=out_scale)
    const = lambda b, h, i: (0, 0)
    return pl.pallas_call(
        kern,
        grid=(B, DA_HEADS, S // tq),
        in_specs=[
            pl.BlockSpec(memory_space=pltpu.SMEM),
            pl.BlockSpec((1, tq, LANE), lambda b, h, i: (b, i, qb + h)),
            pl.BlockSpec((1, S, LANE), lambda b, h, i: (b, 0, kb + h)),
            pl.BlockSpec((1, S, LANE), lambda b, h, i: (b, 0, vb + h)),
            pl.BlockSpec((1, C, LANE), lambda b, h, i: (b, 0, kb + h)),
            pl.BlockSpec((1, C, LANE), lambda b, h, i: (b, 0, vb + h)),
            pl.BlockSpec((tq, LANE), lambda b, h, i: (i, 0)),
            pl.BlockSpec((tq, LANE), lambda b, h, i: (i, 0)),
            pl.BlockSpec((S, LANE), const),
            pl.BlockSpec((S, LANE), const),
            pl.BlockSpec((1, LANE), const),
            pl.BlockSpec((1, LANE), const),
            pl.BlockSpec((1, LANE), const),
            pl.BlockSpec((LANE, LANE), const),
        ],
        out_specs=pl.BlockSpec((1, tq, LANE), lambda b, h, i: (b, i, h)),
        out_shape=jax.ShapeDtypeStruct((B, S, DA_HEADS * DA_V), jnp.bfloat16),
        scratch_shapes=[pltpu.VMEM((C + S, LANE), jnp.bfloat16), pltpu.VMEM((C + S, LANE), jnp.bfloat16)],
        compiler_params=_cparams(("parallel", "parallel", "arbitrary")),
        name="diff_attention",
    )(lam, zx, zx, zx, zc, zc, cos, sin_signed, cos, sin_signed, qg, kg, og, avg64)


def _merge_kernel(x_ref, oa_ref, ob_ref, ga_ref, gb_ref, g1_ref, wa_ref, wb_ref, wo_ref, sc2_ref, sh2_ref, wq_ref,
                  x1_ref, h2_ref, qp_ref):
    ta = jnp.dot(oa_ref[0], wa_ref[...], preferred_element_type=jnp.float32)
    tb = jnp.dot(ob_ref[0], wb_ref[...], preferred_element_type=jnp.float32)
    mix = jax.nn.sigmoid(ga_ref[0]) * ta + jax.nn.sigmoid(gb_ref[0]) * tb
    y = jnp.dot(mix.astype(jnp.bfloat16), wo_ref[...], preferred_element_type=jnp.float32)
    x1 = x_ref[0] + g1_ref[0] * y
    x1_ref[0] = x1
    ms = jnp.mean(x1 * x1, axis=-1, keepdims=True)
    h2 = x1 * lax.rsqrt(ms + EPS) * sc2_ref[0] + sh2_ref[0]
    h2_ref[0] = h2.astype(h2_ref.dtype)
    qp_ref[0] = jnp.dot(h2.astype(jnp.bfloat16), wq_ref[...], preferred_element_type=jnp.float32)


def _merge(x, oa, ob, zx, g1, wa, wb, wo, sc2, sh2, wq, tm, h2_dtype):
    B, S, D = x.shape
    NQ = wq.shape[1]
    tok = lambda b, i: (b, i, 0)
    per_b = lambda b, i: (b, 0, 0)
    const = lambda b, i: (0, 0)
    ga_blk, gb_blk = COL_GATE // D, COL_GATE // D + 1
    return pl.pallas_call(
        _merge_kernel,
        grid=(B, S // tm),
        in_specs=[
            pl.BlockSpec((1, tm, D), tok),
            pl.BlockSpec((1, tm, D), tok),
            pl.BlockSpec((1, tm, D), tok),
            pl.BlockSpec((1, tm, D), lambda b, i: (b, i, ga_blk)),
            pl.BlockSpec((1, tm, D), lambda b, i: (b, i, gb_blk)),
            pl.BlockSpec((1, 1, D), per_b),
            pl.BlockSpec((D, D), const),
            pl.BlockSpec((D, D), const),
            pl.BlockSpec((D, D), const),
            pl.BlockSpec((1, 1, D), per_b),
            pl.BlockSpec((1, 1, D), per_b),
            pl.BlockSpec((D, NQ), const),
        ],
        out_specs=[
            pl.BlockSpec((1, tm, D), tok),
            pl.BlockSpec((1, tm, D), tok),
            pl.BlockSpec((1, tm, NQ), tok),
        ],
        out_shape=[
            jax.ShapeDtypeStruct((B, S, D), jnp.float32),
            jax.ShapeDtypeStruct((B, S, D), h2_dtype),
            jax.ShapeDtypeStruct((B, S, NQ), jnp.float32),
        ],
        compiler_params=_cparams(("parallel", "parallel")),
        name="merge_peerq",
    )(x, oa, ob, zx, zx, g1, wa, wb, wo, sc2, sh2, wq)


def _centred_shift(z, w):
    zp = jnp.pad(z, ((0, 0), (1, 1), (0, 0)))
    return w[0] * zp[:, :-2] + w[1] * zp[:, 1:-1] + w[2] * zp[:, 2:]


def _head_l2norm(t):
    th = t.reshape(t.shape[:-1] + (RW_HEADS, RW_HEAD))
    th = th * lax.rsqrt(jnp.sum(th * th, axis=-1, keepdims=True) + L2_EPS)
    return th.reshape(t.shape)


def _to_scan(t):
    t = jnp.stack([t[0], t[1][:, ::-1]]).astype(jnp.float32)
    return t.reshape(t.shape[:3] + (RW_HEADS, RW_HEAD)).transpose(2, 0, 1, 3, 4)


def _rwkv_prepare(z, shift_w, w0, w_up, a0, a_up, k_k, k_a):
    z = _centred_shift(z, shift_w)
    r, k, v, wd, ad, gd = jnp.split(z, (1024, 2048, 3072, 3136, 3200), axis=-1)
    w_log = -jax.nn.softplus(-(w0[:, None, None, :] + jnp.einsum('btr,drc->dbtc', jnp.tanh(wd), w_up))) - 0.5
    decay = jnp.exp(-jnp.exp(w_log))
    a = jax.nn.sigmoid(a0[:, None, None, :] + jnp.einsum('btr,drc->dbtc', ad, a_up))
    kk = _head_l2norm(k * k_k)
    k_eff = k[None] * (1 + (a - 1) * k_a)
    dual = lambda t: jnp.stack([t, t])
    scan_in = (_to_scan(decay), _to_scan(dual(kk)), _to_scan(a), _to_scan(k_eff), _to_scan(dual(v)), _to_scan(dual(r)))
    return scan_in, (r, k, v, gd)


def _wkv_scan(s0, scan_in):
    def step(s, inp):
        w, kk, a, k, v, r = inp
        sa = jnp.einsum('dbhij,dbhj->dbhi', s, kk)
        s = s * w[..., None, :] - sa[..., :, None] * (kk * a)[..., None, :] + v[..., :, None] * k[..., None, :]
        return s, jnp.einsum('dbhij,dbhj->dbhi', s, r)
    return lax.scan(step, s0, scan_in)


def _rwkv_post(y, r, k, v, gd, g_up, r_k, ln_g, ln_b):
    B, T, C = r.shape
    mu = jnp.mean(y, axis=-1, keepdims=True)
    var = jnp.mean(jnp.square(y - mu), axis=-1, keepdims=True)
    yn = ((y - mu) * lax.rsqrt(var + RW_LN_EPS)).reshape(B, T, C) * ln_g + ln_b
    hs = lambda t: t.reshape(B, T, RW_HEADS, RW_HEAD)
    bonus = jnp.sum(hs(r) * hs(k) * r_k, axis=-1, keepdims=True) * hs(v)
    g = jax.nn.sigmoid(gd) @ g_up
    return (yn + bonus.reshape(B, T, C)) * g


def _peer_tail(h, q, peer_keys, peer_u, peer_v):
    B, T, D = h.shape
    tok = h.reshape(-1, PEER_CHUNK, D)
    qs = q.reshape(-1, PEER_CHUNK, PEER_HEADS, 2, PEER_QDIM // 2)

    def one_chunk(args):
        xc, qc = args
        s = jnp.einsum('chpd,hpkd->chpk', qc, peer_keys).astype(jnp.float32)
        sv, si = lax.top_k(s, PEER_TOPK)
        cand = (sv[..., 0, :, None] + sv[..., 1, None, :]).reshape(PEER_CHUNK, PEER_HEADS, PEER_TOPK * PEER_TOPK)
        cidx = (si[..., 0, :, None] * N_KEYS + si[..., 1, None, :]).reshape(PEER_CHUNK, PEER_HEADS, PEER_TOPK * PEER_TOPK)
        top_s, pos = lax.top_k(cand, PEER_TOPK)
        eidx = jnp.take_along_axis(cidx, pos, axis=-1)
        gate = jax.nn.softmax(top_s, axis=-1)
        act = jax.nn.gelu(jnp.einsum('chkd,cd->chk', peer_u[eidx], xc))
        return jnp.einsum('chk,chkd->cd', gate * act, peer_v[eidx])

    return lax.map(one_chunk, (tok, qs)).reshape(B, T, D)


def _rope_tables(n_tokens):
    rows = n_tokens // GRID_W
    row = jnp.repeat(jnp.arange(rows, dtype=jnp.float32), GRID_W)
    col = jnp.tile(jnp.arange(GRID_W, dtype=jnp.float32), rows)
    inv = ROPE_THETA ** (-jnp.arange(ROPE_FREQS, dtype=jnp.float32) / ROPE_FREQS)
    ang = jnp.stack([row[:, None] * inv, col[:, None] * inv], axis=1)
    cos = jnp.cos(ang)[:, None, :, None, :]
    sin = jnp.sin(ang)[:, None, :, None, :]
    cos = jnp.broadcast_to(cos, (n_tokens, 2, 2, 2, ROPE_FREQS)).reshape(n_tokens, LANE)
    sgn = jnp.array([-1.0, 1.0], jnp.float32)[None, None, None, :, None]
    sin = jnp.broadcast_to(sin * sgn, (n_tokens, 2, 2, 2, ROPE_FREQS)).reshape(n_tokens, LANE)
    return cos, sin


def kernel(x, c, ctx, c_ctx, w_mod, b_mod, norm1_g, w_in, q_norm_g, k_norm_g, diff_lambda, diff_out_g, rw_shift,
           rw_w0, rw_w_up, rw_a0, rw_a_up, rw_g_up, rw_k_k, rw_k_a, rw_r_k, rw_ln_g, rw_ln_b, w_branch_a,
           w_branch_b, w_out, norm2_g, peer_wq, peer_keys, peer_u, peer_v):
    assert w_mod.shape[0] == 1, "single-layer trunk only"
    B, S, D = x.shape
    C = ctx.shape[1]
    bf = jnp.bfloat16
    lam_init = 0.8 - 0.6 * math.exp(-0.3 * 0)

    mod_x = (jax.nn.silu(c) @ w_mod[0] + b_mod[0]).reshape(B, N_MOD, 1, D)
    mod_c = (jax.nn.silu(c_ctx) @ w_mod[0] + b_mod[0]).reshape(1, N_MOD, 1, D)
    sc1x, sh1x = norm1_g[0] * (1 + mod_x[:, 1]), mod_x[:, 0]
    sc1c, sh1c = norm1_g[0] * (1 + mod_c[:, 1]), mod_c[:, 0]
    sc2x, sh2x = norm2_g[0] * (1 + mod_x[:, 4]), mod_x[:, 3]
    g1x, g2x = mod_x[:, 2], mod_x[:, 5]

    w_in_p = jnp.concatenate([w_in[0][:, :6144], w_in[0][:, 6400:], w_in[0][:, 6144:6400]], axis=1).astype(bf)
    tm = min(1024, S)
    zx = _norm_matmul(x, sc1x, sh1x, w_in_p, tm, 768)
    zc = _norm_matmul(ctx, sc1c, sh1c, w_in_p, min(tm, C), 768)

    lam = (jnp.exp(jnp.sum(diff_lambda[0, 0] * diff_lambda[0, 1])) - jnp.exp(jnp.sum(diff_lambda[0, 2] * diff_lambda[0, 3]))
           + lam_init).reshape(1)
    cos, sin_signed = _rope_tables(S)
    qg = jnp.tile(q_norm_g[0], 2).reshape(1, LANE)
    kg = jnp.tile(k_norm_g[0], 2).reshape(1, LANE)
    og = diff_out_g[0].reshape(1, LANE)
    grp = jnp.arange(LANE) // DA_QK
    avg64 = jnp.where(grp[:, None] == grp[None, :], 1.0 / DA_QK, 0.0).astype(bf)
    o_a = _diff_attention(lam, zx, zc, cos, sin_signed, qg, kg, og, avg64, min(256, S), 1.0 - lam_init)

    rw_of = lambda z: jnp.concatenate([z[..., COL_RW:COL_RW + 3072], z[..., COL_LORA:COL_LORA + 256]], axis=-1)
    prm = (rw_shift[0], rw_w0[0], rw_w_up[0], rw_a0[0], rw_a_up[0], rw_k_k[0], rw_k_a[0])
    scan_c, _ = _rwkv_prepare(rw_of(zc), *prm)
    scan_x, aux_x = _rwkv_prepare(rw_of(zx), *prm)
    s0 = jnp.zeros((2, B, RW_HEADS, RW_HEAD, RW_HEAD), jnp.float32)
    s_ctx, _ = _wkv_scan(s0, scan_c)
    _, y_x = _wkv_scan(s_ctx, scan_x)
    y_x = y_x.transpose(1, 2, 0, 3, 4)
    y_x = y_x[0] + y_x[1][:, ::-1]
    o_b = _rwkv_post(y_x, *aux_x, rw_g_up[0], rw_r_k[0], rw_ln_g[0], rw_ln_b[0]).astype(bf)

    x1, h2, qp = _merge(x, o_a, o_b, zx, g1x, w_branch_a[0].astype(bf), w_branch_b[0].astype(bf), w_out[0].astype(bf),
                        sc2x, sh2x, peer_wq[0].astype(bf), min(512, S), jnp.float32)

    return x1 + g2x * _peer_tail(h2, qp, peer_keys[0], peer_u[0], peer_v[0])
```

```python
import functools
import math

import jax
import jax.numpy as jnp
from jax import lax
from jax.experimental import pallas as pl
from jax.experimental.pallas import tpu as pltpu

D_MODEL = 1024
N_MOD = 6
EPS = 1e-6
GRID_W = 64
DA_HEADS = 8
DA_QK = 64
DA_V = 2 * DA_QK
ROPE_THETA = 10000.0
ROPE_FREQS = DA_QK // 4
RW_HEADS = 16
RW_HEAD = 64
RW_WIDTH = RW_HEADS * RW_HEAD
RW_W_LORA = 64
RW_A_LORA = 64
RW_G_LORA = 128
RW_LN_EPS = 64e-5
L2_EPS = 1e-12
PEER_HEADS = 8
N_KEYS = 128
PEER_TOPK = 16
PEER_QDIM = 256
PEER_CHUNK = 128

COL_Q = 0
COL_K = 1024
COL_V = 2048
COL_RW = 3072
COL_GATE = 6144
COL_LORA = 8192
IN_WIDTH = 8448

LANE = 128
VMEM_LIMIT = 56 * 1024 * 1024


def _cparams(sem):
    return pltpu.CompilerParams(dimension_semantics=sem, vmem_limit_bytes=VMEM_LIMIT)


def _norm_matmul_kernel(x_ref, sc_ref, sh_ref, w_ref, z_ref, hn_ref):
    @pl.when(pl.program_id(2) == 0)
    def _():
        x = x_ref[0]
        ms = jnp.mean(x * x, axis=-1, keepdims=True)
        h = x * lax.rsqrt(ms + EPS) * sc_ref[0] + sh_ref[0]
        hn_ref[...] = h.astype(jnp.bfloat16)

    z_ref[0] = jnp.dot(hn_ref[...], w_ref[...], preferred_element_type=jnp.float32)


def _norm_matmul(x, sc, sh, w, tm, tn):
    B, T, D = x.shape
    N = w.shape[1]
    bm = (lambda b, i, j: (b, 0, 0)) if sc.shape[0] == B else (lambda b, i, j: (0, 0, 0))
    return pl.pallas_call(
        _norm_matmul_kernel,
        grid=(B, T // tm, N // tn),
        in_specs=[
            pl.BlockSpec((1, tm, D), lambda b, i, j: (b, i, 0)),
            pl.BlockSpec((1, 1, D), bm),
            pl.BlockSpec((1, 1, D), bm),
            pl.BlockSpec((D, tn), lambda b, i, j: (0, j)),
        ],
        out_specs=pl.BlockSpec((1, tm, tn), lambda b, i, j: (b, i, j)),
        out_shape=jax.ShapeDtypeStruct((B, T, N), jnp.float32),
        scratch_shapes=[pltpu.VMEM((tm, D), jnp.bfloat16)],
        compiler_params=_cparams(("parallel", "parallel", "arbitrary")),
        name="norm_matmul",
    )(x, sc, sh, w)


def _group_mean_sq(x, avg):
    sq = x * x
    hi = sq.astype(jnp.bfloat16)
    lo = (sq - hi.astype(jnp.float32)).astype(jnp.bfloat16)
    return (jnp.dot(hi, avg, preferred_element_type=jnp.float32)
            + jnp.dot(lo, avg, preferred_element_type=jnp.float32))


def _head_sum(x, ones_bd):
    hi = x.astype(jnp.bfloat16)
    lo = (x - hi.astype(jnp.float32)).astype(jnp.bfloat16)
    return (jnp.dot(hi, ones_bd, preferred_element_type=jnp.float32)
            + jnp.dot(lo, ones_bd, preferred_element_type=jnp.float32))


def _rope(x, cos, sin_signed, first_half):
    partner = jnp.where(first_half, pltpu.roll(x, LANE - ROPE_FREQS, axis=1), pltpu.roll(x, ROPE_FREQS, axis=1))
    return x * cos + partner * sin_signed


def _diff_attn_kernel(lam_ref, q_ref, kx_ref, vx_ref, kc_ref, vc_ref, cosq_ref, sinq_ref, cosk_ref, sink_ref,
                      qg_ref, kg_ref, og_ref, avg64_ref, o_ref, k_s, v_s, *, n_ctx, out_scale):
    lane = lax.broadcasted_iota(jnp.int32, (1, LANE), 1)
    first_half = (lane % (2 * ROPE_FREQS)) < ROPE_FREQS
    avg64 = avg64_ref[...]

    @pl.when(pl.program_id(2) == 0)
    def _():
        kc = kc_ref[0]
        kc = kc * lax.rsqrt(_group_mean_sq(kc, avg64) + EPS) * kg_ref[...]
        k_s[0:n_ctx, :] = kc.astype(jnp.bfloat16)
        kx = kx_ref[0]
        kx = kx * lax.rsqrt(_group_mean_sq(kx, avg64) + EPS) * kg_ref[...]
        kx = _rope(kx, cosk_ref[...], sink_ref[...], first_half)
        k_s[n_ctx:, :] = kx.astype(jnp.bfloat16)
        v_s[0:n_ctx, :] = vc_ref[0].astype(jnp.bfloat16)
        v_s[n_ctx:, :] = vx_ref[0].astype(jnp.bfloat16)

    q = q_ref[0]
    q = q * lax.rsqrt(_group_mean_sq(q, avg64) + EPS) * qg_ref[...]
    q = _rope(q, cosq_ref[...], sinq_ref[...], first_half) * (DA_QK ** -0.5)
    k = k_s[...]
    ps = []
    for m in range(2):
        in_map = (lane // DA_QK) == m
        qm = jnp.where(in_map, q, 0.0).astype(jnp.bfloat16)
        s = lax.dot_general(qm, k, (((1,), (1,)), ((), ())), preferred_element_type=jnp.float32)
        s = s - jnp.max(s, axis=-1, keepdims=True)
        p = jnp.exp(s)
        ps.append(p * (1.0 / jnp.sum(p, axis=-1, keepdims=True)))
    pd = (ps[0] - lam_ref[0] * ps[1]).astype(jnp.bfloat16)
    o = jnp.dot(pd, v_s[...], preferred_element_type=jnp.float32)
    o = o * lax.rsqrt(jnp.mean(o * o, axis=-1, keepdims=True) + EPS) * og_ref[...] * out_scale
    o_ref[0] = o.astype(o_ref.dtype)


def _diff_attention(lam, zx, zc, cos, sin_signed, qg, kg, og, avg64, tq, out_scale):
    B, S, _ = zx.shape
    C = zc.shape[1]
    qb, kb, vb = COL_Q // LANE, COL_K // LANE, COL_V // LANE
    kern = functools.partial(_diff_attn_kernel, n_ctx=C, out_scale=out_scale)
    const = lambda b, h, i: (0, 0)
    return pl.pallas_call(
        kern,
        grid=(B, DA_HEADS, S // tq),
        in_specs=[
            pl.BlockSpec(memory_space=pltpu.SMEM),
            pl.BlockSpec((1, tq, LANE), lambda b, h, i: (b, i, qb + h)),
            pl.BlockSpec((1, S, LANE), lambda b, h, i: (b, 0, kb + h)),
            pl.BlockSpec((1, S, LANE), lambda b, h, i: (b, 0, vb + h)),
            pl.BlockSpec((1, C, LANE), lambda b, h, i: (b, 0, kb + h)),
            pl.BlockSpec((1, C, LANE), lambda b, h, i: (b, 0, vb + h)),
            pl.BlockSpec((tq, LANE), lambda b, h, i: (i, 0)),
            pl.BlockSpec((tq, LANE), lambda b, h, i: (i, 0)),
            pl.BlockSpec((S, LANE), const),
            pl.BlockSpec((S, LANE), const),
            pl.BlockSpec((1, LANE), const),
            pl.BlockSpec((1, LANE), const),
            pl.BlockSpec((1, LANE), const),
            pl.BlockSpec((LANE, LANE), const),
        ],
        out_specs=pl.BlockSpec((1, tq, LANE), lambda b, h, i: (b, i, h)),
        out_shape=jax.ShapeDtypeStruct((B, S, DA_HEADS * DA_V), jnp.bfloat16),
        scratch_shapes=[pltpu.VMEM((C + S, LANE), jnp.bfloat16), pltpu.VMEM((C + S, LANE), jnp.bfloat16)],
        compiler_params=_cparams(("parallel", "parallel", "arbitrary")),
        name="diff_attention",
    )(lam, zx, zx, zx, zc, zc, cos, sin_signed, cos, sin_signed, qg, kg, og, avg64)


def _merge_kernel(x_ref, oa_ref, y0_ref, y1_ref, bonus_ref, g_ref, lng_ref, lnb_ref, ones_ref, ga_ref, gb_ref, g1_ref,
                  wa_ref, wb_ref, wo_ref, sc2_ref, sh2_ref, wq_ref, x1_ref, h2_ref, qp_ref):
    ones_bd = ones_ref[...]
    yy = y0_ref[0, 0] + y1_ref[0, 0]
    dlt = yy - _head_sum(yy, ones_bd) * (1.0 / RW_HEAD)
    var = _head_sum(dlt * dlt, ones_bd) * (1.0 / RW_HEAD)
    ob = (dlt * lax.rsqrt(var + RW_LN_EPS) * lng_ref[...] + lnb_ref[...] + bonus_ref[0]) * g_ref[0]
    ta = jnp.dot(oa_ref[0], wa_ref[...], preferred_element_type=jnp.float32)
    tb = jnp.dot(ob.astype(jnp.bfloat16), wb_ref[...], preferred_element_type=jnp.float32)
    mix = jax.nn.sigmoid(ga_ref[0]) * ta + jax.nn.sigmoid(gb_ref[0]) * tb
    y = jnp.dot(mix.astype(jnp.bfloat16), wo_ref[...], preferred_element_type=jnp.float32)
    x1 = x_ref[0] + g1_ref[0] * y
    x1_ref[0] = x1
    ms = jnp.mean(x1 * x1, axis=-1, keepdims=True)
    h2 = x1 * lax.rsqrt(ms + EPS) * sc2_ref[0] + sh2_ref[0]
    h2_ref[0] = h2.astype(h2_ref.dtype)
    qp_ref[0] = jnp.dot(h2.astype(jnp.bfloat16), wq_ref[...], preferred_element_type=jnp.float32)


def _merge(x, oa, y, bonus, g, ln_g, ln_b, ones_bd, zx, g1, wa, wb, wo, sc2, sh2, wq, tm, h2_dtype):
    B, S, D = x.shape
    NQ = wq.shape[1]
    tok = lambda b, i: (b, i, 0)
    per_b = lambda b, i: (b, 0, 0)
    const = lambda b, i: (0, 0)
    ga_blk, gb_blk = COL_GATE // D, COL_GATE // D + 1
    return pl.pallas_call(
        _merge_kernel,
        grid=(B, S // tm),
        in_specs=[
            pl.BlockSpec((1, tm, D), tok),
            pl.BlockSpec((1, tm, D), tok),
            pl.BlockSpec((1, 1, tm, D), lambda b, i: (0, b, i, 0)),
            pl.BlockSpec((1, 1, tm, D), lambda b, i: (1, b, i, 0)),
            pl.BlockSpec((1, tm, D), tok),
            pl.BlockSpec((1, tm, D), tok),
            pl.BlockSpec((1, D), const),
            pl.BlockSpec((1, D), const),
            pl.BlockSpec((D, D), const),
            pl.BlockSpec((1, tm, D), lambda b, i: (b, i, ga_blk)),
            pl.BlockSpec((1, tm, D), lambda b, i: (b, i, gb_blk)),
            pl.BlockSpec((1, 1, D), per_b),
            pl.BlockSpec((D, D), const),
            pl.BlockSpec((D, D), const),
            pl.BlockSpec((D, D), const),
            pl.BlockSpec((1, 1, D), per_b),
            pl.BlockSpec((1, 1, D), per_b),
            pl.BlockSpec((D, NQ), const),
        ],
        out_specs=[
            pl.BlockSpec((1, tm, D), tok),
            pl.BlockSpec((1, tm, D), tok),
            pl.BlockSpec((1, tm, NQ), tok),
        ],
        out_shape=[
            jax.ShapeDtypeStruct((B, S, D), jnp.float32),
            jax.ShapeDtypeStruct((B, S, D), h2_dtype),
            jax.ShapeDtypeStruct((B, S, NQ), jnp.float32),
        ],
        compiler_params=_cparams(("parallel", "parallel")),
        name="merge_peerq",
    )(x, oa, y, y, bonus, g, ln_g, ln_b, ones_bd, zx, zx, g1, wa, wb, wo, sc2, sh2, wq)


CHUNK = 64
SCAN_HEADS = 8


def _split3(a):
    hi = a.astype(jnp.bfloat16)
    r1 = a - hi.astype(jnp.float32)
    mid = r1.astype(jnp.bfloat16)
    lo = (r1 - mid.astype(jnp.float32)).astype(jnp.bfloat16)
    return hi, mid, lo


def _dot3(m, parts):
    acc = jnp.dot(m, parts[0], preferred_element_type=jnp.float32)
    acc = acc + jnp.dot(m, parts[1], preferred_element_type=jnp.float32)
    return acc + jnp.dot(m, parts[2], preferred_element_type=jnp.float32)


def _shift3(z, prev_row, next_row, w):
    n = z.shape[0]
    row = lax.broadcasted_iota(jnp.int32, z.shape, 0)
    zm = jnp.where(row == 0, prev_row, pltpu.roll(z, 1, axis=0))
    zp = jnp.where(row == n - 1, next_row, pltpu.roll(z, n - 1, axis=0))
    return w[0:1] * zm + w[1:2] * z + w[2:3] * zp


def _rwkv_prep_kernel(z_ref, zp_ref, zn_ref, l_ref, lp_ref, ln_ref, sw_ref, swl_ref, w0_ref, a0_ref, kk_ref, ka_ref,
                      rk_ref, wl_ref, gup_ref, ones_ref, tril_ref, triu_ref,
                      at_ref, bt_ref, kt_ref, rt_ref, bh_ref, kh_ref, v_ref, pl_ref, bonus_ref, g_ref, *, n_tiles):
    i = pl.program_id(1)
    inner_lo = 1.0 - (i == 0).astype(jnp.float32)
    inner_hi = 1.0 - (i == n_tiles - 1).astype(jnp.float32)
    zs = _shift3(z_ref[0], zp_ref[0, 7:8, :] * inner_lo, zn_ref[0, 0:1, :] * inner_hi, sw_ref[...])
    ls = _shift3(l_ref[0], lp_ref[0, 7:8, :] * inner_lo, ln_ref[0, 0:1, :] * inner_hi, swl_ref[...])
    r, k, v = zs[:, 0:RW_WIDTH], zs[:, RW_WIDTH:2 * RW_WIDTH], zs[:, 2 * RW_WIDTH:3 * RW_WIDTH]
    lane = lax.broadcasted_iota(jnp.int32, (1, LANE), 1)
    wa_in = jnp.where(lane < RW_W_LORA, jnp.tanh(ls[:, 0:LANE]), ls[:, 0:LANE]).astype(jnp.bfloat16)
    lora = jnp.dot(wa_in, wl_ref[...], preferred_element_type=jnp.float32)
    ones_bd = ones_ref[...]
    kn = k * kk_ref[...]
    kk = kn * lax.rsqrt(_head_sum(kn * kn, ones_bd) + L2_EPS)
    bonus_ref[0] = _head_sum(r * k * rk_ref[...], ones_bd) * v
    g_ref[0] = jnp.dot(jax.nn.sigmoid(ls[:, LANE:2 * LANE]).astype(jnp.bfloat16), gup_ref[...],
                       preferred_element_type=jnp.float32)
    v_ref[0] = v.astype(v_ref.dtype)
    tris = (tril_ref[...], triu_ref[...])
    for d in range(2):
        lw = lora[:, d * RW_WIDTH:(d + 1) * RW_WIDTH] + w0_ref[d:d + 1, :]
        logw = -jnp.exp(-jax.nn.softplus(-lw) - 0.5)
        a = jax.nn.sigmoid(a0_ref[d:d + 1, :] + lora[:, (2 + d) * RW_WIDTH:(3 + d) * RW_WIDTH])
        k_eff = k * (1.0 + (a - 1.0) * ka_ref[...])
        b = kk * a
        parts = _split3(logw)
        incl = _dot3(tris[d], parts)
        rest = _dot3(tris[1 - d], parts) - logw
        at_ref[d, 0] = (kk * jnp.exp(incl - logw)).astype(at_ref.dtype)
        rt_ref[d, 0] = (r * jnp.exp(incl)).astype(rt_ref.dtype)
        inv = jnp.exp(-incl)
        bt_ref[d, 0] = (b * inv).astype(bt_ref.dtype)
        kt_ref[d, 0] = (k_eff * inv).astype(kt_ref.dtype)
        to_end = jnp.exp(rest)
        bh_ref[d, 0] = (b * to_end).astype(bh_ref.dtype)
        kh_ref[d, 0] = (k_eff * to_end).astype(kh_ref.dtype)
        tot = incl + rest
        for c in range(tot.shape[0] // CHUNK):
            pl_ref[d, 0, c] = jnp.exp(tot[c * CHUNK:c * CHUNK + 1])


def _rwkv_prepare(z, sw, swl, w0, a0, k_k, k_a, r_k, wl, g_up, ones_bd, tril, triu, tt):
    B, T, _ = z.shape
    nt = T // tt
    rb, lb = COL_RW // (3 * RW_WIDTH), COL_LORA // (2 * LANE)
    tb = tt // 8
    prev = lambda i: jnp.maximum(i * tb - 1, 0)
    nxt = lambda i: jnp.minimum((i + 1) * tb, T // 8 - 1)
    const2 = lambda b, i: (0, 0)
    tok = lambda b, i: (b, i, 0)
    dtok = lambda b, i: (0, b, i, 0)
    bf = jnp.bfloat16
    kern = functools.partial(_rwkv_prep_kernel, n_tiles=nt)
    return pl.pallas_call(
        kern,
        grid=(B, nt),
        in_specs=[
            pl.BlockSpec((1, tt, 3 * RW_WIDTH), lambda b, i: (b, i, rb)),
            pl.BlockSpec((1, 8, 3 * RW_WIDTH), lambda b, i: (b, prev(i), rb)),
            pl.BlockSpec((1, 8, 3 * RW_WIDTH), lambda b, i: (b, nxt(i), rb)),
            pl.BlockSpec((1, tt, 2 * LANE), lambda b, i: (b, i, lb)),
            pl.BlockSpec((1, 8, 2 * LANE), lambda b, i: (b, prev(i), lb)),
            pl.BlockSpec((1, 8, 2 * LANE), lambda b, i: (b, nxt(i), lb)),
            pl.BlockSpec((3, 3 * RW_WIDTH), const2),
            pl.BlockSpec((3, 2 * LANE), const2),
            pl.BlockSpec((2, RW_WIDTH), const2),
            pl.BlockSpec((2, RW_WIDTH), const2),
            pl.BlockSpec((1, RW_WIDTH), const2),
            pl.BlockSpec((1, RW_WIDTH), const2),
            pl.BlockSpec((1, RW_WIDTH), const2),
            pl.BlockSpec((LANE, 4 * RW_WIDTH), const2),
            pl.BlockSpec((LANE, RW_WIDTH), const2),
            pl.BlockSpec((RW_WIDTH, RW_WIDTH), const2),
            pl.BlockSpec((tt, tt), const2),
            pl.BlockSpec((tt, tt), const2),
        ],
        out_specs=[pl.BlockSpec((2, 1, tt, RW_WIDTH), dtok)] * 6 + [
            pl.BlockSpec((1, tt, RW_WIDTH), tok),
            pl.BlockSpec((2, 1, tt // CHUNK, 1, RW_WIDTH), lambda b, i: (0, b, i, 0, 0)),
            pl.BlockSpec((1, tt, RW_WIDTH), tok),
            pl.BlockSpec((1, tt, RW_WIDTH), tok),
        ],
        out_shape=[jax.ShapeDtypeStruct((2, B, T, RW_WIDTH), bf)] * 6 + [
            jax.ShapeDtypeStruct((B, T, RW_WIDTH), bf),
            jax.ShapeDtypeStruct((2, B, T // CHUNK, 1, RW_WIDTH), jnp.float32),
            jax.ShapeDtypeStruct((B, T, RW_WIDTH), jnp.float32),
            jax.ShapeDtypeStruct((B, T, RW_WIDTH), jnp.float32),
        ],
        compiler_params=_cparams(("parallel", "parallel")),
        name="rwkv_prepare",
    )(z, z, z, z, z, z, sw, swl, w0, a0, k_k, k_a, r_k, wl, g_up, ones_bd, tril, triu)


_NT = (((1,), (1,)), ((), ()))
_TN = (((0,), (0,)), ((), ()))


def _wkv_scan_kernel(at_ref, bt_ref, kt_ref, rt_ref, bh_ref, kh_ref, v_ref, pl_ref, s0_ref, y_ref, sf_ref, s_scr,
                     *, n_chunks):
    d = pl.program_id(0)
    c = pl.program_id(3)
    f32, bf = jnp.float32, jnp.bfloat16

    @pl.when(c == 0)
    def _():
        s_scr[...] = s0_ref[0, 0]

    row = lax.broadcasted_iota(jnp.int32, (CHUNK, CHUNK), 0)
    col = lax.broadcasted_iota(jnp.int32, (CHUNK, CHUNK), 1)
    order = (row - col) * (1 - 2 * d)
    strict = order > 0
    incl = order >= 0
    eye = (row == col).astype(f32)
    blk_sizes = [4 << i for i in range(CHUNK.bit_length() - 2)]
    same_blk = [(row // bs) == (col // bs) for bs in blk_sizes]
    ys = []
    for h in range(SCAN_HEADS):
        sl = slice(h * RW_HEAD, (h + 1) * RW_HEAD)
        a, b, k, r = at_ref[0, 0, :, sl], bt_ref[0, 0, :, sl], kt_ref[0, 0, :, sl], rt_ref[0, 0, :, sl]
        bh, kh, v = bh_ref[0, 0, :, sl], kh_ref[0, 0, :, sl], v_ref[0, :, sl]
        ar = jnp.concatenate([a, r], axis=0)
        xb = lax.dot_general(ar, b, _NT, preferred_element_type=f32)
        xk = lax.dot_general(ar, k, _NT, preferred_element_type=f32)
        m_ab = jnp.where(strict, xb[:CHUNK], 0.0)
        m_ak = jnp.where(strict, xk[:CHUNK], 0.0)
        m_rb = jnp.where(incl, xb[CHUNK:], 0.0).astype(bf)
        m_rk = jnp.where(incl, xk[CHUNK:], 0.0)
        n0 = jnp.where(same_blk[0], m_ab, 0.0)
        n0b = n0.astype(bf)
        x = eye - n0
        x = x + jnp.dot(x.astype(bf), jnp.dot(n0b, n0b, preferred_element_type=f32).astype(bf),
                        preferred_element_type=f32)
        for lvl in range(1, len(same_blk)):
            e = jnp.where(same_blk[lvl] & ~same_blk[lvl - 1], m_ab, 0.0).astype(bf)
            xb = x.astype(bf)
            x = x - jnp.dot(jnp.dot(xb, e, preferred_element_type=f32).astype(bf), xb, preferred_element_type=f32)
        tb = x.astype(bf)
        w = jnp.dot(jnp.concatenate([m_ak, m_rk], axis=0).astype(bf), v, preferred_element_type=f32)
        ua = jnp.dot(tb, a, preferred_element_type=f32).astype(bf)
        u0 = jnp.dot(tb, w[:CHUNK].astype(bf), preferred_element_type=f32).astype(bf)
        ry = r.astype(f32) - jnp.dot(m_rb, ua, preferred_element_type=f32)
        y0 = w[CHUNK:] - jnp.dot(m_rb, u0, preferred_element_type=f32)
        gm = lax.dot_general(ua, bh, _TN, preferred_element_type=f32)
        hm = (lax.dot_general(v, kh, _TN, preferred_element_type=f32)
              - lax.dot_general(u0, bh, _TN, preferred_element_type=f32))
        s = s_scr[h]
        sb = s.astype(bf)
        ys.append(lax.dot_general(ry.astype(bf), sb, _NT, preferred_element_type=f32) + y0)
        s_scr[h] = s * pl_ref[0, 0, 0, :, sl] - jnp.dot(sb, gm.astype(bf), preferred_element_type=f32) + hm
    y_ref[0, 0] = jnp.concatenate(ys, axis=1)

    @pl.when(c == n_chunks - 1)
    def _():
        sf_ref[0, 0] = s_scr[...]


def _wkv_scan(ops, v, pl_arr, s0):
    _, B, T, W = ops[0].shape
    nc = T // CHUNK
    hw = SCAN_HEADS * RW_HEAD
    ng = RW_HEADS // SCAN_HEADS
    cidx = lambda d, c: c + d * (nc - 1 - 2 * c)
    op_spec = pl.BlockSpec((1, 1, CHUNK, hw), lambda d, b, g, c: (d, b, cidx(d, c), g))
    st_spec = pl.BlockSpec((1, 1, SCAN_HEADS, RW_HEAD, RW_HEAD), lambda d, b, g, c: (d, b, g, 0, 0))
    kern = functools.partial(_wkv_scan_kernel, n_chunks=nc)
    return pl.pallas_call(
        kern,
        grid=(2, B, ng, nc),
        in_specs=[op_spec] * 6 + [
            pl.BlockSpec((1, CHUNK, hw), lambda d, b, g, c: (b, cidx(d, c), g)),
            pl.BlockSpec((1, 1, 1, 1, hw), lambda d, b, g, c: (d, b, cidx(d, c), 0, g)),
            st_spec,
        ],
        out_specs=[op_spec, st_spec],
        out_shape=[
            jax.ShapeDtypeStruct((2, B, T, W), jnp.float32),
            jax.ShapeDtypeStruct((2, B, RW_HEADS, RW_HEAD, RW_HEAD), jnp.float32),
        ],
        scratch_shapes=[pltpu.VMEM((SCAN_HEADS, RW_HEAD, RW_HEAD), jnp.float32)],
        compiler_params=_cparams(("parallel", "parallel", "parallel", "arbitrary")),
        name="wkv_scan",
    )(*ops, v, pl_arr, s0)


def _rwkv_constants(rw_shift, rw_w_up, rw_a_up, tt):
    bf = jnp.bfloat16
    sw, swl = rw_shift[:, :3 * RW_WIDTH], rw_shift[:, 3 * RW_WIDTH:]
    zero = jnp.zeros((RW_W_LORA, RW_WIDTH), jnp.float32)
    wl = jnp.concatenate([
        jnp.concatenate([rw_w_up[0], rw_w_up[1], zero, zero], axis=1),
        jnp.concatenate([zero, zero, rw_a_up[0], rw_a_up[1]], axis=1)], axis=0).astype(bf)
    t = jnp.arange(tt)
    same = (t[:, None] // CHUNK) == (t[None, :] // CHUNK)
    tril = (same & (t[None, :] <= t[:, None])).astype(bf)
    triu = (same & (t[None, :] >= t[:, None])).astype(bf)
    return sw, swl, wl, tril, triu


def _split_bf16(a):
    hi = a.astype(jnp.bfloat16)
    return hi, (a - hi.astype(jnp.float32)).astype(jnp.bfloat16)


def _top_rows(s, payload, k):
    n = s.shape[0]
    row = lax.broadcasted_iota(jnp.int32, s.shape, 0)
    vals, pays = [], []
    for _ in range(k):
        m = jnp.max(s, axis=0, keepdims=True)
        pos = jnp.min(jnp.where(s == m, row, n), axis=0, keepdims=True)
        hit = row == pos
        vals.append(m)
        pays.append(jnp.max(jnp.where(hit, payload, -1), axis=0, keepdims=True))
        s = jnp.where(hit, -jnp.inf, s)
    return jnp.concatenate(vals, axis=0), jnp.concatenate(pays, axis=0)


def _peer_topk_kernel(q_ref, keys_ref, eidx_ref, gate_ref):
    nt_dims = (((1,), (1,)), ((), ()))
    sv, si = [], []
    for p in range(2):
        q_hi, q_lo = _split_bf16(q_ref[0, :, p * N_KEYS:(p + 1) * N_KEYS])
        k_hi, k_lo = _split_bf16(keys_ref[0, p])
        s = (lax.dot_general(k_hi, q_hi, nt_dims, preferred_element_type=jnp.float32)
             + lax.dot_general(k_hi, q_lo, nt_dims, preferred_element_type=jnp.float32)
             + lax.dot_general(k_lo, q_hi, nt_dims, preferred_element_type=jnp.float32))
        v, i = _top_rows(s, lax.broadcasted_iota(jnp.int32, s.shape, 0), PEER_TOPK)
        sv.append(v)
        si.append(i)
    cand = jnp.concatenate([sv[0][a:a + 1] + sv[1] for a in range(PEER_TOPK)], axis=0)
    cidx = jnp.concatenate([si[0][a:a + 1] * N_KEYS + si[1] for a in range(PEER_TOPK)], axis=0)
    top_s, eidx = _top_rows(cand, cidx, PEER_TOPK)
    p = jnp.exp(top_s - top_s[0:1])
    eidx_ref[...] = eidx
    gate_ref[...] = p / jnp.sum(p, axis=0, keepdims=True)


def _peer_topk(qp, keys, tt):
    B, S, _ = qp.shape
    nt = S // tt
    rows = PEER_HEADS * PEER_TOPK
    out_map = lambda b, i, h: (h, b * nt + i)
    return pl.pallas_call(
        _peer_topk_kernel,
        grid=(B, nt, PEER_HEADS),
        in_specs=[
            pl.BlockSpec((1, tt, PEER_QDIM), lambda b, i, h: (b, i, h)),
            pl.BlockSpec((1, 2, N_KEYS, PEER_QDIM // 2), lambda b, i, h: (h, 0, 0, 0)),
        ],
        out_specs=[pl.BlockSpec((PEER_TOPK, tt), out_map), pl.BlockSpec((PEER_TOPK, tt), out_map)],
        out_shape=[jax.ShapeDtypeStruct((rows, B * S), jnp.int32), jax.ShapeDtypeStruct((rows, B * S), jnp.float32)],
        compiler_params=_cparams(("parallel", "parallel", "arbitrary")),
        name="peer_topk",
    )(qp, keys)


PEER_PICKS = PEER_HEADS * PEER_TOPK
GATHER_SLOTS = 4


def _gelu_tanh(a):
    return 0.5 * a * (1.0 + jnp.tanh(math.sqrt(2.0 / math.pi) * (a + 0.044715 * (a * a * a))))


def _peer_gather_kernel(eidx_hbm, h2_ref, gate_ref, x1_ref, g2_ref, tab_hbm, out_ref, idx_smem, buf, idx_sem, row_sem,
                        *, tt):
    tile = pl.program_id(0) * pl.num_programs(1) + pl.program_id(1)
    idx_copy = pltpu.make_async_copy(eidx_hbm.at[tile], idx_smem, idx_sem)
    idx_copy.start()
    idx_copy.wait()

    def issue(t, slot):
        for k in range(PEER_PICKS):
            e = idx_smem[t * PEER_PICKS + k]
            pltpu.make_async_copy(tab_hbm.at[pl.ds(e, 1)], buf.at[slot, pl.ds(k, 1)], row_sem.at[slot]).start()

    def wait_rows(slot):
        pltpu.make_async_copy(tab_hbm.at[pl.ds(0, PEER_PICKS)], buf.at[slot], row_sem.at[slot]).wait()

    lane = lax.broadcasted_iota(jnp.int32, (PEER_PICKS, tt), 1)

    def compute(t, slot):
        w = buf[slot]
        u = pltpu.bitcast(w & jnp.uint32(0xFFFF0000), jnp.float32)
        v = pltpu.bitcast(w << 16, jnp.float32)
        xrow = h2_ref[0, pl.ds(t, 1), :]
        act = jnp.sum(u * xrow, axis=1, keepdims=True)
        gate = jnp.sum(jnp.where(lane == t, gate_ref[...], 0.0), axis=1, keepdims=True)
        coef = gate * _gelu_tanh(act)
        y = jnp.sum(coef * v, axis=0, keepdims=True)
        out_ref[0, pl.ds(t, 1), :] = x1_ref[0, pl.ds(t, 1), :] + g2_ref[0] * y

    for j in range(GATHER_SLOTS - 1):
        issue(j, j)

    def group(g, carry):
        for j in range(GATHER_SLOTS):
            t = g * GATHER_SLOTS + j
            ahead = t + GATHER_SLOTS - 1

            @pl.when(ahead < tt)
            def _():
                issue(ahead, (j + GATHER_SLOTS - 1) % GATHER_SLOTS)

            wait_rows(j)
            compute(t, j)
        return carry

    lax.fori_loop(0, tt // GATHER_SLOTS, group, 0)


def _peer_gather(eidx_t, h2, gate_t, x1, g2, tab, tt):
    B, S, D = x1.shape
    nt = S // tt
    tok = lambda b, i: (b, i, 0)
    kern = functools.partial(_peer_gather_kernel, tt=tt)
    return pl.pallas_call(
        kern,
        grid=(B, nt),
        in_specs=[
            pl.BlockSpec(memory_space=pl.ANY),
            pl.BlockSpec((1, tt, D), tok),
            pl.BlockSpec((PEER_PICKS, tt), lambda b, i: (0, b * nt + i)),
            pl.BlockSpec((1, tt, D), tok),
            pl.BlockSpec((1, 1, D), lambda b, i: (b, 0, 0)),
            pl.BlockSpec(memory_space=pl.ANY),
        ],
        out_specs=pl.BlockSpec((1, tt, D), tok),
        out_shape=jax.ShapeDtypeStruct((B, S, D), jnp.float32),
        scratch_shapes=[
            pltpu.SMEM((tt * PEER_PICKS,), jnp.int32),
            pltpu.VMEM((GATHER_SLOTS, PEER_PICKS, D), jnp.uint32),
            pltpu.SemaphoreType.DMA(()),
            pltpu.SemaphoreType.DMA((GATHER_SLOTS,)),
        ],
        compiler_params=_cparams(("arbitrary", "arbitrary")),
        name="peer_gather",
    )(eidx_t, h2, gate_t, x1, g2, tab)


def _pack_expert_table(peer_u, peer_v):
    u16 = lax.bitcast_convert_type(peer_u.astype(jnp.bfloat16), jnp.uint16).astype(jnp.uint32)
    v16 = lax.bitcast_convert_type(peer_v.astype(jnp.bfloat16), jnp.uint16).astype(jnp.uint32)
    return (u16 << 16) | v16


def _rope_tables(n_tokens):
    rows = n_tokens // GRID_W
    row = jnp.repeat(jnp.arange(rows, dtype=jnp.float32), GRID_W)
    col = jnp.tile(jnp.arange(GRID_W, dtype=jnp.float32), rows)
    inv = ROPE_THETA ** (-jnp.arange(ROPE_FREQS, dtype=jnp.float32) / ROPE_FREQS)
    ang = jnp.stack([row[:, None] * inv, col[:, None] * inv], axis=1)
    cos = jnp.cos(ang)[:, None, :, None, :]
    sin = jnp.sin(ang)[:, None, :, None, :]
    cos = jnp.broadcast_to(cos, (n_tokens, 2, 2, 2, ROPE_FREQS)).reshape(n_tokens, LANE)
    sgn = jnp.array([-1.0, 1.0], jnp.float32)[None, None, None, :, None]
    sin = jnp.broadcast_to(sin * sgn, (n_tokens, 2, 2, 2, ROPE_FREQS)).reshape(n_tokens, LANE)
    return cos, sin


def kernel(x, c, ctx, c_ctx, w_mod, b_mod, norm1_g, w_in, q_norm_g, k_norm_g, diff_lambda, diff_out_g, rw_shift,
           rw_w0, rw_w_up, rw_a0, rw_a_up, rw_g_up, rw_k_k, rw_k_a, rw_r_k, rw_ln_g, rw_ln_b, w_branch_a,
           w_branch_b, w_out, norm2_g, peer_wq, peer_keys, peer_u, peer_v):
    assert w_mod.shape[0] == 1, "single-layer trunk only"
    B, S, D = x.shape
    C = ctx.shape[1]
    bf = jnp.bfloat16
    lam_init = 0.8 - 0.6 * math.exp(-0.3 * 0)

    mod_x = (jax.nn.silu(c) @ w_mod[0] + b_mod[0]).reshape(B, N_MOD, 1, D)
    mod_c = (jax.nn.silu(c_ctx) @ w_mod[0] + b_mod[0]).reshape(1, N_MOD, 1, D)
    sc1x, sh1x = norm1_g[0] * (1 + mod_x[:, 1]), mod_x[:, 0]
    sc1c, sh1c = norm1_g[0] * (1 + mod_c[:, 1]), mod_c[:, 0]
    sc2x, sh2x = norm2_g[0] * (1 + mod_x[:, 4]), mod_x[:, 3]
    g1x, g2x = mod_x[:, 2], mod_x[:, 5]

    w_in_p = jnp.concatenate([w_in[0][:, :6144], w_in[0][:, 6400:], w_in[0][:, 6144:6400]], axis=1).astype(bf)
    tm = min(1024, S)
    zx = _norm_matmul(x, sc1x, sh1x, w_in_p, tm, 768)
    zc = _norm_matmul(ctx, sc1c, sh1c, w_in_p, min(tm, C), 768)

    lam = (jnp.exp(jnp.sum(diff_lambda[0, 0] * diff_lambda[0, 1])) - jnp.exp(jnp.sum(diff_lambda[0, 2] * diff_lambda[0, 3]))
           + lam_init).reshape(1)
    cos, sin_signed = _rope_tables(S)
    qg = jnp.tile(q_norm_g[0], 2).reshape(1, LANE)
    kg = jnp.tile(k_norm_g[0], 2).reshape(1, LANE)
    og = diff_out_g[0].reshape(1, LANE)
    grp = jnp.arange(LANE) // DA_QK
    avg64 = jnp.where(grp[:, None] == grp[None, :], 1.0 / DA_QK, 0.0).astype(bf)
    o_a = _diff_attention(lam, zx, zc, cos, sin_signed, qg, kg, og, avg64, min(256, S), 1.0 - lam_init)

    hd = jnp.arange(RW_WIDTH) // RW_HEAD
    ones_bd = (hd[:, None] == hd[None, :]).astype(bf)
    row1 = lambda t: t.reshape(1, RW_WIDTH)
    prep = {}
    for name, z, T in (("ctx", zc, C), ("x", zx, S)):
        tt = min(256, T)
        sw, swl, wl, tril, triu = _rwkv_constants(rw_shift[0], rw_w_up[0], rw_a_up[0], tt)
        prep[name] = _rwkv_prepare(z, sw, swl, rw_w0[0], rw_a0[0], row1(rw_k_k[0]), row1(rw_k_a[0]), row1(rw_r_k[0]),
                                   wl, rw_g_up[0].astype(bf), ones_bd, tril, triu, tt)
    s0 = jnp.zeros((2, B, RW_HEADS, RW_HEAD, RW_HEAD), jnp.float32)
    pc, px = prep["ctx"], prep["x"]
    _, s_ctx = _wkv_scan(pc[0:6], pc[6], pc[7], s0)
    y_rw, _ = _wkv_scan(px[0:6], px[6], px[7], s_ctx)

    x1, h2, qp = _merge(x, o_a, y_rw, px[8], px[9], row1(rw_ln_g[0]), row1(rw_ln_b[0]), ones_bd, zx, g1x,
                        w_branch_a[0].astype(bf), w_branch_b[0].astype(bf), w_out[0].astype(bf),
                        sc2x, sh2x, peer_wq[0].astype(bf), min(256, S), jnp.float32)

    tt_k = min(256, S)
    eidx_t, gate_t = _peer_topk(qp, peer_keys[0], tt_k)
    tt_g = min(128, S)
    eidx = eidx_t.T.reshape(B * S // tt_g, tt_g * PEER_PICKS)
    tab = _pack_expert_table(peer_u[0], peer_v[0])
    return _peer_gather(eidx, h2, gate_t, x1, g2x, tab, tt_g)
```

```python
import functools
import math

import jax
import jax.numpy as jnp
from jax import lax
from jax.experimental import pallas as pl
from jax.experimental.pallas import tpu as pltpu

D_MODEL = 1024
N_MOD = 6
EPS = 1e-6
GRID_W = 64
DA_HEADS = 8
DA_QK = 64
DA_V = 2 * DA_QK
ROPE_THETA = 10000.0
ROPE_FREQS = DA_QK // 4
RW_HEADS = 16
RW_HEAD = 64
RW_WIDTH = RW_HEADS * RW_HEAD
RW_W_LORA = 64
RW_A_LORA = 64
RW_G_LORA = 128
RW_LN_EPS = 64e-5
L2_EPS = 1e-12
PEER_HEADS = 8
N_KEYS = 128
PEER_TOPK = 16
PEER_QDIM = 256
PEER_CHUNK = 128

COL_Q = 0
COL_K = 1024
COL_V = 2048
COL_RW = 3072
COL_GATE = 6144
COL_LORA = 8192
IN_WIDTH = 8448

LANE = 128
VMEM_LIMIT = 56 * 1024 * 1024


def _cparams(sem):
    return pltpu.CompilerParams(dimension_semantics=sem, vmem_limit_bytes=VMEM_LIMIT)


def _norm_matmul_kernel(x_ref, sc_ref, sh_ref, w_ref, z_ref, hn_ref):
    @pl.when(pl.program_id(2) == 0)
    def _():
        x = x_ref[0]
        ms = jnp.mean(x * x, axis=-1, keepdims=True)
        h = x * lax.rsqrt(ms + EPS) * sc_ref[0] + sh_ref[0]
        hn_ref[...] = h.astype(jnp.bfloat16)

    z_ref[0] = jnp.dot(hn_ref[...], w_ref[...], preferred_element_type=jnp.float32)


def _norm_matmul(x, sc, sh, w, tm, tn):
    B, T, D = x.shape
    N = w.shape[1]
    bm = (lambda b, i, j: (b, 0, 0)) if sc.shape[0] == B else (lambda b, i, j: (0, 0, 0))
    return pl.pallas_call(
        _norm_matmul_kernel,
        grid=(B, T // tm, N // tn),
        in_specs=[
            pl.BlockSpec((1, tm, D), lambda b, i, j: (b, i, 0)),
            pl.BlockSpec((1, 1, D), bm),
            pl.BlockSpec((1, 1, D), bm),
            pl.BlockSpec((D, tn), lambda b, i, j: (0, j)),
        ],
        out_specs=pl.BlockSpec((1, tm, tn), lambda b, i, j: (b, i, j)),
        out_shape=jax.ShapeDtypeStruct((B, T, N), jnp.float32),
        scratch_shapes=[pltpu.VMEM((tm, D), jnp.bfloat16)],
        compiler_params=_cparams(("parallel", "parallel", "arbitrary")),
        name="norm_matmul",
    )(x, sc, sh, w)


def _group_mean_sq(x, avg):
    sq = x * x
    hi = sq.astype(jnp.bfloat16)
    lo = (sq - hi.astype(jnp.float32)).astype(jnp.bfloat16)
    return (jnp.dot(hi, avg, preferred_element_type=jnp.float32)
            + jnp.dot(lo, avg, preferred_element_type=jnp.float32))


def _head_sum(x, ones_bd):
    hi = x.astype(jnp.bfloat16)
    lo = (x - hi.astype(jnp.float32)).astype(jnp.bfloat16)
    return (jnp.dot(hi, ones_bd, preferred_element_type=jnp.float32)
            + jnp.dot(lo, ones_bd, preferred_element_type=jnp.float32))


def _rope(x, cos, sin_signed, first_half):
    partner = jnp.where(first_half, pltpu.roll(x, LANE - ROPE_FREQS, axis=1), pltpu.roll(x, ROPE_FREQS, axis=1))
    return x * cos + partner * sin_signed


def _diff_attn_kernel(lam_ref, q_ref, kx_ref, vx_ref, kc_ref, vc_ref, cosq_ref, sinq_ref, cosk_ref, sink_ref,
                      qg_ref, kg_ref, og_ref, avg64_ref, o_ref, k_s, v_s, *, n_ctx, out_scale):
    lane = lax.broadcasted_iota(jnp.int32, (1, LANE), 1)
    first_half = (lane % (2 * ROPE_FREQS)) < ROPE_FREQS
    avg64 = avg64_ref[...]

    @pl.when(pl.program_id(2) == 0)
    def _():
        kc = kc_ref[0]
        kc = kc * lax.rsqrt(_group_mean_sq(kc, avg64) + EPS) * kg_ref[...]
        k_s[0:n_ctx, :] = kc.astype(jnp.bfloat16)
        kx = kx_ref[0]
        kx = kx * lax.rsqrt(_group_mean_sq(kx, avg64) + EPS) * kg_ref[...]
        kx = _rope(kx, cosk_ref[...], sink_ref[...], first_half)
        k_s[n_ctx:, :] = kx.astype(jnp.bfloat16)
        v_s[0:n_ctx, :] = vc_ref[0].astype(jnp.bfloat16)
        v_s[n_ctx:, :] = vx_ref[0].astype(jnp.bfloat16)

    q = q_ref[0]
    q = q * lax.rsqrt(_group_mean_sq(q, avg64) + EPS) * qg_ref[...]
    q = _rope(q, cosq_ref[...], sinq_ref[...], first_half) * (DA_QK ** -0.5 * math.log2(math.e))
    k = k_s[...]
    ps = []
    for m in range(2):
        in_map = (lane // DA_QK) == m
        qm = jnp.where(in_map, q, 0.0).astype(jnp.bfloat16)
        s = lax.dot_general(qm, k, (((1,), (1,)), ((), ())), preferred_element_type=jnp.float32)
        p = jnp.exp2(s - jnp.max(s, axis=-1, keepdims=True))
        scale = (1.0 if m == 0 else lam_ref[0]) / jnp.sum(p, axis=-1, keepdims=True)
        ps.append(p * scale)
    pd = (ps[0] - ps[1]).astype(jnp.bfloat16)
    o = jnp.dot(pd, v_s[...], preferred_element_type=jnp.float32)
    o = o * lax.rsqrt(jnp.mean(o * o, axis=-1, keepdims=True) + EPS) * og_ref[...] * out_scale
    o_ref[0] = o.astype(o_ref.dtype)


def _diff_attention(lam, zx, zc, cos, sin_signed, qg, kg, og, avg64, tq, out_scale):
    B, S, _ = zx.shape
    C = zc.shape[1]
    qb, kb, vb = COL_Q // LANE, COL_K // LANE, COL_V // LANE
    kern = functools.partial(_diff_attn_kernel, n_ctx=C, out_scale=out_scale)
    const = lambda b, h, i: (0, 0)
    return pl.pallas_call(
        kern,
        grid=(B, DA_HEADS, S // tq),
        in_specs=[
            pl.BlockSpec(memory_space=pltpu.SMEM),
            pl.BlockSpec((1, tq, LANE), lambda b, h, i: (b, i, qb + h)),
            pl.BlockSpec((1, S, LANE), lambda b, h, i: (b, 0, kb + h)),
            pl.BlockSpec((1, S, LANE), lambda b, h, i: (b, 0, vb + h)),
            pl.BlockSpec((1, C, LANE), lambda b, h, i: (b, 0, kb + h)),
            pl.BlockSpec((1, C, LANE), lambda b, h, i: (b, 0, vb + h)),
            pl.BlockSpec((tq, LANE), lambda b, h, i: (i, 0)),
            pl.BlockSpec((tq, LANE), lambda b, h, i: (i, 0)),
            pl.BlockSpec((S, LANE), const),
            pl.BlockSpec((S, LANE), const),
            pl.BlockSpec((1, LANE), const),
            pl.BlockSpec((1, LANE), const),
            pl.BlockSpec((1, LANE), const),
            pl.BlockSpec((LANE, LANE), const),
        ],
        out_specs=pl.BlockSpec((1, tq, LANE), lambda b, h, i: (b, i, h)),
        out_shape=jax.ShapeDtypeStruct((B, S, DA_HEADS * DA_V), jnp.bfloat16),
        scratch_shapes=[pltpu.VMEM((C + S, LANE), jnp.bfloat16), pltpu.VMEM((C + S, LANE), jnp.bfloat16)],
        compiler_params=_cparams(("parallel", "parallel", "arbitrary")),
        name="diff_attention",
    )(lam, zx, zx, zx, zc, zc, cos, sin_signed, cos, sin_signed, qg, kg, og, avg64)


def _merge_kernel(x_ref, oa_ref, y0_ref, y1_ref, bonus_ref, g_ref, lng_ref, lnb_ref, ones_ref, ga_ref, gb_ref, g1_ref,
                  wa_ref, wb_ref, wo_ref, sc2_ref, sh2_ref, wq_ref, x1_ref, h2_ref, qp_ref):
    ones_bd = ones_ref[...]
    yy = y0_ref[0, 0] + y1_ref[0, 0]
    dlt = yy - _head_sum(yy, ones_bd) * (1.0 / RW_HEAD)
    var = _head_sum(dlt * dlt, ones_bd) * (1.0 / RW_HEAD)
    ob = (dlt * lax.rsqrt(var + RW_LN_EPS) * lng_ref[...] + lnb_ref[...] + bonus_ref[0]) * g_ref[0]
    ta = jnp.dot(oa_ref[0], wa_ref[...], preferred_element_type=jnp.float32)
    tb = jnp.dot(ob.astype(jnp.bfloat16), wb_ref[...], preferred_element_type=jnp.float32)
    mix = jax.nn.sigmoid(ga_ref[0]) * ta + jax.nn.sigmoid(gb_ref[0]) * tb
    y = jnp.dot(mix.astype(jnp.bfloat16), wo_ref[...], preferred_element_type=jnp.float32)
    x1 = x_ref[0] + g1_ref[0] * y
    x1_ref[0] = x1
    ms = jnp.mean(x1 * x1, axis=-1, keepdims=True)
    h2 = x1 * lax.rsqrt(ms + EPS) * sc2_ref[0] + sh2_ref[0]
    h2_ref[0] = h2.astype(h2_ref.dtype)
    qp_ref[0] = jnp.dot(h2.astype(jnp.bfloat16), wq_ref[...], preferred_element_type=jnp.float32)


def _merge(x, oa, y, bonus, g, ln_g, ln_b, ones_bd, zx, g1, wa, wb, wo, sc2, sh2, wq, tm, h2_dtype):
    B, S, D = x.shape
    NQ = wq.shape[1]
    tok = lambda b, i: (b, i, 0)
    per_b = lambda b, i: (b, 0, 0)
    const = lambda b, i: (0, 0)
    ga_blk, gb_blk = COL_GATE // D, COL_GATE // D + 1
    return pl.pallas_call(
        _merge_kernel,
        grid=(B, S // tm),
        in_specs=[
            pl.BlockSpec((1, tm, D), tok),
            pl.BlockSpec((1, tm, D), tok),
            pl.BlockSpec((1, 1, tm, D), lambda b, i: (0, b, i, 0)),
            pl.BlockSpec((1, 1, tm, D), lambda b, i: (1, b, i, 0)),
            pl.BlockSpec((1, tm, D), tok),
            pl.BlockSpec((1, tm, D), tok),
            pl.BlockSpec((1, D), const),
            pl.BlockSpec((1, D), const),
            pl.BlockSpec((D, D), const),
            pl.BlockSpec((1, tm, D), lambda b, i: (b, i, ga_blk)),
            pl.BlockSpec((1, tm, D), lambda b, i: (b, i, gb_blk)),
            pl.BlockSpec((1, 1, D), per_b),
            pl.BlockSpec((D, D), const),
            pl.BlockSpec((D, D), const),
            pl.BlockSpec((D, D), const),
            pl.BlockSpec((1, 1, D), per_b),
            pl.BlockSpec((1, 1, D), per_b),
            pl.BlockSpec((D, NQ), const),
        ],
        out_specs=[
            pl.BlockSpec((1, tm, D), tok),
            pl.BlockSpec((1, tm, D), tok),
            pl.BlockSpec((1, tm, NQ), tok),
        ],
        out_shape=[
            jax.ShapeDtypeStruct((B, S, D), jnp.float32),
            jax.ShapeDtypeStruct((B, S, D), h2_dtype),
            jax.ShapeDtypeStruct((B, S, NQ), jnp.float32),
        ],
        compiler_params=_cparams(("parallel", "parallel")),
        name="merge_peerq",
    )(x, oa, y, y, bonus, g, ln_g, ln_b, ones_bd, zx, zx, g1, wa, wb, wo, sc2, sh2, wq)


CHUNK = 64
QUAD = 4
QW = QUAD * RW_HEAD
SCAN_QUADS = 4


def _split3(a):
    hi = a.astype(jnp.bfloat16)
    r1 = a - hi.astype(jnp.float32)
    mid = r1.astype(jnp.bfloat16)
    lo = (r1 - mid.astype(jnp.float32)).astype(jnp.bfloat16)
    return hi, mid, lo


def _dot3(m, parts):
    acc = jnp.dot(m, parts[0], preferred_element_type=jnp.float32)
    acc = acc + jnp.dot(m, parts[1], preferred_element_type=jnp.float32)
    return acc + jnp.dot(m, parts[2], preferred_element_type=jnp.float32)


def _shift3(z, prev_row, next_row, w):
    n = z.shape[0]
    row = lax.broadcasted_iota(jnp.int32, z.shape, 0)
    zm = jnp.where(row == 0, prev_row, pltpu.roll(z, 1, axis=0))
    zp = jnp.where(row == n - 1, next_row, pltpu.roll(z, n - 1, axis=0))
    return w[0:1] * zm + w[1:2] * z + w[2:3] * zp


def _rwkv_prep_kernel(z_ref, zp_ref, zn_ref, l_ref, lp_ref, ln_ref, sw_ref, swl_ref, w0_ref, a0_ref, kk_ref, ka_ref,
                      rk_ref, wl_ref, gup_ref, ones_ref, tril_ref, triu_ref,
                      at_ref, bt_ref, kt_ref, rt_ref, bh_ref, kh_ref, v_ref, pl_ref, bonus_ref, g_ref, *, n_tiles):
    i = pl.program_id(1)
    inner_lo = 1.0 - (i == 0).astype(jnp.float32)
    inner_hi = 1.0 - (i == n_tiles - 1).astype(jnp.float32)
    zs = _shift3(z_ref[0], zp_ref[0, 7:8, :] * inner_lo, zn_ref[0, 0:1, :] * inner_hi, sw_ref[...])
    ls = _shift3(l_ref[0], lp_ref[0, 7:8, :] * inner_lo, ln_ref[0, 0:1, :] * inner_hi, swl_ref[...])
    r, k, v = zs[:, 0:RW_WIDTH], zs[:, RW_WIDTH:2 * RW_WIDTH], zs[:, 2 * RW_WIDTH:3 * RW_WIDTH]
    lane = lax.broadcasted_iota(jnp.int32, (1, LANE), 1)
    wa_in = jnp.where(lane < RW_W_LORA, jnp.tanh(ls[:, 0:LANE]), ls[:, 0:LANE]).astype(jnp.bfloat16)
    lora = jnp.dot(wa_in, wl_ref[...], preferred_element_type=jnp.float32)
    ones_bd = ones_ref[...]
    kn = k * kk_ref[...]
    kk = kn * lax.rsqrt(_head_sum(kn * kn, ones_bd) + L2_EPS)
    bonus_ref[0] = _head_sum(r * k * rk_ref[...], ones_bd) * v
    g_ref[0] = jnp.dot(jax.nn.sigmoid(ls[:, LANE:2 * LANE]).astype(jnp.bfloat16), gup_ref[...],
                       preferred_element_type=jnp.float32)
    v_ref[0] = v.astype(v_ref.dtype)
    tris = (tril_ref[...], triu_ref[...])
    for d in range(2):
        lw = lora[:, d * RW_WIDTH:(d + 1) * RW_WIDTH] + w0_ref[d:d + 1, :]
        logw = -jnp.exp(-jax.nn.softplus(-lw) - 0.5)
        a = jax.nn.sigmoid(a0_ref[d:d + 1, :] + lora[:, (2 + d) * RW_WIDTH:(3 + d) * RW_WIDTH])
        k_eff = k * (1.0 + (a - 1.0) * ka_ref[...])
        b = kk * a
        parts = _split3(logw)
        incl = _dot3(tris[d], parts)
        rest = _dot3(tris[1 - d], parts) - logw
        at_ref[d, 0] = (kk * jnp.exp(incl - logw)).astype(at_ref.dtype)
        rt_ref[d, 0] = (r * jnp.exp(incl)).astype(rt_ref.dtype)
        inv = jnp.exp(-incl)
        bt_ref[d, 0] = (b * inv).astype(bt_ref.dtype)
        kt_ref[d, 0] = (k_eff * inv).astype(kt_ref.dtype)
        to_end = jnp.exp(rest)
        bh_ref[d, 0] = (b * to_end).astype(bh_ref.dtype)
        kh_ref[d, 0] = (k_eff * to_end).astype(kh_ref.dtype)
        tot = incl + rest
        for c in range(tot.shape[0] // CHUNK):
            pl_ref[d, 0, c] = jnp.exp(tot[c * CHUNK:c * CHUNK + 1])


def _rwkv_prepare(z, sw, swl, w0, a0, k_k, k_a, r_k, wl, g_up, ones_bd, tril, triu, tt):
    B, T, _ = z.shape
    nt = T // tt
    rb, lb = COL_RW // (3 * RW_WIDTH), COL_LORA // (2 * LANE)
    tb = tt // 8
    prev = lambda i: jnp.maximum(i * tb - 1, 0)
    nxt = lambda i: jnp.minimum((i + 1) * tb, T // 8 - 1)
    const2 = lambda b, i: (0, 0)
    tok = lambda b, i: (b, i, 0)
    dtok = lambda b, i: (0, b, i, 0)
    bf = jnp.bfloat16
    kern = functools.partial(_rwkv_prep_kernel, n_tiles=nt)
    return pl.pallas_call(
        kern,
        grid=(B, nt),
        in_specs=[
            pl.BlockSpec((1, tt, 3 * RW_WIDTH), lambda b, i: (b, i, rb)),
            pl.BlockSpec((1, 8, 3 * RW_WIDTH), lambda b, i: (b, prev(i), rb)),
            pl.BlockSpec((1, 8, 3 * RW_WIDTH), lambda b, i: (b, nxt(i), rb)),
            pl.BlockSpec((1, tt, 2 * LANE), lambda b, i: (b, i, lb)),
            pl.BlockSpec((1, 8, 2 * LANE), lambda b, i: (b, prev(i), lb)),
            pl.BlockSpec((1, 8, 2 * LANE), lambda b, i: (b, nxt(i), lb)),
            pl.BlockSpec((3, 3 * RW_WIDTH), const2),
            pl.BlockSpec((3, 2 * LANE), const2),
            pl.BlockSpec((2, RW_WIDTH), const2),
            pl.BlockSpec((2, RW_WIDTH), const2),
            pl.BlockSpec((1, RW_WIDTH), const2),
            pl.BlockSpec((1, RW_WIDTH), const2),
            pl.BlockSpec((1, RW_WIDTH), const2),
            pl.BlockSpec((LANE, 4 * RW_WIDTH), const2),
            pl.BlockSpec((LANE, RW_WIDTH), const2),
            pl.BlockSpec((RW_WIDTH, RW_WIDTH), const2),
            pl.BlockSpec((tt, tt), const2),
            pl.BlockSpec((tt, tt), const2),
        ],
        out_specs=[pl.BlockSpec((2, 1, tt, RW_WIDTH), dtok)] * 6 + [
            pl.BlockSpec((1, tt, RW_WIDTH), tok),
            pl.BlockSpec((2, 1, tt // CHUNK, 1, RW_WIDTH), lambda b, i: (0, b, i, 0, 0)),
            pl.BlockSpec((1, tt, RW_WIDTH), tok),
            pl.BlockSpec((1, tt, RW_WIDTH), tok),
        ],
        out_shape=[jax.ShapeDtypeStruct((2, B, T, RW_WIDTH), bf)] * 6 + [
            jax.ShapeDtypeStruct((B, T, RW_WIDTH), bf),
            jax.ShapeDtypeStruct((2, B, T // CHUNK, 1, RW_WIDTH), jnp.float32),
            jax.ShapeDtypeStruct((B, T, RW_WIDTH), jnp.float32),
            jax.ShapeDtypeStruct((B, T, RW_WIDTH), jnp.float32),
        ],
        compiler_params=_cparams(("parallel", "parallel")),
        name="rwkv_prepare",
    )(z, z, z, z, z, z, sw, swl, w0, a0, k_k, k_a, r_k, wl, g_up, ones_bd, tril, triu)


_NT = (((1,), (1,)), ((), ()))
_TN = (((0,), (0,)), ((), ()))


def _wkv_scan_kernel(at_ref, bt_ref, kt_ref, rt_ref, bh_ref, kh_ref, v_ref, pl_ref, s0_ref, y_ref, sf_ref, s_scr,
                     *, n_chunks):
    d = pl.program_id(0)
    c = pl.program_id(3)
    f32, bf = jnp.float32, jnp.bfloat16

    @pl.when(c == 0)
    def _():
        s_scr[...] = s0_ref[0, 0]

    row = lax.broadcasted_iota(jnp.int32, (QW, QW), 0)
    col = lax.broadcasted_iota(jnp.int32, (QW, QW), 1)
    same_head = (row // RW_HEAD) == (col // RW_HEAD)
    order = (row - col) * (1 - 2 * d)
    strict = same_head & (order > 0)
    incl = same_head & (order >= 0)
    eye = (row == col).astype(f32)
    blk_sizes = [4 << i for i in range(CHUNK.bit_length() - 2)]
    same_blk = [(row // bs) == (col // bs) for bs in blk_sizes]

    def spread(t):
        return jnp.where(same_head, jnp.concatenate([t] * QUAD, axis=0), jnp.zeros((), t.dtype))

    def collapse(t):
        return sum(t[h * CHUNK:(h + 1) * CHUNK] for h in range(QUAD))

    for q in range(SCAN_QUADS):
        sl = slice(q * QW, (q + 1) * QW)
        a, b, k, r = at_ref[0, 0, :, sl], bt_ref[0, 0, :, sl], kt_ref[0, 0, :, sl], rt_ref[0, 0, :, sl]
        bh, kh, v = bh_ref[0, 0, :, sl], kh_ref[0, 0, :, sl], v_ref[0, :, sl]
        a_s, r_s, v_s = spread(a), spread(r), spread(v)
        ar = jnp.concatenate([a_s, r_s], axis=0)
        xb = lax.dot_general(ar, jnp.concatenate([b] * QUAD, axis=0), _NT, preferred_element_type=f32)
        xk = lax.dot_general(ar, jnp.concatenate([k] * QUAD, axis=0), _NT, preferred_element_type=f32)
        m_ab = jnp.where(strict, xb[:QW], 0.0)
        m_ak = jnp.where(strict, xk[:QW], 0.0)
        m_rb = jnp.where(incl, xb[QW:], 0.0).astype(bf)
        m_rk = jnp.where(incl, xk[QW:], 0.0)
        n0 = jnp.where(same_blk[0], m_ab, 0.0)
        n0b = n0.astype(bf)
        x = eye - n0
        x = x + jnp.dot(x.astype(bf), jnp.dot(n0b, n0b, preferred_element_type=f32).astype(bf),
                        preferred_element_type=f32)
        for lvl in range(1, len(same_blk)):
            e = jnp.where(same_blk[lvl] & ~same_blk[lvl - 1], m_ab, 0.0).astype(bf)
            xb16 = x.astype(bf)
            x = x - jnp.dot(jnp.dot(xb16, e, preferred_element_type=f32).astype(bf), xb16, preferred_element_type=f32)
        tb = x.astype(bf)
        w = jnp.dot(jnp.concatenate([m_ak, m_rk], axis=0).astype(bf), v_s, preferred_element_type=f32)
        uu = jnp.dot(tb, jnp.concatenate([a_s, w[:QW].astype(bf)], axis=1), preferred_element_type=f32).astype(bf)
        rr = jnp.dot(m_rb, uu, preferred_element_type=f32)
        ua, u0 = collapse(uu[:, :QW]), collapse(uu[:, QW:])
        ry = r.astype(f32) - collapse(rr[:, :QW])
        y0 = collapse(w[QW:] - rr[:, QW:])
        gm = jnp.where(same_head, lax.dot_general(ua, bh, _TN, preferred_element_type=f32), 0.0)
        hm = jnp.where(same_head, lax.dot_general(v, kh, _TN, preferred_element_type=f32)
                       - lax.dot_general(u0, bh, _TN, preferred_element_type=f32), 0.0)
        s = s_scr[q]
        sb = s.astype(bf)
        y_ref[0, 0, :, sl] = lax.dot_general(ry.astype(bf), sb, _NT, preferred_element_type=f32) + y0
        s_scr[q] = s * pl_ref[0, 0, 0, :, sl] - jnp.dot(sb, gm.astype(bf), preferred_element_type=f32) + hm

    @pl.when(c == n_chunks - 1)
    def _():
        sf_ref[0, 0] = s_scr[...]


def _wkv_scan(ops, v, pl_arr, s0):
    _, B, T, W = ops[0].shape
    nc = T // CHUNK
    hw = SCAN_QUADS * QW
    ng = W // hw
    cidx = lambda d, c: c + d * (nc - 1 - 2 * c)
    op_spec = pl.BlockSpec((1, 1, CHUNK, hw), lambda d, b, g, c: (d, b, cidx(d, c), g))
    st_spec = pl.BlockSpec((1, 1, SCAN_QUADS, QW, QW), lambda d, b, g, c: (d, b, g, 0, 0))
    kern = functools.partial(_wkv_scan_kernel, n_chunks=nc)
    return pl.pallas_call(
        kern,
        grid=(2, B, ng, nc),
        in_specs=[op_spec] * 6 + [
            pl.BlockSpec((1, CHUNK, hw), lambda d, b, g, c: (b, cidx(d, c), g)),
            pl.BlockSpec((1, 1, 1, 1, hw), lambda d, b, g, c: (d, b, cidx(d, c), 0, g)),
            st_spec,
        ],
        out_specs=[op_spec, st_spec],
        out_shape=[
            jax.ShapeDtypeStruct((2, B, T, W), jnp.float32),
            jax.ShapeDtypeStruct(s0.shape, jnp.float32),
        ],
        scratch_shapes=[pltpu.VMEM((SCAN_QUADS, QW, QW), jnp.float32)],
        compiler_params=_cparams(("parallel", "parallel", "parallel", "arbitrary")),
        name="wkv_scan",
    )(*ops, v, pl_arr, s0)


def _rwkv_constants(rw_shift, rw_w_up, rw_a_up, tt):
    bf = jnp.bfloat16
    sw, swl = rw_shift[:, :3 * RW_WIDTH], rw_shift[:, 3 * RW_WIDTH:]
    zero = jnp.zeros((RW_W_LORA, RW_WIDTH), jnp.float32)
    wl = jnp.concatenate([
        jnp.concatenate([rw_w_up[0], rw_w_up[1], zero, zero], axis=1),
        jnp.concatenate([zero, zero, rw_a_up[0], rw_a_up[1]], axis=1)], axis=0).astype(bf)
    t = jnp.arange(tt)
    same = (t[:, None] // CHUNK) == (t[None, :] // CHUNK)
    tril = (same & (t[None, :] <= t[:, None])).astype(bf)
    triu = (same & (t[None, :] >= t[:, None])).astype(bf)
    return sw, swl, wl, tril, triu


def _split_bf16(a):
    hi = a.astype(jnp.bfloat16)
    return hi, (a - hi.astype(jnp.float32)).astype(jnp.bfloat16)


def _top_rows(s, k):
    n = s.shape[0]
    row = lax.broadcasted_iota(jnp.int32, s.shape, 0)
    vals, poss = [], []
    for _ in range(k):
        m = jnp.max(s, axis=0, keepdims=True)
        pos = jnp.min(jnp.where(s == m, row, n), axis=0, keepdims=True)
        vals.append(m)
        poss.append(pos)
        s = jnp.where(row == pos, -jnp.inf, s)
    return jnp.concatenate(vals, axis=0), jnp.concatenate(poss, axis=0)


def _take_rows(table, idx):
    out = jnp.zeros(idx.shape, table.dtype)
    for a in range(table.shape[0]):
        out = jnp.where(idx == a, table[a:a + 1], out)
    return out


def _peer_topk_kernel(q_ref, keys_ref, eidx_ref, gate_ref):
    nt_dims = (((1,), (1,)), ((), ()))
    sv, si = [], []
    for p in range(2):
        q_hi, q_lo = _split_bf16(q_ref[0, :, p * N_KEYS:(p + 1) * N_KEYS])
        k_hi, k_lo = _split_bf16(keys_ref[0, p])
        s = (lax.dot_general(k_hi, q_hi, nt_dims, preferred_element_type=jnp.float32)
             + lax.dot_general(k_hi, q_lo, nt_dims, preferred_element_type=jnp.float32)
             + lax.dot_general(k_lo, q_hi, nt_dims, preferred_element_type=jnp.float32))
        v, i = _top_rows(s, PEER_TOPK)
        sv.append(v)
        si.append(i)
    cand = jnp.concatenate([sv[0][a:a + 1] + sv[1] for a in range(PEER_TOPK)], axis=0)
    top_s, pos = _top_rows(cand, PEER_TOPK)
    eidx_ref[...] = (_take_rows(si[0], pos // PEER_TOPK) * N_KEYS + _take_rows(si[1], pos % PEER_TOPK))
    p = jnp.exp(top_s - top_s[0:1])
    gate_ref[...] = p / jnp.sum(p, axis=0, keepdims=True)


def _peer_topk(qp, keys, tt):
    B, S, _ = qp.shape
    nt = S // tt
    rows = PEER_HEADS * PEER_TOPK
    out_map = lambda b, i, h: (h, b * nt + i)
    return pl.pallas_call(
        _peer_topk_kernel,
        grid=(B, nt, PEER_HEADS),
        in_specs=[
            pl.BlockSpec((1, tt, PEER_QDIM), lambda b, i, h: (b, i, h)),
            pl.BlockSpec((1, 2, N_KEYS, PEER_QDIM // 2), lambda b, i, h: (h, 0, 0, 0)),
        ],
        out_specs=[pl.BlockSpec((PEER_TOPK, tt), out_map), pl.BlockSpec((PEER_TOPK, tt), out_map)],
        out_shape=[jax.ShapeDtypeStruct((rows, B * S), jnp.int32), jax.ShapeDtypeStruct((rows, B * S), jnp.float32)],
        compiler_params=_cparams(("parallel", "parallel", "arbitrary")),
        name="peer_topk",
    )(qp, keys)


PEER_PICKS = PEER_HEADS * PEER_TOPK
GATHER_SLOTS = 4


def _gelu_tanh(a):
    return 0.5 * a * (1.0 + jnp.tanh(math.sqrt(2.0 / math.pi) * (a + 0.044715 * (a * a * a))))


def _peer_gather_kernel(eidx_hbm, h2_ref, gate_ref, x1_ref, g2_ref, tab_hbm, out_ref, idx_smem, buf, idx_sem, row_sem,
                        *, tt):
    tile = pl.program_id(0) * pl.num_programs(1) + pl.program_id(1)
    idx_copy = pltpu.make_async_copy(eidx_hbm.at[tile], idx_smem, idx_sem)
    idx_copy.start()
    idx_copy.wait()

    def issue(t, slot):
        for k in range(PEER_PICKS):
            e = idx_smem[t * PEER_PICKS + k]
            pltpu.make_async_copy(tab_hbm.at[e], buf.at[slot, :, k, :], row_sem.at[slot]).start(priority=k % 2)

    def wait_rows(slot):
        pltpu.make_async_copy(buf.at[slot], buf.at[slot], row_sem.at[slot]).wait()

    lane = lax.broadcasted_iota(jnp.int32, (PEER_PICKS, tt), 1)
    n_seg = buf.shape[1]

    def compute(t, slot):
        xrow = h2_ref[0, pl.ds(t, 1), :]
        prod = None
        for s in range(n_seg):
            u = pltpu.bitcast(buf[slot, s] & jnp.uint32(0xFFFF0000), jnp.float32)
            term = u * xrow[:, s * LANE:(s + 1) * LANE]
            prod = term if prod is None else prod + term
        act = jnp.sum(prod, axis=1, keepdims=True)
        gate = jnp.sum(jnp.where(lane == t, gate_ref[...], 0.0), axis=1, keepdims=True)
        coef = gate * _gelu_tanh(act)
        ys = [jnp.sum(coef * pltpu.bitcast(buf[slot, s] << 16, jnp.float32), axis=0, keepdims=True)
              for s in range(n_seg)]
        out_ref[0, pl.ds(t, 1), :] = x1_ref[0, pl.ds(t, 1), :] + g2_ref[0] * jnp.concatenate(ys, axis=1)

    for j in range(GATHER_SLOTS - 1):
        issue(j, j)

    def group(g, carry):
        for j in range(GATHER_SLOTS):
            t = g * GATHER_SLOTS + j
            ahead = t + GATHER_SLOTS - 1

            @pl.when(ahead < tt)
            def _():
                issue(ahead, (j + GATHER_SLOTS - 1) % GATHER_SLOTS)

            wait_rows(j)
            compute(t, j)
        return carry

    lax.fori_loop(0, tt // GATHER_SLOTS, group, 0)


def _peer_gather(eidx_t, h2, gate_t, x1, g2, tab, tt):
    B, S, D = x1.shape
    nt = S // tt
    tok = lambda b, i: (b, i, 0)
    kern = functools.partial(_peer_gather_kernel, tt=tt)
    return pl.pallas_call(
        kern,
        grid=(B, nt),
        in_specs=[
            pl.BlockSpec(memory_space=pl.ANY),
            pl.BlockSpec((1, tt, D), tok),
            pl.BlockSpec((PEER_PICKS, tt), lambda b, i: (0, b * nt + i)),
            pl.BlockSpec((1, tt, D), tok),
            pl.BlockSpec((1, 1, D), lambda b, i: (b, 0, 0)),
            pl.BlockSpec(memory_space=pl.ANY),
        ],
        out_specs=pl.BlockSpec((1, tt, D), tok),
        out_shape=jax.ShapeDtypeStruct((B, S, D), jnp.float32),
        scratch_shapes=[
            pltpu.SMEM((tt * PEER_PICKS,), jnp.int32),
            pltpu.VMEM((GATHER_SLOTS, D // LANE, PEER_PICKS, LANE), jnp.uint32),
            pltpu.SemaphoreType.DMA(()),
            pltpu.SemaphoreType.DMA((GATHER_SLOTS,)),
        ],
        compiler_params=_cparams(("arbitrary", "arbitrary")),
        name="peer_gather",
    )(eidx_t, h2, gate_t, x1, g2, tab)


def _pack_expert_table(peer_u, peer_v):
    u16 = lax.bitcast_convert_type(peer_u.astype(jnp.bfloat16), jnp.uint16).astype(jnp.uint32)
    v16 = lax.bitcast_convert_type(peer_v.astype(jnp.bfloat16), jnp.uint16).astype(jnp.uint32)
    return ((u16 << 16) | v16).reshape(peer_u.shape[0], peer_u.shape[1] // LANE, LANE)


def _rope_tables(n_tokens):
    rows = n_tokens // GRID_W
    row = jnp.repeat(jnp.arange(rows, dtype=jnp.float32), GRID_W)
    col = jnp.tile(jnp.arange(GRID_W, dtype=jnp.float32), rows)
    inv = ROPE_THETA ** (-jnp.arange(ROPE_FREQS, dtype=jnp.float32) / ROPE_FREQS)
    ang = jnp.stack([row[:, None] * inv, col[:, None] * inv], axis=1)
    cos = jnp.cos(ang)[:, None, :, None, :]
    sin = jnp.sin(ang)[:, None, :, None, :]
    cos = jnp.broadcast_to(cos, (n_tokens, 2, 2, 2, ROPE_FREQS)).reshape(n_tokens, LANE)
    sgn = jnp.array([-1.0, 1.0], jnp.float32)[None, None, None, :, None]
    sin = jnp.broadcast_to(sin * sgn, (n_tokens, 2, 2, 2, ROPE_FREQS)).reshape(n_tokens, LANE)
    return cos, sin


def kernel(x, c, ctx, c_ctx, w_mod, b_mod, norm1_g, w_in, q_norm_g, k_norm_g, diff_lambda, diff_out_g, rw_shift,
           rw_w0, rw_w_up, rw_a0, rw_a_up, rw_g_up, rw_k_k, rw_k_a, rw_r_k, rw_ln_g, rw_ln_b, w_branch_a,
           w_branch_b, w_out, norm2_g, peer_wq, peer_keys, peer_u, peer_v):
    assert w_mod.shape[0] == 1, "single-layer trunk only"
    B, S, D = x.shape
    C = ctx.shape[1]
    bf = jnp.bfloat16
    lam_init = 0.8 - 0.6 * math.exp(-0.3 * 0)

    mod_x = (jax.nn.silu(c) @ w_mod[0] + b_mod[0]).reshape(B, N_MOD, 1, D)
    mod_c = (jax.nn.silu(c_ctx) @ w_mod[0] + b_mod[0]).reshape(1, N_MOD, 1, D)
    sc1x, sh1x = norm1_g[0] * (1 + mod_x[:, 1]), mod_x[:, 0]
    sc1c, sh1c = norm1_g[0] * (1 + mod_c[:, 1]), mod_c[:, 0]
    sc2x, sh2x = norm2_g[0] * (1 + mod_x[:, 4]), mod_x[:, 3]
    g1x, g2x = mod_x[:, 2], mod_x[:, 5]

    w_in_p = jnp.concatenate([w_in[0][:, :6144], w_in[0][:, 6400:], w_in[0][:, 6144:6400]], axis=1).astype(bf)
    tm = min(1024, S)
    zx = _norm_matmul(x, sc1x, sh1x, w_in_p, tm, 768)
    zc = _norm_matmul(ctx, sc1c, sh1c, w_in_p, min(tm, C), 768)

    lam = (jnp.exp(jnp.sum(diff_lambda[0, 0] * diff_lambda[0, 1])) - jnp.exp(jnp.sum(diff_lambda[0, 2] * diff_lambda[0, 3]))
           + lam_init).reshape(1)
    cos, sin_signed = _rope_tables(S)
    qg = jnp.tile(q_norm_g[0], 2).reshape(1, LANE)
    kg = jnp.tile(k_norm_g[0], 2).reshape(1, LANE)
    og = diff_out_g[0].reshape(1, LANE)
    grp = jnp.arange(LANE) // DA_QK
    avg64 = jnp.where(grp[:, None] == grp[None, :], 1.0 / DA_QK, 0.0).astype(bf)
    o_a = _diff_attention(lam, zx, zc, cos, sin_signed, qg, kg, og, avg64, min(256, S), 1.0 - lam_init)

    hd = jnp.arange(RW_WIDTH) // RW_HEAD
    ones_bd = (hd[:, None] == hd[None, :]).astype(bf)
    row1 = lambda t: t.reshape(1, RW_WIDTH)
    prep = {}
    for name, z, T in (("ctx", zc, C), ("x", zx, S)):
        tt = min(256, T)
        sw, swl, wl, tril, triu = _rwkv_constants(rw_shift[0], rw_w_up[0], rw_a_up[0], tt)
        prep[name] = _rwkv_prepare(z, sw, swl, rw_w0[0], rw_a0[0], row1(rw_k_k[0]), row1(rw_k_a[0]), row1(rw_r_k[0]),
                                   wl, rw_g_up[0].astype(bf), ones_bd, tril, triu, tt)
    s0 = jnp.zeros((2, B, RW_WIDTH // QW, QW, QW), jnp.float32)
    pc, px = prep["ctx"], prep["x"]
    _, s_ctx = _wkv_scan(pc[0:6], pc[6], pc[7], s0)
    y_rw, _ = _wkv_scan(px[0:6], px[6], px[7], s_ctx)

    x1, h2, qp = _merge(x, o_a, y_rw, px[8], px[9], row1(rw_ln_g[0]), row1(rw_ln_b[0]), ones_bd, zx, g1x,
                        w_branch_a[0].astype(bf), w_branch_b[0].astype(bf), w_out[0].astype(bf),
                        sc2x, sh2x, peer_wq[0].astype(bf), min(256, S), jnp.float32)

    tt_k = min(256, S)
    eidx_t, gate_t = _peer_topk(qp, peer_keys[0], tt_k)
    tt_g = min(128, S)
    eidx = eidx_t.T.reshape(B * S // tt_g, tt_g * PEER_PICKS)
    tab = _pack_expert_table(peer_u[0], peer_v[0])
    return _peer_gather(eidx, h2, gate_t, x1, g2x, tab, tt_g)
```

```python
import functools
import math

import jax
import jax.numpy as jnp
from jax import lax
from jax.experimental import pallas as pl
from jax.experimental.pallas import tpu as pltpu

D_MODEL = 1024
N_MOD = 6
EPS = 1e-6
GRID_W = 64
DA_HEADS = 8
DA_QK = 64
DA_V = 2 * DA_QK
ROPE_THETA = 10000.0
ROPE_FREQS = DA_QK // 4
RW_HEADS = 16
RW_HEAD = 64
RW_WIDTH = RW_HEADS * RW_HEAD
RW_W_LORA = 64
RW_A_LORA = 64
RW_G_LORA = 128
RW_LN_EPS = 64e-5
L2_EPS = 1e-12
PEER_HEADS = 8
N_KEYS = 128
PEER_TOPK = 16
PEER_QDIM = 256
PEER_CHUNK = 128

COL_Q = 0
COL_K = 1024
COL_V = 2048
COL_RW = 3072
COL_GATE = 6144
COL_LORA = 8192
IN_WIDTH = 8448

LANE = 128
VMEM_LIMIT = 56 * 1024 * 1024


def _cparams(sem):
    return pltpu.CompilerParams(dimension_semantics=sem, vmem_limit_bytes=VMEM_LIMIT)


def _norm_matmul_kernel(x_ref, sc_ref, sh_ref, w_ref, z_ref, hn_ref):
    @pl.when(pl.program_id(2) == 0)
    def _():
        x = x_ref[0]
        ms = jnp.mean(x * x, axis=-1, keepdims=True)
        h = x * lax.rsqrt(ms + EPS) * sc_ref[0] + sh_ref[0]
        hn_ref[...] = h.astype(jnp.bfloat16)

    z_ref[0] = jnp.dot(hn_ref[...], w_ref[...], preferred_element_type=jnp.float32)


def _norm_matmul(x, sc, sh, w, tm, tn):
    B, T, D = x.shape
    N = w.shape[1]
    bm = (lambda b, i, j: (b, 0, 0)) if sc.shape[0] == B else (lambda b, i, j: (0, 0, 0))
    return pl.pallas_call(
        _norm_matmul_kernel,
        grid=(B, T // tm, N // tn),
        in_specs=[
            pl.BlockSpec((1, tm, D), lambda b, i, j: (b, i, 0)),
            pl.BlockSpec((1, 1, D), bm),
            pl.BlockSpec((1, 1, D), bm),
            pl.BlockSpec((D, tn), lambda b, i, j: (0, j)),
        ],
        out_specs=pl.BlockSpec((1, tm, tn), lambda b, i, j: (b, i, j)),
        out_shape=jax.ShapeDtypeStruct((B, T, N), jnp.float32),
        scratch_shapes=[pltpu.VMEM((tm, D), jnp.bfloat16)],
        compiler_params=_cparams(("parallel", "parallel", "arbitrary")),
        name="norm_matmul",
    )(x, sc, sh, w)


def _group_mean_sq(x, avg):
    sq = x * x
    hi = sq.astype(jnp.bfloat16)
    lo = (sq - hi.astype(jnp.float32)).astype(jnp.bfloat16)
    return (jnp.dot(hi, avg, preferred_element_type=jnp.float32)
            + jnp.dot(lo, avg, preferred_element_type=jnp.float32))


def _head_sum(x, ones_bd):
    hi = x.astype(jnp.bfloat16)
    lo = (x - hi.astype(jnp.float32)).astype(jnp.bfloat16)
    return (jnp.dot(hi, ones_bd, preferred_element_type=jnp.float32)
            + jnp.dot(lo, ones_bd, preferred_element_type=jnp.float32))


def _rope(x, cos, sin_signed, first_half):
    partner = jnp.where(first_half, pltpu.roll(x, LANE - ROPE_FREQS, axis=1), pltpu.roll(x, ROPE_FREQS, axis=1))
    return x * cos + partner * sin_signed


def _diff_attn_kernel(lam_ref, q_ref, kx_ref, vx_ref, kc_ref, vc_ref, cosq_ref, sinq_ref, cosk_ref, sink_ref,
                      qg_ref, kg_ref, og_ref, avg64_ref, o_ref, k_s, v_s, *, n_ctx, out_scale):
    lane = lax.broadcasted_iota(jnp.int32, (1, LANE), 1)
    first_half = (lane % (2 * ROPE_FREQS)) < ROPE_FREQS
    avg64 = avg64_ref[...]

    @pl.when(pl.program_id(2) == 0)
    def _():
        kc = kc_ref[0]
        kc = kc * lax.rsqrt(_group_mean_sq(kc, avg64) + EPS) * kg_ref[...]
        k_s[0:n_ctx, :] = kc.astype(jnp.bfloat16)
        kx = kx_ref[0]
        kx = kx * lax.rsqrt(_group_mean_sq(kx, avg64) + EPS) * kg_ref[...]
        kx = _rope(kx, cosk_ref[...], sink_ref[...], first_half)
        k_s[n_ctx:, :] = kx.astype(jnp.bfloat16)
        v_s[0:n_ctx, :] = vc_ref[0].astype(jnp.bfloat16)
        v_s[n_ctx:, :] = vx_ref[0].astype(jnp.bfloat16)

    q = q_ref[0]
    q = q * lax.rsqrt(_group_mean_sq(q, avg64) + EPS) * qg_ref[...]
    q = _rope(q, cosq_ref[...], sinq_ref[...], first_half) * (DA_QK ** -0.5 * math.log2(math.e))
    k = k_s[...]
    ps = []
    for m in range(2):
        in_map = (lane // DA_QK) == m
        qm = jnp.where(in_map, q, 0.0).astype(jnp.bfloat16)
        s = lax.dot_general(qm, k, (((1,), (1,)), ((), ())), preferred_element_type=jnp.float32)
        p = jnp.exp2(s - jnp.max(s, axis=-1, keepdims=True))
        scale = (1.0 if m == 0 else lam_ref[0]) / jnp.sum(p, axis=-1, keepdims=True)
        ps.append(p * scale)
    pd = (ps[0] - ps[1]).astype(jnp.bfloat16)
    o = jnp.dot(pd, v_s[...], preferred_element_type=jnp.float32)
    o = o * lax.rsqrt(jnp.mean(o * o, axis=-1, keepdims=True) + EPS) * og_ref[...] * out_scale
    o_ref[0] = o.astype(o_ref.dtype)


def _diff_attention(lam, zx, zc, cos, sin_signed, qg, kg, og, avg64, tq, out_scale):
    B, S, _ = zx.shape
    C = zc.shape[1]
    qb, kb, vb = COL_Q // LANE, COL_K // LANE, COL_V // LANE
    kern = functools.partial(_diff_attn_kernel, n_ctx=C, out_scale=out_scale)
    const = lambda b, h, i: (0, 0)
    return pl.pallas_call(
        kern,
        grid=(B, DA_HEADS, S // tq),
        in_specs=[
            pl.BlockSpec(memory_space=pltpu.SMEM),
            pl.BlockSpec((1, tq, LANE), lambda b, h, i: (b, i, qb + h)),
            pl.BlockSpec((1, S, LANE), lambda b, h, i: (b, 0, kb + h)),
            pl.BlockSpec((1, S, LANE), lambda b, h, i: (b, 0, vb + h)),
            pl.BlockSpec((1, C, LANE), lambda b, h, i: (b, 0, kb + h)),
            pl.BlockSpec((1, C, LANE), lambda b, h, i: (b, 0, vb + h)),
            pl.BlockSpec((tq, LANE), lambda b, h, i: (i, 0)),
            pl.BlockSpec((tq, LANE), lambda b, h, i: (i, 0)),
            pl.BlockSpec((S, LANE), const),
            pl.BlockSpec((S, LANE), const),
            pl.BlockSpec((1, LANE), const),
            pl.BlockSpec((1, LANE), const),
            pl.BlockSpec((1, LANE), const),
            pl.BlockSpec((LANE, LANE), const),
        ],
        out_specs=pl.BlockSpec((1, tq, LANE), lambda b, h, i: (b, i, h)),
        out_shape=jax.ShapeDtypeStruct((B, S, DA_HEADS * DA_V), jnp.bfloat16),
        scratch_shapes=[pltpu.VMEM((C + S, LANE), jnp.bfloat16), pltpu.VMEM((C + S, LANE), jnp.bfloat16)],
        compiler_params=_cparams(("parallel", "parallel", "arbitrary")),
        name="diff_attention",
    )(lam, zx, zx, zx, zc, zc, cos, sin_signed, cos, sin_signed, qg, kg, og, avg64)


def _merge_kernel(x_ref, oa_ref, y0_ref, y1_ref, bonus_ref, g_ref, lng_ref, lnb_ref, ones_ref, ga_ref, gb_ref, g1_ref,
                  wa_ref, wb_ref, wo_ref, sc2_ref, sh2_ref, wq_ref, x1_ref, h2_ref, qp_ref):
    ones_bd = ones_ref[...]
    yy = y0_ref[0, 0] + y1_ref[0, 0]
    dlt = yy - _head_sum(yy, ones_bd) * (1.0 / RW_HEAD)
    var = _head_sum(dlt * dlt, ones_bd) * (1.0 / RW_HEAD)
    ob = (dlt * lax.rsqrt(var + RW_LN_EPS) * lng_ref[...] + lnb_ref[...] + bonus_ref[0]) * g_ref[0]
    ta = jnp.dot(oa_ref[0], wa_ref[...], preferred_element_type=jnp.float32)
    tb = jnp.dot(ob.astype(jnp.bfloat16), wb_ref[...], preferred_element_type=jnp.float32)
    mix = jax.nn.sigmoid(ga_ref[0]) * ta + jax.nn.sigmoid(gb_ref[0]) * tb
    y = jnp.dot(mix.astype(jnp.bfloat16), wo_ref[...], preferred_element_type=jnp.float32)
    x1 = x_ref[0] + g1_ref[0] * y
    x1_ref[0] = x1
    ms = jnp.mean(x1 * x1, axis=-1, keepdims=True)
    h2 = x1 * lax.rsqrt(ms + EPS) * sc2_ref[0] + sh2_ref[0]
    h2_ref[0] = h2.astype(h2_ref.dtype)
    qp_ref[0] = jnp.dot(h2.astype(jnp.bfloat16), wq_ref[...], preferred_element_type=jnp.float32)


def _merge(x, oa, y, bonus, g, ln_g, ln_b, ones_bd, zx, g1, wa, wb, wo, sc2, sh2, wq, tm, h2_dtype):
    B, S, D = x.shape
    NQ = wq.shape[1]
    tok = lambda b, i: (b, i, 0)
    per_b = lambda b, i: (b, 0, 0)
    const = lambda b, i: (0, 0)
    ga_blk, gb_blk = COL_GATE // D, COL_GATE // D + 1
    return pl.pallas_call(
        _merge_kernel,
        grid=(B, S // tm),
        in_specs=[
            pl.BlockSpec((1, tm, D), tok),
            pl.BlockSpec((1, tm, D), tok),
            pl.BlockSpec((1, 1, tm, D), lambda b, i: (0, b, i, 0)),
            pl.BlockSpec((1, 1, tm, D), lambda b, i: (1, b, i, 0)),
            pl.BlockSpec((1, tm, D), tok),
            pl.BlockSpec((1, tm, D), tok),
            pl.BlockSpec((1, D), const),
            pl.BlockSpec((1, D), const),
            pl.BlockSpec((D, D), const),
            pl.BlockSpec((1, tm, D), lambda b, i: (b, i, ga_blk)),
            pl.BlockSpec((1, tm, D), lambda b, i: (b, i, gb_blk)),
            pl.BlockSpec((1, 1, D), per_b),
            pl.BlockSpec((D, D), const),
            pl.BlockSpec((D, D), const),
            pl.BlockSpec((D, D), const),
            pl.BlockSpec((1, 1, D), per_b),
            pl.BlockSpec((1, 1, D), per_b),
            pl.BlockSpec((D, NQ), const),
        ],
        out_specs=[
            pl.BlockSpec((1, tm, D), tok),
            pl.BlockSpec((1, tm, D), tok),
            pl.BlockSpec((1, tm, NQ), tok),
        ],
        out_shape=[
            jax.ShapeDtypeStruct((B, S, D), jnp.float32),
            jax.ShapeDtypeStruct((B, S, D), h2_dtype),
            jax.ShapeDtypeStruct((B, S, NQ), jnp.float32),
        ],
        compiler_params=_cparams(("parallel", "parallel")),
        name="merge_peerq",
    )(x, oa, y, y, bonus, g, ln_g, ln_b, ones_bd, zx, zx, g1, wa, wb, wo, sc2, sh2, wq)


CHUNK = 64
QUAD = 4
QW = QUAD * RW_HEAD
SCAN_QUADS = 4


def _split3(a):
    hi = a.astype(jnp.bfloat16)
    r1 = a - hi.astype(jnp.float32)
    mid = r1.astype(jnp.bfloat16)
    lo = (r1 - mid.astype(jnp.float32)).astype(jnp.bfloat16)
    return hi, mid, lo


def _dot3(m, parts):
    acc = jnp.dot(m, parts[0], preferred_element_type=jnp.float32)
    acc = acc + jnp.dot(m, parts[1], preferred_element_type=jnp.float32)
    return acc + jnp.dot(m, parts[2], preferred_element_type=jnp.float32)


def _shift3(z, prev_row, next_row, w):
    n = z.shape[0]
    row = lax.broadcasted_iota(jnp.int32, z.shape, 0)
    zm = jnp.where(row == 0, prev_row, pltpu.roll(z, 1, axis=0))
    zp = jnp.where(row == n - 1, next_row, pltpu.roll(z, n - 1, axis=0))
    return w[0:1] * zm + w[1:2] * z + w[2:3] * zp


def _rwkv_prep_kernel(z_ref, zp_ref, zn_ref, l_ref, lp_ref, ln_ref, sw_ref, swl_ref, w0_ref, a0_ref, kk_ref, ka_ref,
                      rk_ref, wl_ref, gup_ref, ones_ref, tril_ref, triu_ref,
                      at_ref, bt_ref, kt_ref, rt_ref, bh_ref, kh_ref, v_ref, pl_ref, bonus_ref, g_ref, *, n_tiles):
    i = pl.program_id(1)
    inner_lo = 1.0 - (i == 0).astype(jnp.float32)
    inner_hi = 1.0 - (i == n_tiles - 1).astype(jnp.float32)
    zs = _shift3(z_ref[0], zp_ref[0, 7:8, :] * inner_lo, zn_ref[0, 0:1, :] * inner_hi, sw_ref[...])
    ls = _shift3(l_ref[0], lp_ref[0, 7:8, :] * inner_lo, ln_ref[0, 0:1, :] * inner_hi, swl_ref[...])
    r, k, v = zs[:, 0:RW_WIDTH], zs[:, RW_WIDTH:2 * RW_WIDTH], zs[:, 2 * RW_WIDTH:3 * RW_WIDTH]
    lane = lax.broadcasted_iota(jnp.int32, (1, LANE), 1)
    wa_in = jnp.where(lane < RW_W_LORA, jnp.tanh(ls[:, 0:LANE]), ls[:, 0:LANE]).astype(jnp.bfloat16)
    lora = jnp.dot(wa_in, wl_ref[...], preferred_element_type=jnp.float32)
    ones_bd = ones_ref[...]
    kn = k * kk_ref[...]
    kk = kn * lax.rsqrt(_head_sum(kn * kn, ones_bd) + L2_EPS)
    bonus_ref[0] = _head_sum(r * k * rk_ref[...], ones_bd) * v
    g_ref[0] = jnp.dot(jax.nn.sigmoid(ls[:, LANE:2 * LANE]).astype(jnp.bfloat16), gup_ref[...],
                       preferred_element_type=jnp.float32)
    v_ref[0] = v.astype(v_ref.dtype)
    tris = (tril_ref[...], triu_ref[...])
    for d in range(2):
        lw = lora[:, d * RW_WIDTH:(d + 1) * RW_WIDTH] + w0_ref[d:d + 1, :]
        logw = -jnp.exp(-jax.nn.softplus(-lw) - 0.5)
        a = jax.nn.sigmoid(a0_ref[d:d + 1, :] + lora[:, (2 + d) * RW_WIDTH:(3 + d) * RW_WIDTH])
        k_eff = k * (1.0 + (a - 1.0) * ka_ref[...])
        b = kk * a
        parts = _split3(logw)
        incl = _dot3(tris[d], parts)
        rest = _dot3(tris[1 - d], parts) - logw
        at_ref[d, 0] = (kk * jnp.exp(incl - logw)).astype(at_ref.dtype)
        rt_ref[d, 0] = (r * jnp.exp(incl)).astype(rt_ref.dtype)
        inv = jnp.exp(-incl)
        bt_ref[d, 0] = (b * inv).astype(bt_ref.dtype)
        kt_ref[d, 0] = (k_eff * inv).astype(kt_ref.dtype)
        to_end = jnp.exp(rest)
        bh_ref[d, 0] = (b * to_end).astype(bh_ref.dtype)
        kh_ref[d, 0] = (k_eff * to_end).astype(kh_ref.dtype)
        tot = incl + rest
        for c in range(tot.shape[0] // CHUNK):
            pl_ref[d, 0, c] = jnp.exp(tot[c * CHUNK:c * CHUNK + 1])


def _rwkv_prepare(z, sw, swl, w0, a0, k_k, k_a, r_k, wl, g_up, ones_bd, tril, triu, tt):
    B, T, _ = z.shape
    nt = T // tt
    rb, lb = COL_RW // (3 * RW_WIDTH), COL_LORA // (2 * LANE)
    tb = tt // 8
    prev = lambda i: jnp.maximum(i * tb - 1, 0)
    nxt = lambda i: jnp.minimum((i + 1) * tb, T // 8 - 1)
    const2 = lambda b, i: (0, 0)
    tok = lambda b, i: (b, i, 0)
    dtok = lambda b, i: (0, b, i, 0)
    bf = jnp.bfloat16
    kern = functools.partial(_rwkv_prep_kernel, n_tiles=nt)
    return pl.pallas_call(
        kern,
        grid=(B, nt),
        in_specs=[
            pl.BlockSpec((1, tt, 3 * RW_WIDTH), lambda b, i: (b, i, rb)),
            pl.BlockSpec((1, 8, 3 * RW_WIDTH), lambda b, i: (b, prev(i), rb)),
            pl.BlockSpec((1, 8, 3 * RW_WIDTH), lambda b, i: (b, nxt(i), rb)),
            pl.BlockSpec((1, tt, 2 * LANE), lambda b, i: (b, i, lb)),
            pl.BlockSpec((1, 8, 2 * LANE), lambda b, i: (b, prev(i), lb)),
            pl.BlockSpec((1, 8, 2 * LANE), lambda b, i: (b, nxt(i), lb)),
            pl.BlockSpec((3, 3 * RW_WIDTH), const2),
            pl.BlockSpec((3, 2 * LANE), const2),
            pl.BlockSpec((2, RW_WIDTH), const2),
            pl.BlockSpec((2, RW_WIDTH), const2),
            pl.BlockSpec((1, RW_WIDTH), const2),
            pl.BlockSpec((1, RW_WIDTH), const2),
            pl.BlockSpec((1, RW_WIDTH), const2),
            pl.BlockSpec((LANE, 4 * RW_WIDTH), const2),
            pl.BlockSpec((LANE, RW_WIDTH), const2),
            pl.BlockSpec((RW_WIDTH, RW_WIDTH), const2),
            pl.BlockSpec((tt, tt), const2),
            pl.BlockSpec((tt, tt), const2),
        ],
        out_specs=[pl.BlockSpec((2, 1, tt, RW_WIDTH), dtok)] * 6 + [
            pl.BlockSpec((1, tt, RW_WIDTH), tok),
            pl.BlockSpec((2, 1, tt // CHUNK, 1, RW_WIDTH), lambda b, i: (0, b, i, 0, 0)),
            pl.BlockSpec((1, tt, RW_WIDTH), tok),
            pl.BlockSpec((1, tt, RW_WIDTH), tok),
        ],
        out_shape=[jax.ShapeDtypeStruct((2, B, T, RW_WIDTH), bf)] * 6 + [
            jax.ShapeDtypeStruct((B, T, RW_WIDTH), bf),
            jax.ShapeDtypeStruct((2, B, T // CHUNK, 1, RW_WIDTH), jnp.float32),
            jax.ShapeDtypeStruct((B, T, RW_WIDTH), jnp.float32),
            jax.ShapeDtypeStruct((B, T, RW_WIDTH), jnp.float32),
        ],
        compiler_params=_cparams(("parallel", "parallel")),
        name="rwkv_prepare",
    )(z, z, z, z, z, z, sw, swl, w0, a0, k_k, k_a, r_k, wl, g_up, ones_bd, tril, triu)


_NT = (((1,), (1,)), ((), ()))
_TN = (((0,), (0,)), ((), ()))


def _wkv_scan_kernel(at_ref, bt_ref, kt_ref, rt_ref, bh_ref, kh_ref, v_ref, pl_ref, s0_ref, y_ref, sf_ref, s_scr,
                     *, n_chunks):
    d = pl.program_id(0)
    c = pl.program_id(3)
    f32, bf = jnp.float32, jnp.bfloat16

    @pl.when(c == 0)
    def _():
        s_scr[...] = s0_ref[0, 0]

    row = lax.broadcasted_iota(jnp.int32, (QW, QW), 0)
    col = lax.broadcasted_iota(jnp.int32, (QW, QW), 1)
    same_head = (row // RW_HEAD) == (col // RW_HEAD)
    order = (row - col) * (1 - 2 * d)
    strict = same_head & (order > 0)
    incl = same_head & (order >= 0)
    eye = (row == col).astype(f32)
    blk_sizes = [4 << i for i in range(CHUNK.bit_length() - 2)]
    same_blk = [(row // bs) == (col // bs) for bs in blk_sizes]

    def spread(t):
        return jnp.where(same_head, jnp.concatenate([t] * QUAD, axis=0), jnp.zeros((), t.dtype))

    def collapse(t):
        return sum(t[h * CHUNK:(h + 1) * CHUNK] for h in range(QUAD))

    for q in range(SCAN_QUADS):
        sl = slice(q * QW, (q + 1) * QW)
        a, b, k, r = at_ref[0, 0, :, sl], bt_ref[0, 0, :, sl], kt_ref[0, 0, :, sl], rt_ref[0, 0, :, sl]
        bh, kh, v = bh_ref[0, 0, :, sl], kh_ref[0, 0, :, sl], v_ref[0, :, sl]
        a_s, r_s, v_s = spread(a), spread(r), spread(v)
        ar = jnp.concatenate([a_s, r_s], axis=0)
        xb = lax.dot_general(ar, jnp.concatenate([b] * QUAD, axis=0), _NT, preferred_element_type=f32)
        xk = lax.dot_general(ar, jnp.concatenate([k] * QUAD, axis=0), _NT, preferred_element_type=f32)
        m_ab = jnp.where(strict, xb[:QW], 0.0)
        m_ak = jnp.where(strict, xk[:QW], 0.0)
        m_rb = jnp.where(incl, xb[QW:], 0.0).astype(bf)
        m_rk = jnp.where(incl, xk[QW:], 0.0)
        n0 = jnp.where(same_blk[0], m_ab, 0.0)
        n0b = n0.astype(bf)
        x = eye - n0
        x = x + jnp.dot(x.astype(bf), jnp.dot(n0b, n0b, preferred_element_type=f32).astype(bf),
                        preferred_element_type=f32)
        for lvl in range(1, len(same_blk)):
            e = jnp.where(same_blk[lvl] & ~same_blk[lvl - 1], m_ab, 0.0).astype(bf)
            xb16 = x.astype(bf)
            x = x - jnp.dot(jnp.dot(xb16, e, preferred_element_type=f32).astype(bf), xb16, preferred_element_type=f32)
        tb = x.astype(bf)
        w = jnp.dot(jnp.concatenate([m_ak, m_rk], axis=0).astype(bf), v_s, preferred_element_type=f32)
        uu = jnp.dot(tb, jnp.concatenate([a_s, w[:QW].astype(bf)], axis=1), preferred_element_type=f32).astype(bf)
        rr = jnp.dot(m_rb, uu, preferred_element_type=f32)
        ua, u0 = collapse(uu[:, :QW]), collapse(uu[:, QW:])
        ry = r.astype(f32) - collapse(rr[:, :QW])
        y0 = collapse(w[QW:] - rr[:, QW:])
        gm = jnp.where(same_head, lax.dot_general(ua, bh, _TN, preferred_element_type=f32), 0.0)
        hm = jnp.where(same_head, lax.dot_general(v, kh, _TN, preferred_element_type=f32)
                       - lax.dot_general(u0, bh, _TN, preferred_element_type=f32), 0.0)
        s = s_scr[q]
        sb = s.astype(bf)
        y_ref[0, 0, :, sl] = lax.dot_general(ry.astype(bf), sb, _NT, preferred_element_type=f32) + y0
        s_scr[q] = s * pl_ref[0, 0, 0, :, sl] - jnp.dot(sb, gm.astype(bf), preferred_element_type=f32) + hm

    @pl.when(c == n_chunks - 1)
    def _():
        sf_ref[0, 0] = s_scr[...]


def _wkv_scan(ops, v, pl_arr, s0):
    _, B, T, W = ops[0].shape
    nc = T // CHUNK
    hw = SCAN_QUADS * QW
    ng = W // hw
    cidx = lambda d, c: c + d * (nc - 1 - 2 * c)
    op_spec = pl.BlockSpec((1, 1, CHUNK, hw), lambda d, b, g, c: (d, b, cidx(d, c), g))
    st_spec = pl.BlockSpec((1, 1, SCAN_QUADS, QW, QW), lambda d, b, g, c: (d, b, g, 0, 0))
    kern = functools.partial(_wkv_scan_kernel, n_chunks=nc)
    return pl.pallas_call(
        kern,
        grid=(2, B, ng, nc),
        in_specs=[op_spec] * 6 + [
            pl.BlockSpec((1, CHUNK, hw), lambda d, b, g, c: (b, cidx(d, c), g)),
            pl.BlockSpec((1, 1, 1, 1, hw), lambda d, b, g, c: (d, b, cidx(d, c), 0, g)),
            st_spec,
        ],
        out_specs=[op_spec, st_spec],
        out_shape=[
            jax.ShapeDtypeStruct((2, B, T, W), jnp.float32),
            jax.ShapeDtypeStruct(s0.shape, jnp.float32),
        ],
        scratch_shapes=[pltpu.VMEM((SCAN_QUADS, QW, QW), jnp.float32)],
        compiler_params=_cparams(("parallel", "parallel", "parallel", "arbitrary")),
        name="wkv_scan",
    )(*ops, v, pl_arr, s0)


def _rwkv_constants(rw_shift, rw_w_up, rw_a_up, tt):
    bf = jnp.bfloat16
    sw, swl = rw_shift[:, :3 * RW_WIDTH], rw_shift[:, 3 * RW_WIDTH:]
    zero = jnp.zeros((RW_W_LORA, RW_WIDTH), jnp.float32)
    wl = jnp.concatenate([
        jnp.concatenate([rw_w_up[0], rw_w_up[1], zero, zero], axis=1),
        jnp.concatenate([zero, zero, rw_a_up[0], rw_a_up[1]], axis=1)], axis=0).astype(bf)
    t = jnp.arange(tt)
    same = (t[:, None] // CHUNK) == (t[None, :] // CHUNK)
    tril = (same & (t[None, :] <= t[:, None])).astype(bf)
    triu = (same & (t[None, :] >= t[:, None])).astype(bf)
    return sw, swl, wl, tril, triu


def _split_bf16(a):
    hi = a.astype(jnp.bfloat16)
    return hi, (a - hi.astype(jnp.float32)).astype(jnp.bfloat16)


def _top_rows(s, k):
    n = s.shape[0]
    row = lax.broadcasted_iota(jnp.int32, s.shape, 0)
    vals, poss = [], []
    for _ in range(k):
        m = jnp.max(s, axis=0, keepdims=True)
        pos = jnp.min(jnp.where(s == m, row, n), axis=0, keepdims=True)
        vals.append(m)
        poss.append(pos)
        s = jnp.where(row == pos, -jnp.inf, s)
    return jnp.concatenate(vals, axis=0), jnp.concatenate(poss, axis=0)


def _take_rows(table, idx):
    out = jnp.zeros(idx.shape, table.dtype)
    for a in range(table.shape[0]):
        out = jnp.where(idx == a, table[a:a + 1], out)
    return out


def _peer_topk_kernel(q_ref, keys_ref, eidx_ref, gate_ref):
    nt_dims = (((1,), (1,)), ((), ()))
    sv, si = [], []
    for p in range(2):
        q_hi, q_lo = _split_bf16(q_ref[0, :, p * N_KEYS:(p + 1) * N_KEYS])
        k_hi, k_lo = _split_bf16(keys_ref[0, p])
        s = (lax.dot_general(k_hi, q_hi, nt_dims, preferred_element_type=jnp.float32)
             + lax.dot_general(k_hi, q_lo, nt_dims, preferred_element_type=jnp.float32)
             + lax.dot_general(k_lo, q_hi, nt_dims, preferred_element_type=jnp.float32))
        v, i = _top_rows(s, PEER_TOPK)
        sv.append(v)
        si.append(i)
    cand = jnp.concatenate([sv[0][a:a + 1] + sv[1] for a in range(PEER_TOPK)], axis=0)
    top_s, pos = _top_rows(cand, PEER_TOPK)
    eidx_ref[...] = (_take_rows(si[0], pos // PEER_TOPK) * N_KEYS + _take_rows(si[1], pos % PEER_TOPK))
    p = jnp.exp(top_s - top_s[0:1])
    gate_ref[...] = p / jnp.sum(p, axis=0, keepdims=True)


def _peer_topk(qp, keys, tt):
    B, S, _ = qp.shape
    nt = S // tt
    rows = PEER_HEADS * PEER_TOPK
    out_map = lambda b, i, h: (h, b * nt + i)
    return pl.pallas_call(
        _peer_topk_kernel,
        grid=(B, nt, PEER_HEADS),
        in_specs=[
            pl.BlockSpec((1, tt, PEER_QDIM), lambda b, i, h: (b, i, h)),
            pl.BlockSpec((1, 2, N_KEYS, PEER_QDIM // 2), lambda b, i, h: (h, 0, 0, 0)),
        ],
        out_specs=[pl.BlockSpec((PEER_TOPK, tt), out_map), pl.BlockSpec((PEER_TOPK, tt), out_map)],
        out_shape=[jax.ShapeDtypeStruct((rows, B * S), jnp.int32), jax.ShapeDtypeStruct((rows, B * S), jnp.float32)],
        compiler_params=_cparams(("parallel", "parallel", "arbitrary")),
        name="peer_topk",
    )(qp, keys)


PEER_PICKS = PEER_HEADS * PEER_TOPK
GATHER_SLOTS = 8


def _gelu_tanh(a):
    return 0.5 * a * (1.0 + jnp.tanh(math.sqrt(2.0 / math.pi) * (a + 0.044715 * (a * a * a))))


def _peer_gather_kernel(eidx_hbm, h2_ref, gate_ref, x1_ref, g2_ref, tab_hbm, out_ref, idx_smem, buf, idx_sem, row_sem,
                        *, tt):
    tile = pl.program_id(0) * pl.num_programs(1) + pl.program_id(1)
    idx_copy = pltpu.make_async_copy(eidx_hbm.at[tile], idx_smem, idx_sem)
    idx_copy.start()
    idx_copy.wait()

    def issue(t, slot):
        for k in range(PEER_PICKS):
            e = idx_smem[t * PEER_PICKS + k]
            pltpu.make_async_copy(tab_hbm.at[e], buf.at[slot, :, k, :], row_sem.at[slot]).start(priority=k % 2)

    def wait_rows(slot):
        pltpu.make_async_copy(buf.at[slot], buf.at[slot], row_sem.at[slot]).wait()

    lane = lax.broadcasted_iota(jnp.int32, (PEER_PICKS, tt), 1)
    n_seg = buf.shape[1]

    def compute(t, slot):
        xrow = h2_ref[0, pl.ds(t, 1), :]
        prod = None
        for s in range(n_seg):
            u = pltpu.bitcast(buf[slot, s] & jnp.uint32(0xFFFF0000), jnp.float32)
            term = u * xrow[:, s * LANE:(s + 1) * LANE]
            prod = term if prod is None else prod + term
        act = jnp.sum(prod, axis=1, keepdims=True)
        gate = jnp.sum(jnp.where(lane == t, gate_ref[...], 0.0), axis=1, keepdims=True)
        coef = gate * _gelu_tanh(act)
        ys = [jnp.sum(coef * pltpu.bitcast(buf[slot, s] << 16, jnp.float32), axis=0, keepdims=True)
              for s in range(n_seg)]
        out_ref[0, pl.ds(t, 1), :] = x1_ref[0, pl.ds(t, 1), :] + g2_ref[0] * jnp.concatenate(ys, axis=1)

    lookahead = GATHER_SLOTS - 1
    for j in range(lookahead):
        issue(j, j)

    def group(g, carry):
        for j in range(GATHER_SLOTS):
            t = g * GATHER_SLOTS + j

            @pl.when(t + lookahead < tt)
            def _():
                issue(t + lookahead, (j + lookahead) % GATHER_SLOTS)

            wait_rows(j)
            compute(t, j)
        return carry

    lax.fori_loop(0, tt // GATHER_SLOTS, group, 0)


def _peer_gather(eidx_t, h2, gate_t, x1, g2, tab, tt):
    B, S, D = x1.shape
    nt = S // tt
    tok = lambda b, i: (b, i, 0)
    kern = functools.partial(_peer_gather_kernel, tt=tt)
    return pl.pallas_call(
        kern,
        grid=(B, nt),
        in_specs=[
            pl.BlockSpec(memory_space=pl.ANY),
            pl.BlockSpec((1, tt, D), tok),
            pl.BlockSpec((PEER_PICKS, tt), lambda b, i: (0, b * nt + i)),
            pl.BlockSpec((1, tt, D), tok),
            pl.BlockSpec((1, 1, D), lambda b, i: (b, 0, 0)),
            pl.BlockSpec(memory_space=pl.ANY),
        ],
        out_specs=pl.BlockSpec((1, tt, D), tok),
        out_shape=jax.ShapeDtypeStruct((B, S, D), jnp.float32),
        scratch_shapes=[
            pltpu.SMEM((tt * PEER_PICKS,), jnp.int32),
            pltpu.VMEM((GATHER_SLOTS, D // LANE, PEER_PICKS, LANE), jnp.uint32),
            pltpu.SemaphoreType.DMA(()),
            pltpu.SemaphoreType.DMA((GATHER_SLOTS,)),
        ],
        compiler_params=_cparams(("arbitrary", "arbitrary")),
        name="peer_gather",
    )(eidx_t, h2, gate_t, x1, g2, tab)


def _pack_expert_table(peer_u, peer_v):
    u16 = lax.bitcast_convert_type(peer_u.astype(jnp.bfloat16), jnp.uint16).astype(jnp.uint32)
    v16 = lax.bitcast_convert_type(peer_v.astype(jnp.bfloat16), jnp.uint16).astype(jnp.uint32)
    return ((u16 << 16) | v16).reshape(peer_u.shape[0], peer_u.shape[1] // LANE, LANE)


def _rope_tables(n_tokens):
    rows = n_tokens // GRID_W
    row = jnp.repeat(jnp.arange(rows, dtype=jnp.float32), GRID_W)
    col = jnp.tile(jnp.arange(GRID_W, dtype=jnp.float32), rows)
    inv = ROPE_THETA ** (-jnp.arange(ROPE_FREQS, dtype=jnp.float32) / ROPE_FREQS)
    ang = jnp.stack([row[:, None] * inv, col[:, None] * inv], axis=1)
    cos = jnp.cos(ang)[:, None, :, None, :]
    sin = jnp.sin(ang)[:, None, :, None, :]
    cos = jnp.broadcast_to(cos, (n_tokens, 2, 2, 2, ROPE_FREQS)).reshape(n_tokens, LANE)
    sgn = jnp.array([-1.0, 1.0], jnp.float32)[None, None, None, :, None]
    sin = jnp.broadcast_to(sin * sgn, (n_tokens, 2, 2, 2, ROPE_FREQS)).reshape(n_tokens, LANE)
    return cos, sin


def kernel(x, c, ctx, c_ctx, w_mod, b_mod, norm1_g, w_in, q_norm_g, k_norm_g, diff_lambda, diff_out_g, rw_shift,
           rw_w0, rw_w_up, rw_a0, rw_a_up, rw_g_up, rw_k_k, rw_k_a, rw_r_k, rw_ln_g, rw_ln_b, w_branch_a,
           w_branch_b, w_out, norm2_g, peer_wq, peer_keys, peer_u, peer_v):
    assert w_mod.shape[0] == 1, "single-layer trunk only"
    B, S, D = x.shape
    C = ctx.shape[1]
    bf = jnp.bfloat16
    lam_init = 0.8 - 0.6 * math.exp(-0.3 * 0)

    mod_x = (jax.nn.silu(c) @ w_mod[0] + b_mod[0]).reshape(B, N_MOD, 1, D)
    mod_c = (jax.nn.silu(c_ctx) @ w_mod[0] + b_mod[0]).reshape(1, N_MOD, 1, D)
    sc1x, sh1x = norm1_g[0] * (1 + mod_x[:, 1]), mod_x[:, 0]
    sc1c, sh1c = norm1_g[0] * (1 + mod_c[:, 1]), mod_c[:, 0]
    sc2x, sh2x = norm2_g[0] * (1 + mod_x[:, 4]), mod_x[:, 3]
    g1x, g2x = mod_x[:, 2], mod_x[:, 5]

    w_in_p = jnp.concatenate([w_in[0][:, :6144], w_in[0][:, 6400:], w_in[0][:, 6144:6400]], axis=1).astype(bf)
    tm = min(1024, S)
    zx = _norm_matmul(x, sc1x, sh1x, w_in_p, tm, 768)
    zc = _norm_matmul(ctx, sc1c, sh1c, w_in_p, min(tm, C), 768)

    lam = (jnp.exp(jnp.sum(diff_lambda[0, 0] * diff_lambda[0, 1])) - jnp.exp(jnp.sum(diff_lambda[0, 2] * diff_lambda[0, 3]))
           + lam_init).reshape(1)
    cos, sin_signed = _rope_tables(S)
    qg = jnp.tile(q_norm_g[0], 2).reshape(1, LANE)
    kg = jnp.tile(k_norm_g[0], 2).reshape(1, LANE)
    og = diff_out_g[0].reshape(1, LANE)
    grp = jnp.arange(LANE) // DA_QK
    avg64 = jnp.where(grp[:, None] == grp[None, :], 1.0 / DA_QK, 0.0).astype(bf)
    o_a = _diff_attention(lam, zx, zc, cos, sin_signed, qg, kg, og, avg64, min(256, S), 1.0 - lam_init)

    hd = jnp.arange(RW_WIDTH) // RW_HEAD
    ones_bd = (hd[:, None] == hd[None, :]).astype(bf)
    row1 = lambda t: t.reshape(1, RW_WIDTH)
    prep = {}
    for name, z, T in (("ctx", zc, C), ("x", zx, S)):
        tt = min(256, T)
        sw, swl, wl, tril, triu = _rwkv_constants(rw_shift[0], rw_w_up[0], rw_a_up[0], tt)
        prep[name] = _rwkv_prepare(z, sw, swl, rw_w0[0], rw_a0[0], row1(rw_k_k[0]), row1(rw_k_a[0]), row1(rw_r_k[0]),
                                   wl, rw_g_up[0].astype(bf), ones_bd, tril, triu, tt)
    s0 = jnp.zeros((2, B, RW_WIDTH // QW, QW, QW), jnp.float32)
    pc, px = prep["ctx"], prep["x"]
    _, s_ctx = _wkv_scan(pc[0:6], pc[6], pc[7], s0)
    y_rw, _ = _wkv_scan(px[0:6], px[6], px[7], s_ctx)

    x1, h2, qp = _merge(x, o_a, y_rw, px[8], px[9], row1(rw_ln_g[0]), row1(rw_ln_b[0]), ones_bd, zx, g1x,
                        w_branch_a[0].astype(bf), w_branch_b[0].astype(bf), w_out[0].astype(bf),
                        sc2x, sh2x, peer_wq[0].astype(bf), min(256, S), jnp.float32)

    tt_k = min(256, S)
    eidx_t, gate_t = _peer_topk(qp, peer_keys[0], tt_k)
    tt_g = min(256, S)
    eidx = eidx_t.T.reshape(B * S // tt_g, tt_g * PEER_PICKS)
    tab = _pack_expert_table(peer_u[0], peer_v[0])
    return _peer_gather(eidx, h2, gate_t, x1, g2x, tab, tt_g)
```

```python
import functools
import math

import jax
import jax.numpy as jnp
from jax import lax
from jax.experimental import pallas as pl
from jax.experimental.pallas import tpu as pltpu

D_MODEL = 1024
N_MOD = 6
EPS = 1e-6
GRID_W = 64
DA_HEADS = 8
DA_QK = 64
DA_V = 2 * DA_QK
ROPE_THETA = 10000.0
ROPE_FREQS = DA_QK // 4
RW_HEADS = 16
RW_HEAD = 64
RW_WIDTH = RW_HEADS * RW_HEAD
RW_W_LORA = 64
RW_A_LORA = 64
RW_G_LORA = 128
RW_LN_EPS = 64e-5
L2_EPS = 1e-12
PEER_HEADS = 8
N_KEYS = 128
PEER_TOPK = 16
PEER_QDIM = 256
PEER_CHUNK = 128

COL_Q = 0
COL_K = 1024
COL_V = 2048
COL_RW = 3072
COL_GATE = 6144
COL_LORA = 8192
IN_WIDTH = 8448

LANE = 128
SUBLANES = 8
VMEM_LIMIT = 56 * 1024 * 1024


def _cparams(sem):
    return pltpu.CompilerParams(dimension_semantics=sem, vmem_limit_bytes=VMEM_LIMIT)


def _norm_matmul_kernel(x_ref, sc_ref, sh_ref, w_ref, z_ref, hn_ref):
    @pl.when(pl.program_id(2) == 0)
    def _():
        x = x_ref[0]
        ms = jnp.mean(x * x, axis=-1, keepdims=True)
        h = x * lax.rsqrt(ms + EPS) * sc_ref[0] + sh_ref[0]
        hn_ref[...] = h.astype(jnp.bfloat16)

    z_ref[0] = jnp.dot(hn_ref[...], w_ref[...], preferred_element_type=jnp.float32)


def _norm_matmul(x, sc, sh, w, tm, tn):
    B, T, D = x.shape
    N = w.shape[1]
    bm = (lambda b, i, j: (b, 0, 0)) if sc.shape[0] == B else (lambda b, i, j: (0, 0, 0))
    return pl.pallas_call(
        _norm_matmul_kernel,
        grid=(B, T // tm, N // tn),
        in_specs=[
            pl.BlockSpec((1, tm, D), lambda b, i, j: (b, i, 0)),
            pl.BlockSpec((1, 1, D), bm),
            pl.BlockSpec((1, 1, D), bm),
            pl.BlockSpec((D, tn), lambda b, i, j: (0, j)),
        ],
        out_specs=pl.BlockSpec((1, tm, tn), lambda b, i, j: (b, i, j)),
        out_shape=jax.ShapeDtypeStruct((B, T, N), jnp.float32),
        scratch_shapes=[pltpu.VMEM((tm, D), jnp.bfloat16)],
        compiler_params=_cparams(("parallel", "parallel", "arbitrary")),
        name="norm_matmul",
    )(x, sc, sh, w)


def _group_mean_sq(x, avg):
    sq = x * x
    hi = sq.astype(jnp.bfloat16)
    lo = (sq - hi.astype(jnp.float32)).astype(jnp.bfloat16)
    return (jnp.dot(hi, avg, preferred_element_type=jnp.float32)
            + jnp.dot(lo, avg, preferred_element_type=jnp.float32))


def _head_sum(x, ones_bd):
    hi = x.astype(jnp.bfloat16)
    lo = (x - hi.astype(jnp.float32)).astype(jnp.bfloat16)
    return (jnp.dot(hi, ones_bd, preferred_element_type=jnp.float32)
            + jnp.dot(lo, ones_bd, preferred_element_type=jnp.float32))


def _rope(x, cos, sin_signed, first_half):
    partner = jnp.where(first_half, pltpu.roll(x, LANE - ROPE_FREQS, axis=1), pltpu.roll(x, ROPE_FREQS, axis=1))
    return x * cos + partner * sin_signed


def _diff_attn_kernel(lam_ref, q_ref, kx_ref, vx_ref, kc_ref, vc_ref, cosq_ref, sinq_ref, cosk_ref, sink_ref,
                      qg_ref, kg_ref, og_ref, avg64_ref, o_ref, k_s, v_s, *, n_ctx, out_scale):
    lane = lax.broadcasted_iota(jnp.int32, (1, LANE), 1)
    first_half = (lane % (2 * ROPE_FREQS)) < ROPE_FREQS
    avg64 = avg64_ref[...]

    @pl.when(pl.program_id(2) == 0)
    def _():
        kc = kc_ref[0]
        kc = kc * lax.rsqrt(_group_mean_sq(kc, avg64) + EPS) * kg_ref[...]
        k_s[0:n_ctx, :] = kc.astype(jnp.bfloat16)
        kx = kx_ref[0]
        kx = kx * lax.rsqrt(_group_mean_sq(kx, avg64) + EPS) * kg_ref[...]
        kx = _rope(kx, cosk_ref[...], sink_ref[...], first_half)
        k_s[n_ctx:, :] = kx.astype(jnp.bfloat16)
        v_s[0:n_ctx, :] = vc_ref[0].astype(jnp.bfloat16)
        v_s[n_ctx:, :] = vx_ref[0].astype(jnp.bfloat16)

    q = q_ref[0]
    q = q * lax.rsqrt(_group_mean_sq(q, avg64) + EPS) * qg_ref[...]
    q = _rope(q, cosq_ref[...], sinq_ref[...], first_half) * (DA_QK ** -0.5 * math.log2(math.e))
    k = k_s[...]
    ps = []
    for m in range(2):
        in_map = (lane // DA_QK) == m
        qm = jnp.where(in_map, q, 0.0).astype(jnp.bfloat16)
        s = lax.dot_general(qm, k, (((1,), (1,)), ((), ())), preferred_element_type=jnp.float32)
        p = jnp.exp2(s - jnp.max(s, axis=-1, keepdims=True))
        scale = (1.0 if m == 0 else lam_ref[0]) / jnp.sum(p, axis=-1, keepdims=True)
        ps.append(p * scale)
    pd = (ps[0] - ps[1]).astype(jnp.bfloat16)
    o = jnp.dot(pd, v_s[...], preferred_element_type=jnp.float32)
    o = o * lax.rsqrt(jnp.mean(o * o, axis=-1, keepdims=True) + EPS) * og_ref[...] * out_scale
    o_ref[0] = o.astype(o_ref.dtype)


def _diff_attention(lam, zx, zc, cos, sin_signed, qg, kg, og, avg64, tq, out_scale):
    B, S, _ = zx.shape
    C = zc.shape[1]
    qb, kb, vb = COL_Q // LANE, COL_K // LANE, COL_V // LANE
    kern = functools.partial(_diff_attn_kernel, n_ctx=C, out_scale=out_scale)
    const = lambda b, h, i: (0, 0)
    return pl.pallas_call(
        kern,
        grid=(B, DA_HEADS, S // tq),
        in_specs=[
            pl.BlockSpec(memory_space=pltpu.SMEM),
            pl.BlockSpec((1, tq, LANE), lambda b, h, i: (b, i, qb + h)),
            pl.BlockSpec((1, S, LANE), lambda b, h, i: (b, 0, kb + h)),
            pl.BlockSpec((1, S, LANE), lambda b, h, i: (b, 0, vb + h)),
            pl.BlockSpec((1, C, LANE), lambda b, h, i: (b, 0, kb + h)),
            pl.BlockSpec((1, C, LANE), lambda b, h, i: (b, 0, vb + h)),
            pl.BlockSpec((tq, LANE), lambda b, h, i: (i, 0)),
            pl.BlockSpec((tq, LANE), lambda b, h, i: (i, 0)),
            pl.BlockSpec((S, LANE), const),
            pl.BlockSpec((S, LANE), const),
            pl.BlockSpec((1, LANE), const),
            pl.BlockSpec((1, LANE), const),
            pl.BlockSpec((1, LANE), const),
            pl.BlockSpec((LANE, LANE), const),
        ],
        out_specs=pl.BlockSpec((1, tq, LANE), lambda b, h, i: (b, i, h)),
        out_shape=jax.ShapeDtypeStruct((B, S, DA_HEADS * DA_V), jnp.bfloat16),
        scratch_shapes=[pltpu.VMEM((C + S, LANE), jnp.bfloat16), pltpu.VMEM((C + S, LANE), jnp.bfloat16)],
        compiler_params=_cparams(("parallel", "parallel", "arbitrary")),
        name="diff_attention",
    )(lam, zx, zx, zx, zc, zc, cos, sin_signed, cos, sin_signed, qg, kg, og, avg64)


def _merge_kernel(x_ref, oa_ref, y0_ref, y1_ref, bonus_ref, g_ref, lng_ref, lnb_ref, ones_ref, ga_ref, gb_ref, g1_ref,
                  wa_ref, wb_ref, wo_ref, sc2_ref, sh2_ref, wq_ref, x1_ref, h2_ref, qp_ref):
    ones_bd = ones_ref[...]
    yy = y0_ref[0, 0] + y1_ref[0, 0]
    dlt = yy - _head_sum(yy, ones_bd) * (1.0 / RW_HEAD)
    var = _head_sum(dlt * dlt, ones_bd) * (1.0 / RW_HEAD)
    ob = (dlt * lax.rsqrt(var + RW_LN_EPS) * lng_ref[...] + lnb_ref[...] + bonus_ref[0]) * g_ref[0]
    ta = jnp.dot(oa_ref[0], wa_ref[...], preferred_element_type=jnp.float32)
    tb = jnp.dot(ob.astype(jnp.bfloat16), wb_ref[...], preferred_element_type=jnp.float32)
    mix = jax.nn.sigmoid(ga_ref[0]) * ta + jax.nn.sigmoid(gb_ref[0]) * tb
    y = jnp.dot(mix.astype(jnp.bfloat16), wo_ref[...], preferred_element_type=jnp.float32)
    x1 = x_ref[0] + g1_ref[0] * y
    x1_ref[0] = x1
    ms = jnp.mean(x1 * x1, axis=-1, keepdims=True)
    h2 = x1 * lax.rsqrt(ms + EPS) * sc2_ref[0] + sh2_ref[0]
    h2_ref[0] = h2.astype(h2_ref.dtype)
    qp_ref[0] = jnp.dot(h2.astype(jnp.bfloat16), wq_ref[...], preferred_element_type=jnp.float32)


def _merge(x, oa, y, bonus, g, ln_g, ln_b, ones_bd, zx, g1, wa, wb, wo, sc2, sh2, wq, tm, h2_dtype):
    B, S, D = x.shape
    NQ = wq.shape[1]
    tok = lambda b, i: (b, i, 0)
    per_b = lambda b, i: (b, 0, 0)
    const = lambda b, i: (0, 0)
    ga_blk, gb_blk = COL_GATE // D, COL_GATE // D + 1
    return pl.pallas_call(
        _merge_kernel,
        grid=(B, S // tm),
        in_specs=[
            pl.BlockSpec((1, tm, D), tok),
            pl.BlockSpec((1, tm, D), tok),
            pl.BlockSpec((1, 1, tm, D), lambda b, i: (0, b, i, 0)),
            pl.BlockSpec((1, 1, tm, D), lambda b, i: (1, b, i, 0)),
            pl.BlockSpec((1, tm, D), tok),
            pl.BlockSpec((1, tm, D), tok),
            pl.BlockSpec((1, D), const),
            pl.BlockSpec((1, D), const),
            pl.BlockSpec((D, D), const),
            pl.BlockSpec((1, tm, D), lambda b, i: (b, i, ga_blk)),
            pl.BlockSpec((1, tm, D), lambda b, i: (b, i, gb_blk)),
            pl.BlockSpec((1, 1, D), per_b),
            pl.BlockSpec((D, D), const),
            pl.BlockSpec((D, D), const),
            pl.BlockSpec((D, D), const),
            pl.BlockSpec((1, 1, D), per_b),
            pl.BlockSpec((1, 1, D), per_b),
            pl.BlockSpec((D, NQ), const),
        ],
        out_specs=[
            pl.BlockSpec((1, tm, D), tok),
            pl.BlockSpec((1, tm, D), tok),
            pl.BlockSpec((1, tm, NQ), tok),
        ],
        out_shape=[
            jax.ShapeDtypeStruct((B, S, D), jnp.float32),
            jax.ShapeDtypeStruct((B, S, D), h2_dtype),
            jax.ShapeDtypeStruct((B, S, NQ), jnp.float32),
        ],
        compiler_params=_cparams(("parallel", "parallel")),
        name="merge_peerq",
    )(x, oa, y, y, bonus, g, ln_g, ln_b, ones_bd, zx, zx, g1, wa, wb, wo, sc2, sh2, wq)


CHUNK = 64
QUAD = 4
QW = QUAD * RW_HEAD
SCAN_QUADS = 4
SCAN_BATCH = 2


def _split3(a):
    hi = a.astype(jnp.bfloat16)
    r1 = a - hi.astype(jnp.float32)
    mid = r1.astype(jnp.bfloat16)
    lo = (r1 - mid.astype(jnp.float32)).astype(jnp.bfloat16)
    return hi, mid, lo


def _dot3(m, parts):
    acc = jnp.dot(m, parts[0], preferred_element_type=jnp.float32)
    acc = acc + jnp.dot(m, parts[1], preferred_element_type=jnp.float32)
    return acc + jnp.dot(m, parts[2], preferred_element_type=jnp.float32)


def _shift3(z, prev_row, next_row, w):
    n = z.shape[0]
    row = lax.broadcasted_iota(jnp.int32, z.shape, 0)
    zm = jnp.where(row == 0, prev_row, pltpu.roll(z, 1, axis=0))
    zp = jnp.where(row == n - 1, next_row, pltpu.roll(z, n - 1, axis=0))
    return w[0:1] * zm + w[1:2] * z + w[2:3] * zp


def _rwkv_prep_kernel(z_ref, zp_ref, zn_ref, l_ref, lp_ref, ln_ref, sw_ref, swl_ref, w0_ref, a0_ref, kk_ref, ka_ref,
                      rk_ref, wl_ref, gup_ref, ones_ref, tril_ref, triu_ref,
                      at_ref, bt_ref, kt_ref, rt_ref, bh_ref, kh_ref, v_ref, pl_ref, bonus_ref, g_ref, *, n_tiles):
    i = pl.program_id(1)
    inner_lo = 1.0 - (i == 0).astype(jnp.float32)
    inner_hi = 1.0 - (i == n_tiles - 1).astype(jnp.float32)
    zs = _shift3(z_ref[0], zp_ref[0, 7:8, :] * inner_lo, zn_ref[0, 0:1, :] * inner_hi, sw_ref[...])
    ls = _shift3(l_ref[0], lp_ref[0, 7:8, :] * inner_lo, ln_ref[0, 0:1, :] * inner_hi, swl_ref[...])
    r, k, v = zs[:, 0:RW_WIDTH], zs[:, RW_WIDTH:2 * RW_WIDTH], zs[:, 2 * RW_WIDTH:3 * RW_WIDTH]
    lane = lax.broadcasted_iota(jnp.int32, (1, LANE), 1)
    wa_in = jnp.where(lane < RW_W_LORA, jnp.tanh(ls[:, 0:LANE]), ls[:, 0:LANE]).astype(jnp.bfloat16)
    lora = jnp.dot(wa_in, wl_ref[...], preferred_element_type=jnp.float32)
    ones_bd = ones_ref[...]
    kn = k * kk_ref[...]
    kk = kn * lax.rsqrt(_head_sum(kn * kn, ones_bd) + L2_EPS)
    bonus_ref[0] = _head_sum(r * k * rk_ref[...], ones_bd) * v
    g_ref[0] = jnp.dot(jax.nn.sigmoid(ls[:, LANE:2 * LANE]).astype(jnp.bfloat16), gup_ref[...],
                       preferred_element_type=jnp.float32)
    v_ref[0] = v.astype(v_ref.dtype)
    tris = (tril_ref[...], triu_ref[...])
    for d in range(2):
        lw = lora[:, d * RW_WIDTH:(d + 1) * RW_WIDTH] + w0_ref[d:d + 1, :]
        logw = -jnp.exp(-jax.nn.softplus(-lw) - 0.5)
        a = jax.nn.sigmoid(a0_ref[d:d + 1, :] + lora[:, (2 + d) * RW_WIDTH:(3 + d) * RW_WIDTH])
        k_eff = k * (1.0 + (a - 1.0) * ka_ref[...])
        b = kk * a
        parts = _split3(logw)
        incl = _dot3(tris[d], parts)
        rest = _dot3(tris[1 - d], parts) - logw
        at_ref[d, 0] = (kk * jnp.exp(incl - logw)).astype(at_ref.dtype)
        rt_ref[d, 0] = (r * jnp.exp(incl)).astype(rt_ref.dtype)
        inv = jnp.exp(-incl)
        bt_ref[d, 0] = (b * inv).astype(bt_ref.dtype)
        kt_ref[d, 0] = (k_eff * inv).astype(kt_ref.dtype)
        to_end = jnp.exp(rest)
        bh_ref[d, 0] = (b * to_end).astype(bh_ref.dtype)
        kh_ref[d, 0] = (k_eff * to_end).astype(kh_ref.dtype)
        tot = incl + rest
        for c in range(tot.shape[0] // CHUNK):
            pl_ref[d, 0, c] = jnp.exp(tot[c * CHUNK:c * CHUNK + 1])


def _rwkv_prepare(z, sw, swl, w0, a0, k_k, k_a, r_k, wl, g_up, ones_bd, tril, triu, tt):
    B, T, _ = z.shape
    nt = T // tt
    rb, lb = COL_RW // (3 * RW_WIDTH), COL_LORA // (2 * LANE)
    tb = tt // 8
    prev = lambda i: jnp.maximum(i * tb - 1, 0)
    nxt = lambda i: jnp.minimum((i + 1) * tb, T // 8 - 1)
    const2 = lambda b, i: (0, 0)
    tok = lambda b, i: (b, i, 0)
    dtok = lambda b, i: (0, b, i, 0)
    bf = jnp.bfloat16
    kern = functools.partial(_rwkv_prep_kernel, n_tiles=nt)
    return pl.pallas_call(
        kern,
        grid=(B, nt),
        in_specs=[
            pl.BlockSpec((1, tt, 3 * RW_WIDTH), lambda b, i: (b, i, rb)),
            pl.BlockSpec((1, 8, 3 * RW_WIDTH), lambda b, i: (b, prev(i), rb)),
            pl.BlockSpec((1, 8, 3 * RW_WIDTH), lambda b, i: (b, nxt(i), rb)),
            pl.BlockSpec((1, tt, 2 * LANE), lambda b, i: (b, i, lb)),
            pl.BlockSpec((1, 8, 2 * LANE), lambda b, i: (b, prev(i), lb)),
            pl.BlockSpec((1, 8, 2 * LANE), lambda b, i: (b, nxt(i), lb)),
            pl.BlockSpec((3, 3 * RW_WIDTH), const2),
            pl.BlockSpec((3, 2 * LANE), const2),
            pl.BlockSpec((2, RW_WIDTH), const2),
            pl.BlockSpec((2, RW_WIDTH), const2),
            pl.BlockSpec((1, RW_WIDTH), const2),
            pl.BlockSpec((1, RW_WIDTH), const2),
            pl.BlockSpec((1, RW_WIDTH), const2),
            pl.BlockSpec((LANE, 4 * RW_WIDTH), const2),
            pl.BlockSpec((LANE, RW_WIDTH), const2),
            pl.BlockSpec((RW_WIDTH, RW_WIDTH), const2),
            pl.BlockSpec((tt, tt), const2),
            pl.BlockSpec((tt, tt), const2),
        ],
        out_specs=[pl.BlockSpec((2, 1, tt, RW_WIDTH), dtok)] * 6 + [
            pl.BlockSpec((1, tt, RW_WIDTH), tok),
            pl.BlockSpec((2, 1, tt // CHUNK, 1, RW_WIDTH), lambda b, i: (0, b, i, 0, 0)),
            pl.BlockSpec((1, tt, RW_WIDTH), tok),
            pl.BlockSpec((1, tt, RW_WIDTH), tok),
        ],
        out_shape=[jax.ShapeDtypeStruct((2, B, T, RW_WIDTH), bf)] * 6 + [
            jax.ShapeDtypeStruct((B, T, RW_WIDTH), bf),
            jax.ShapeDtypeStruct((2, B, T // CHUNK, 1, RW_WIDTH), jnp.float32),
            jax.ShapeDtypeStruct((B, T, RW_WIDTH), jnp.float32),
            jax.ShapeDtypeStruct((B, T, RW_WIDTH), jnp.float32),
        ],
        compiler_params=_cparams(("parallel", "parallel")),
        name="rwkv_prepare",
    )(z, z, z, z, z, z, sw, swl, w0, a0, k_k, k_a, r_k, wl, g_up, ones_bd, tril, triu)


_NT = (((1,), (1,)), ((), ()))
_TN = (((0,), (0,)), ((), ()))


def _wkv_scan_kernel(at_ref, bt_ref, kt_ref, rt_ref, bh_ref, kh_ref, v_ref, pl_ref, s0_ref, y_ref, sf_ref, s_scr,
                     *, n_chunks):
    d = pl.program_id(0)
    c = pl.program_id(3)
    f32, bf = jnp.float32, jnp.bfloat16

    @pl.when(c == 0)
    def _():
        s_scr[...] = s0_ref[0]

    row = lax.broadcasted_iota(jnp.int32, (QW, QW), 0)
    col = lax.broadcasted_iota(jnp.int32, (QW, QW), 1)
    same_head = (row // RW_HEAD) == (col // RW_HEAD)
    tok = lax.broadcasted_iota(jnp.int32, (CHUNK, QW), 0)
    src = lax.broadcasted_iota(jnp.int32, (CHUNK, QW), 1) % RW_HEAD
    order = (tok - src) * (1 - 2 * d)
    strict = order > 0
    incl = order >= 0
    eye = (tok == src).astype(f32)
    blk_sizes = [4 << i for i in range(CHUNK.bit_length() - 2)]
    same_blk = [(tok // bs) == (src // bs) for bs in blk_sizes]

    def spread(t):
        return jnp.where(same_head, jnp.concatenate([t] * QUAD, axis=0), jnp.zeros((), t.dtype))

    def mm(lhs, rhs):
        return jnp.dot(lhs.astype(bf), rhs.astype(bf), preferred_element_type=f32)

    for bi, q in [(bi, q) for bi in range(at_ref.shape[1]) for q in range(SCAN_QUADS)]:
        sl = slice(q * QW, (q + 1) * QW)
        a, b, k, r = at_ref[0, bi, :, sl], bt_ref[0, bi, :, sl], kt_ref[0, bi, :, sl], rt_ref[0, bi, :, sl]
        bh, kh, v = bh_ref[0, bi, :, sl], kh_ref[0, bi, :, sl], v_ref[bi, :, sl]
        ar = jnp.concatenate([a, r], axis=0)
        xb = lax.dot_general(ar, spread(b), _NT, preferred_element_type=f32)
        xk = lax.dot_general(ar, spread(k), _NT, preferred_element_type=f32)
        m_ab = jnp.where(strict, xb[:CHUNK], 0.0)
        m_ak = jnp.where(strict, xk[:CHUNK], 0.0)
        m_rb = jnp.where(incl, xb[CHUNK:], 0.0).astype(bf)
        m_rk = jnp.where(incl, xk[CHUNK:], 0.0)
        n0 = jnp.where(same_blk[0], m_ab, 0.0).astype(bf)
        x = eye - n0.astype(f32)
        x = x + mm(x, spread(mm(n0, spread(n0)).astype(bf)))
        for lvl in range(1, len(same_blk)):
            e = jnp.where(same_blk[lvl] & ~same_blk[lvl - 1], m_ab, 0.0).astype(bf)
            x = x - mm(mm(x, spread(e)), spread(x.astype(bf)))
        tb = x.astype(bf)
        w = mm(jnp.concatenate([m_ak, m_rk], axis=0), spread(v))
        uu = mm(tb, jnp.concatenate([spread(a), spread(w[:CHUNK].astype(bf))], axis=1))
        ua, u0 = uu[:, :QW].astype(bf), uu[:, QW:].astype(bf)
        rr = mm(m_rb, jnp.concatenate([spread(ua), spread(u0)], axis=1))
        ry = r.astype(f32) - rr[:, :QW]
        y0 = w[CHUNK:] - rr[:, QW:]
        gm = jnp.where(same_head, lax.dot_general(ua, bh, _TN, preferred_element_type=f32), 0.0)
        hm = jnp.where(same_head, lax.dot_general(v, kh, _TN, preferred_element_type=f32)
                       - lax.dot_general(u0, bh, _TN, preferred_element_type=f32), 0.0)
        s = s_scr[bi, q]
        sb = s.astype(bf)
        y_ref[0, bi, :, sl] = lax.dot_general(ry.astype(bf), sb, _NT, preferred_element_type=f32) + y0
        s_scr[bi, q] = s * pl_ref[0, bi, 0, :, sl] - jnp.dot(sb, gm.astype(bf), preferred_element_type=f32) + hm

    @pl.when(c == n_chunks - 1)
    def _():
        sf_ref[0] = s_scr[...]


def _wkv_scan(ops, v, pl_arr, s0):
    _, B, T, W = ops[0].shape
    nc = T // CHUNK
    hw = SCAN_QUADS * QW
    ng = W // hw
    nb = SCAN_BATCH if B % SCAN_BATCH == 0 else 1
    cidx = lambda d, c: c + d * (nc - 1 - 2 * c)
    op_spec = pl.BlockSpec((1, nb, CHUNK, hw), lambda d, b, g, c: (d, b, cidx(d, c), g))
    st_spec = pl.BlockSpec((1, nb, SCAN_QUADS, QW, QW), lambda d, b, g, c: (d, b, g, 0, 0))
    kern = functools.partial(_wkv_scan_kernel, n_chunks=nc)
    return pl.pallas_call(
        kern,
        grid=(2, B // nb, ng, nc),
        in_specs=[op_spec] * 6 + [
            pl.BlockSpec((nb, CHUNK, hw), lambda d, b, g, c: (b, cidx(d, c), g)),
            pl.BlockSpec((1, nb, 1, 1, hw), lambda d, b, g, c: (d, b, cidx(d, c), 0, g)),
            st_spec,
        ],
        out_specs=[op_spec, st_spec],
        out_shape=[
            jax.ShapeDtypeStruct((2, B, T, W), jnp.float32),
            jax.ShapeDtypeStruct(s0.shape, jnp.float32),
        ],
        scratch_shapes=[pltpu.VMEM((nb, SCAN_QUADS, QW, QW), jnp.float32)],
        compiler_params=_cparams(("parallel", "parallel", "parallel", "arbitrary")),
        name="wkv_scan",
    )(*ops, v, pl_arr, s0)


def _rwkv_constants(rw_shift, rw_w_up, rw_a_up, tt):
    bf = jnp.bfloat16
    sw, swl = rw_shift[:, :3 * RW_WIDTH], rw_shift[:, 3 * RW_WIDTH:]
    zero = jnp.zeros((RW_W_LORA, RW_WIDTH), jnp.float32)
    wl = jnp.concatenate([
        jnp.concatenate([rw_w_up[0], rw_w_up[1], zero, zero], axis=1),
        jnp.concatenate([zero, zero, rw_a_up[0], rw_a_up[1]], axis=1)], axis=0).astype(bf)
    t = jnp.arange(tt)
    same = (t[:, None] // CHUNK) == (t[None, :] // CHUNK)
    tril = (same & (t[None, :] <= t[:, None])).astype(bf)
    triu = (same & (t[None, :] >= t[:, None])).astype(bf)
    return sw, swl, wl, tril, triu


def _split_bf16(a):
    hi = a.astype(jnp.bfloat16)
    return hi, (a - hi.astype(jnp.float32)).astype(jnp.bfloat16)


def _top_rows(s, k):
    n = s.shape[0]
    row = lax.broadcasted_iota(jnp.int32, s.shape, 0)
    vals, poss = [], []
    for _ in range(k):
        m = jnp.max(s, axis=0, keepdims=True)
        pos = jnp.min(jnp.where(s == m, row, n), axis=0, keepdims=True)
        vals.append(m)
        poss.append(pos)
        s = jnp.where(row == pos, -jnp.inf, s)
    return jnp.concatenate(vals, axis=0), jnp.concatenate(poss, axis=0)


def _take_rows(table, idx):
    out = jnp.zeros(idx.shape, table.dtype)
    for a in range(table.shape[0]):
        out = jnp.where(idx == a, table[a:a + 1], out)
    return out


def _peer_topk_kernel(q_ref, keys_ref, eidx_ref, gate_ref):
    nt_dims = (((1,), (1,)), ((), ()))
    sv, si = [], []
    for p in range(2):
        q_hi, q_lo = _split_bf16(q_ref[0, :, p * N_KEYS:(p + 1) * N_KEYS])
        k_hi, k_lo = _split_bf16(keys_ref[0, p])
        s = (lax.dot_general(k_hi, q_hi, nt_dims, preferred_element_type=jnp.float32)
             + lax.dot_general(k_hi, q_lo, nt_dims, preferred_element_type=jnp.float32)
             + lax.dot_general(k_lo, q_hi, nt_dims, preferred_element_type=jnp.float32))
        v, i = _top_rows(s, PEER_TOPK)
        sv.append(v)
        si.append(i)
    cand = jnp.concatenate([sv[0][a:a + 1] + sv[1] for a in range(PEER_TOPK)], axis=0)
    top_s, pos = _top_rows(cand, PEER_TOPK)
    eidx_ref[...] = (_take_rows(si[0], pos // PEER_TOPK) * N_KEYS + _take_rows(si[1], pos % PEER_TOPK))
    p = jnp.exp(top_s - top_s[0:1])
    gate_ref[...] = p / jnp.sum(p, axis=0, keepdims=True)


def _peer_topk(qp, keys, tt):
    B, S, _ = qp.shape
    nt = S // tt
    rows = PEER_HEADS * PEER_TOPK
    out_map = lambda b, i, h: (h, b * nt + i)
    return pl.pallas_call(
        _peer_topk_kernel,
        grid=(B, nt, PEER_HEADS),
        in_specs=[
            pl.BlockSpec((1, tt, PEER_QDIM), lambda b, i, h: (b, i, h)),
            pl.BlockSpec((1, 2, N_KEYS, PEER_QDIM // 2), lambda b, i, h: (h, 0, 0, 0)),
        ],
        out_specs=[pl.BlockSpec((PEER_TOPK, tt), out_map), pl.BlockSpec((PEER_TOPK, tt), out_map)],
        out_shape=[jax.ShapeDtypeStruct((rows, B * S), jnp.int32), jax.ShapeDtypeStruct((rows, B * S), jnp.float32)],
        compiler_params=_cparams(("parallel", "parallel", "arbitrary")),
        name="peer_topk",
    )(qp, keys)


PEER_PICKS = PEER_HEADS * PEER_TOPK
GATHER_SLOTS = 4


def _gelu_tanh(a):
    return 0.5 * a * (1.0 + jnp.tanh(math.sqrt(2.0 / math.pi) * (a + 0.044715 * (a * a * a))))


def _peer_gather_kernel(eidx_hbm, h2_ref, gate_ref, x1_ref, g2_ref, tab_hbm, out_ref, idx_smem, buf, idx_sem, row_sem,
                        *, tt):
    tile = pl.program_id(0) * pl.num_programs(1) + pl.program_id(1)
    idx_copy = pltpu.make_async_copy(eidx_hbm.at[tile], idx_smem, idx_sem)
    idx_copy.start()
    idx_copy.wait()

    def issue(t, slot):
        for k in range(PEER_PICKS):
            e = idx_smem[t * PEER_PICKS + k]
            pltpu.make_async_copy(tab_hbm.at[e], buf.at[slot, :, k, :], row_sem.at[slot]).start(priority=k % 2)

    def wait_rows(slot):
        pltpu.make_async_copy(buf.at[slot], buf.at[slot], row_sem.at[slot]).wait()

    lane = lax.broadcasted_iota(jnp.int32, (PEER_PICKS, tt), 1)
    n_seg = buf.shape[1]

    def compute(t, slot):
        xrow = h2_ref[0, pl.ds(t, 1), :]
        prod = None
        for s in range(n_seg):
            u = pltpu.bitcast(buf[slot, s] & jnp.uint32(0xFFFF0000), jnp.float32)
            term = u * xrow[:, s * LANE:(s + 1) * LANE]
            prod = term if prod is None else prod + term
        act = jnp.sum(prod, axis=1, keepdims=True)
        gate = jnp.sum(jnp.where(lane == t, gate_ref[...], 0.0), axis=1, keepdims=True)
        coef = gate * _gelu_tanh(act)
        ys = [jnp.sum(coef * pltpu.bitcast(buf[slot, s] << 16, jnp.float32), axis=0, keepdims=True)
              for s in range(n_seg)]
        out_ref[0, pl.ds(t, 1), :] = x1_ref[0, pl.ds(t, 1), :] + g2_ref[0] * jnp.concatenate(ys, axis=1)

    for j in range(GATHER_SLOTS):
        issue(j, j)

    def step(t, j, prefetch):
        wait_rows(j)
        compute(t, j)
        if prefetch:
            issue(t + GATHER_SLOTS, j)

    def group(g, carry):
        for j in range(GATHER_SLOTS):
            step(g * GATHER_SLOTS + j, j, True)
        return carry

    n_groups = tt // GATHER_SLOTS
    lax.fori_loop(0, n_groups - 1, group, 0)
    for j in range(GATHER_SLOTS):
        step((n_groups - 1) * GATHER_SLOTS + j, j, False)


def _peer_gather(eidx_t, h2, gate_t, x1, g2, tab, tt):
    B, S, D = x1.shape
    nt = S // tt
    tok = lambda b, i: (b, i, 0)
    kern = functools.partial(_peer_gather_kernel, tt=tt)
    return pl.pallas_call(
        kern,
        grid=(B, nt),
        in_specs=[
            pl.BlockSpec(memory_space=pl.ANY),
            pl.BlockSpec((1, tt, D), tok),
            pl.BlockSpec((PEER_PICKS, tt), lambda b, i: (0, b * nt + i)),
            pl.BlockSpec((1, tt, D), tok),
            pl.BlockSpec((1, 1, D), lambda b, i: (b, 0, 0)),
            pl.BlockSpec(memory_space=pl.ANY),
        ],
        out_specs=pl.BlockSpec((1, tt, D), tok),
        out_shape=jax.ShapeDtypeStruct((B, S, D), jnp.float32),
        scratch_shapes=[
            pltpu.SMEM((tt * PEER_PICKS,), jnp.int32),
            pltpu.VMEM((GATHER_SLOTS, D // LANE, PEER_PICKS, LANE), jnp.uint32),
            pltpu.SemaphoreType.DMA(()),
            pltpu.SemaphoreType.DMA((GATHER_SLOTS,)),
        ],
        compiler_params=_cparams(("arbitrary", "arbitrary")),
        name="peer_gather",
    )(eidx_t, h2, gate_t, x1, g2, tab)


def _pack_expert_table(peer_u, peer_v):
    u16 = lax.bitcast_convert_type(peer_u.astype(jnp.bfloat16), jnp.uint16).astype(jnp.uint32)
    v16 = lax.bitcast_convert_type(peer_v.astype(jnp.bfloat16), jnp.uint16).astype(jnp.uint32)
    return ((u16 << 16) | v16).reshape(peer_u.shape[0], peer_u.shape[1] // LANE, LANE)


def _rope_tables(n_tokens):
    rows = n_tokens // GRID_W
    row = jnp.repeat(jnp.arange(rows, dtype=jnp.float32), GRID_W)
    col = jnp.tile(jnp.arange(GRID_W, dtype=jnp.float32), rows)
    inv = ROPE_THETA ** (-jnp.arange(ROPE_FREQS, dtype=jnp.float32) / ROPE_FREQS)
    ang = jnp.stack([row[:, None] * inv, col[:, None] * inv], axis=1)
    cos = jnp.cos(ang)[:, None, :, None, :]
    sin = jnp.sin(ang)[:, None, :, None, :]
    cos = jnp.broadcast_to(cos, (n_tokens, 2, 2, 2, ROPE_FREQS)).reshape(n_tokens, LANE)
    sgn = jnp.array([-1.0, 1.0], jnp.float32)[None, None, None, :, None]
    sin = jnp.broadcast_to(sin * sgn, (n_tokens, 2, 2, 2, ROPE_FREQS)).reshape(n_tokens, LANE)
    return cos, sin


def kernel(x, c, ctx, c_ctx, w_mod, b_mod, norm1_g, w_in, q_norm_g, k_norm_g, diff_lambda, diff_out_g, rw_shift,
           rw_w0, rw_w_up, rw_a0, rw_a_up, rw_g_up, rw_k_k, rw_k_a, rw_r_k, rw_ln_g, rw_ln_b, w_branch_a,
           w_branch_b, w_out, norm2_g, peer_wq, peer_keys, peer_u, peer_v):
    assert w_mod.shape[0] == 1, "single-layer trunk only"
    B, S, D = x.shape
    C = ctx.shape[1]
    bf = jnp.bfloat16
    lam_init = 0.8 - 0.6 * math.exp(-0.3 * 0)

    mod_x = (jax.nn.silu(c) @ w_mod[0] + b_mod[0]).reshape(B, N_MOD, 1, D)
    mod_c = (jax.nn.silu(c_ctx) @ w_mod[0] + b_mod[0]).reshape(1, N_MOD, 1, D)
    sc1x, sh1x = norm1_g[0] * (1 + mod_x[:, 1]), mod_x[:, 0]
    sc1c, sh1c = norm1_g[0] * (1 + mod_c[:, 1]), mod_c[:, 0]
    sc2x, sh2x = norm2_g[0] * (1 + mod_x[:, 4]), mod_x[:, 3]
    g1x, g2x = mod_x[:, 2], mod_x[:, 5]

    w_in_p = jnp.concatenate([w_in[0][:, :6144], w_in[0][:, 6400:], w_in[0][:, 6144:6400]], axis=1).astype(bf)
    tm = min(1024, S)
    zx = _norm_matmul(x, sc1x, sh1x, w_in_p, tm, 768)
    zc = _norm_matmul(ctx, sc1c, sh1c, w_in_p, min(tm, C), 768)

    lam = (jnp.exp(jnp.sum(diff_lambda[0, 0] * diff_lambda[0, 1])) - jnp.exp(jnp.sum(diff_lambda[0, 2] * diff_lambda[0, 3]))
           + lam_init).reshape(1)
    cos, sin_signed = _rope_tables(S)
    qg = jnp.tile(q_norm_g[0], 2).reshape(1, LANE)
    kg = jnp.tile(k_norm_g[0], 2).reshape(1, LANE)
    og = diff_out_g[0].reshape(1, LANE)
    grp = jnp.arange(LANE) // DA_QK
    avg64 = jnp.where(grp[:, None] == grp[None, :], 1.0 / DA_QK, 0.0).astype(bf)
    o_a = _diff_attention(lam, zx, zc, cos, sin_signed, qg, kg, og, avg64, min(256, S), 1.0 - lam_init)

    hd = jnp.arange(RW_WIDTH) // RW_HEAD
    ones_bd = (hd[:, None] == hd[None, :]).astype(bf)
    row1 = lambda t: t.reshape(1, RW_WIDTH)
    prep = {}
    for name, z, T in (("ctx", zc, C), ("x", zx, S)):
        tt = min(256, T)
        sw, swl, wl, tril, triu = _rwkv_constants(rw_shift[0], rw_w_up[0], rw_a_up[0], tt)
        prep[name] = _rwkv_prepare(z, sw, swl, rw_w0[0], rw_a0[0], row1(rw_k_k[0]), row1(rw_k_a[0]), row1(rw_r_k[0]),
                                   wl, rw_g_up[0].astype(bf), ones_bd, tril, triu, tt)
    s0 = jnp.zeros((2, B, RW_WIDTH // QW, QW, QW), jnp.float32)
    pc, px = prep["ctx"], prep["x"]
    _, s_ctx = _wkv_scan(pc[0:6], pc[6], pc[7], s0)
    y_rw, _ = _wkv_scan(px[0:6], px[6], px[7], s_ctx)

    x1, h2, qp = _merge(x, o_a, y_rw, px[8], px[9], row1(rw_ln_g[0]), row1(rw_ln_b[0]), ones_bd, zx, g1x,
                        w_branch_a[0].astype(bf), w_branch_b[0].astype(bf), w_out[0].astype(bf),
                        sc2x, sh2x, peer_wq[0].astype(bf), min(256, S), jnp.float32)

    tt_k = min(256, S)
    eidx_t, gate_t = _peer_topk(qp, peer_keys[0], tt_k)
    tt_g = min(256, S)
    eidx = eidx_t.T.reshape(B * S // tt_g, tt_g * PEER_PICKS)
    tab = _pack_expert_table(peer_u[0], peer_v[0])
    return _peer_gather(eidx, h2, gate_t, x1, g2x, tab, tt_g)
```

```python
import functools
import math

import jax
import jax.numpy as jnp
from jax import lax
from jax.experimental import pallas as pl
from jax.experimental.pallas import tpu as pltpu

D_MODEL = 1024
N_MOD = 6
EPS = 1e-6
GRID_W = 64
DA_HEADS = 8
DA_QK = 64
DA_V = 2 * DA_QK
ROPE_THETA = 10000.0
ROPE_FREQS = DA_QK // 4
RW_HEADS = 16
RW_HEAD = 64
RW_WIDTH = RW_HEADS * RW_HEAD
RW_W_LORA = 64
RW_A_LORA = 64
RW_G_LORA = 128
RW_LN_EPS = 64e-5
L2_EPS = 1e-12
PEER_HEADS = 8
N_KEYS = 128
PEER_TOPK = 16
PEER_QDIM = 256
PEER_CHUNK = 128

COL_Q = 0
COL_K = 1024
COL_V = 2048
COL_RW = 3072
COL_GATE = 6144
COL_LORA = 8192
IN_WIDTH = 8448

LANE = 128
SUBLANES = 8
VMEM_LIMIT = 56 * 1024 * 1024


def _cparams(sem):
    return pltpu.CompilerParams(dimension_semantics=sem, vmem_limit_bytes=VMEM_LIMIT)


def _norm_matmul_kernel(x_ref, sc_ref, sh_ref, w_ref, z_ref, hn_ref):
    @pl.when(pl.program_id(2) == 0)
    def _():
        x = x_ref[0]
        ms = jnp.mean(x * x, axis=-1, keepdims=True)
        h = x * lax.rsqrt(ms + EPS) * sc_ref[0] + sh_ref[0]
        hn_ref[...] = h.astype(jnp.bfloat16)

    z_ref[0] = jnp.dot(hn_ref[...], w_ref[...], preferred_element_type=jnp.float32)


def _norm_matmul(x, sc, sh, w, tm, tn):
    B, T, D = x.shape
    N = w.shape[1]
    bm = (lambda b, i, j: (b, 0, 0)) if sc.shape[0] == B else (lambda b, i, j: (0, 0, 0))
    return pl.pallas_call(
        _norm_matmul_kernel,
        grid=(B, T // tm, N // tn),
        in_specs=[
            pl.BlockSpec((1, tm, D), lambda b, i, j: (b, i, 0)),
            pl.BlockSpec((1, 1, D), bm),
            pl.BlockSpec((1, 1, D), bm),
            pl.BlockSpec((D, tn), lambda b, i, j: (0, j)),
        ],
        out_specs=pl.BlockSpec((1, tm, tn), lambda b, i, j: (b, i, j)),
        out_shape=jax.ShapeDtypeStruct((B, T, N), jnp.float32),
        scratch_shapes=[pltpu.VMEM((tm, D), jnp.bfloat16)],
        compiler_params=_cparams(("parallel", "parallel", "arbitrary")),
        name="norm_matmul",
    )(x, sc, sh, w)


def _group_mean_sq(x, avg):
    sq = x * x
    hi = sq.astype(jnp.bfloat16)
    lo = (sq - hi.astype(jnp.float32)).astype(jnp.bfloat16)
    return (jnp.dot(hi, avg, preferred_element_type=jnp.float32)
            + jnp.dot(lo, avg, preferred_element_type=jnp.float32))


def _head_sum(x, ones_bd):
    hi = x.astype(jnp.bfloat16)
    lo = (x - hi.astype(jnp.float32)).astype(jnp.bfloat16)
    return (jnp.dot(hi, ones_bd, preferred_element_type=jnp.float32)
            + jnp.dot(lo, ones_bd, preferred_element_type=jnp.float32))


def _rope(x, cos, sin_signed, first_half):
    partner = jnp.where(first_half, pltpu.roll(x, LANE - ROPE_FREQS, axis=1), pltpu.roll(x, ROPE_FREQS, axis=1))
    return x * cos + partner * sin_signed


def _diff_attn_kernel(lam_ref, q_ref, kx_ref, vx_ref, kc_ref, vc_ref, cosq_ref, sinq_ref, cosk_ref, sink_ref,
                      qg_ref, kg_ref, og_ref, avg64_ref, o_ref, k_s, v_s, *, n_ctx, out_scale):
    lane = lax.broadcasted_iota(jnp.int32, (1, LANE), 1)
    first_half = (lane % (2 * ROPE_FREQS)) < ROPE_FREQS
    avg64 = avg64_ref[...]

    @pl.when(pl.program_id(2) == 0)
    def _():
        kc = kc_ref[0]
        kc = kc * lax.rsqrt(_group_mean_sq(kc, avg64) + EPS) * kg_ref[...]
        k_s[0:n_ctx, :] = kc.astype(jnp.bfloat16)
        kx = kx_ref[0]
        kx = kx * lax.rsqrt(_group_mean_sq(kx, avg64) + EPS) * kg_ref[...]
        kx = _rope(kx, cosk_ref[...], sink_ref[...], first_half)
        k_s[n_ctx:, :] = kx.astype(jnp.bfloat16)
        v_s[0:n_ctx, :] = vc_ref[0].astype(jnp.bfloat16)
        v_s[n_ctx:, :] = vx_ref[0].astype(jnp.bfloat16)

    q = q_ref[0]
    q = q * lax.rsqrt(_group_mean_sq(q, avg64) + EPS) * qg_ref[...]
    q = _rope(q, cosq_ref[...], sinq_ref[...], first_half) * (DA_QK ** -0.5 * math.log2(math.e))
    k = k_s[...]
    ps = []
    for m in range(2):
        in_map = (lane // DA_QK) == m
        qm = jnp.where(in_map, q, 0.0).astype(jnp.bfloat16)
        s = lax.dot_general(qm, k, (((1,), (1,)), ((), ())), preferred_element_type=jnp.float32)
        p = jnp.exp2(s - jnp.max(s, axis=-1, keepdims=True))
        scale = (1.0 if m == 0 else lam_ref[0]) / jnp.sum(p, axis=-1, keepdims=True)
        ps.append(p * scale)
    pd = (ps[0] - ps[1]).astype(jnp.bfloat16)
    o = jnp.dot(pd, v_s[...], preferred_element_type=jnp.float32)
    o = o * lax.rsqrt(jnp.mean(o * o, axis=-1, keepdims=True) + EPS) * og_ref[...] * out_scale
    o_ref[0] = o.astype(o_ref.dtype)


def _diff_attention(lam, zx, zc, cos, sin_signed, qg, kg, og, avg64, tq, out_scale):
    B, S, _ = zx.shape
    C = zc.shape[1]
    qb, kb, vb = COL_Q // LANE, COL_K // LANE, COL_V // LANE
    kern = functools.partial(_diff_attn_kernel, n_ctx=C, out_scale=out_scale)
    const = lambda b, h, i: (0, 0)
    return pl.pallas_call(
        kern,
        grid=(B, DA_HEADS, S // tq),
        in_specs=[
            pl.BlockSpec(memory_space=pltpu.SMEM),
            pl.BlockSpec((1, tq, LANE), lambda b, h, i: (b, i, qb + h)),
            pl.BlockSpec((1, S, LANE), lambda b, h, i: (b, 0, kb + h)),
            pl.BlockSpec((1, S, LANE), lambda b, h, i: (b, 0, vb + h)),
            pl.BlockSpec((1, C, LANE), lambda b, h, i: (b, 0, kb + h)),
            pl.BlockSpec((1, C, LANE), lambda b, h, i: (b, 0, vb + h)),
            pl.BlockSpec((tq, LANE), lambda b, h, i: (i, 0)),
            pl.BlockSpec((tq, LANE), lambda b, h, i: (i, 0)),
            pl.BlockSpec((S, LANE), const),
            pl.BlockSpec((S, LANE), const),
            pl.BlockSpec((1, LANE), const),
            pl.BlockSpec((1, LANE), const),
            pl.BlockSpec((1, LANE), const),
            pl.BlockSpec((LANE, LANE), const),
        ],
        out_specs=pl.BlockSpec((1, tq, LANE), lambda b, h, i: (b, i, h)),
        out_shape=jax.ShapeDtypeStruct((B, S, DA_HEADS * DA_V), jnp.bfloat16),
        scratch_shapes=[pltpu.VMEM((C + S, LANE), jnp.bfloat16), pltpu.VMEM((C + S, LANE), jnp.bfloat16)],
        compiler_params=_cparams(("parallel", "parallel", "arbitrary")),
        name="diff_attention",
    )(lam, zx, zx, zx, zc, zc, cos, sin_signed, cos, sin_signed, qg, kg, og, avg64)


def _merge_kernel(x_ref, oa_ref, y0_ref, y1_ref, bonus_ref, g_ref, lng_ref, lnb_ref, ones_ref, ga_ref, gb_ref, g1_ref,
                  wa_ref, wb_ref, wo_ref, sc2_ref, sh2_ref, wq_ref, x1_ref, h2_ref, qp_ref):
    ones_bd = ones_ref[...]
    yy = y0_ref[0, 0] + y1_ref[0, 0]
    dlt = yy - _head_sum(yy, ones_bd) * (1.0 / RW_HEAD)
    var = _head_sum(dlt * dlt, ones_bd) * (1.0 / RW_HEAD)
    ob = (dlt * lax.rsqrt(var + RW_LN_EPS) * lng_ref[...] + lnb_ref[...] + bonus_ref[0]) * g_ref[0]
    ta = jnp.dot(oa_ref[0], wa_ref[...], preferred_element_type=jnp.float32)
    tb = jnp.dot(ob.astype(jnp.bfloat16), wb_ref[...], preferred_element_type=jnp.float32)
    mix = jax.nn.sigmoid(ga_ref[0]) * ta + jax.nn.sigmoid(gb_ref[0]) * tb
    y = jnp.dot(mix.astype(jnp.bfloat16), wo_ref[...], preferred_element_type=jnp.float32)
    x1 = x_ref[0] + g1_ref[0] * y
    x1_ref[0] = x1
    ms = jnp.mean(x1 * x1, axis=-1, keepdims=True)
    h2 = x1 * lax.rsqrt(ms + EPS) * sc2_ref[0] + sh2_ref[0]
    h2_ref[0] = h2.astype(h2_ref.dtype)
    qp_ref[0] = jnp.dot(h2.astype(jnp.bfloat16), wq_ref[...], preferred_element_type=jnp.float32)


def _merge(x, oa, y, bonus, g, ln_g, ln_b, ones_bd, zx, g1, wa, wb, wo, sc2, sh2, wq, tm, h2_dtype):
    B, S, D = x.shape
    NQ = wq.shape[1]
    tok = lambda b, i: (b, i, 0)
    per_b = lambda b, i: (b, 0, 0)
    const = lambda b, i: (0, 0)
    ga_blk, gb_blk = COL_GATE // D, COL_GATE // D + 1
    return pl.pallas_call(
        _merge_kernel,
        grid=(B, S // tm),
        in_specs=[
            pl.BlockSpec((1, tm, D), tok),
            pl.BlockSpec((1, tm, D), tok),
            pl.BlockSpec((1, 1, tm, D), lambda b, i: (0, b, i, 0)),
            pl.BlockSpec((1, 1, tm, D), lambda b, i: (1, b, i, 0)),
            pl.BlockSpec((1, tm, D), tok),
            pl.BlockSpec((1, tm, D), tok),
            pl.BlockSpec((1, D), const),
            pl.BlockSpec((1, D), const),
            pl.BlockSpec((D, D), const),
            pl.BlockSpec((1, tm, D), lambda b, i: (b, i, ga_blk)),
            pl.BlockSpec((1, tm, D), lambda b, i: (b, i, gb_blk)),
            pl.BlockSpec((1, 1, D), per_b),
            pl.BlockSpec((D, D), const),
            pl.BlockSpec((D, D), const),
            pl.BlockSpec((D, D), const),
            pl.BlockSpec((1, 1, D), per_b),
            pl.BlockSpec((1, 1, D), per_b),
            pl.BlockSpec((D, NQ), const),
        ],
        out_specs=[
            pl.BlockSpec((1, tm, D), tok),
            pl.BlockSpec((1, tm, D), tok),
            pl.BlockSpec((1, tm, NQ), tok),
        ],
        out_shape=[
            jax.ShapeDtypeStruct((B, S, D), jnp.float32),
            jax.ShapeDtypeStruct((B, S, D), h2_dtype),
            jax.ShapeDtypeStruct((B, S, NQ), jnp.float32),
        ],
        compiler_params=_cparams(("parallel", "parallel")),
        name="merge_peerq",
    )(x, oa, y, y, bonus, g, ln_g, ln_b, ones_bd, zx, zx, g1, wa, wb, wo, sc2, sh2, wq)


CHUNK = 64
QUAD = 4
QW = QUAD * RW_HEAD
SCAN_QUADS = 4
SCAN_BATCH = 2


def _split3(a):
    hi = a.astype(jnp.bfloat16)
    r1 = a - hi.astype(jnp.float32)
    mid = r1.astype(jnp.bfloat16)
    lo = (r1 - mid.astype(jnp.float32)).astype(jnp.bfloat16)
    return hi, mid, lo


def _dot3(m, parts):
    acc = jnp.dot(m, parts[0], preferred_element_type=jnp.float32)
    acc = acc + jnp.dot(m, parts[1], preferred_element_type=jnp.float32)
    return acc + jnp.dot(m, parts[2], preferred_element_type=jnp.float32)


def _shift3(z, prev_row, next_row, w):
    n = z.shape[0]
    row = lax.broadcasted_iota(jnp.int32, z.shape, 0)
    zm = jnp.where(row == 0, prev_row, pltpu.roll(z, 1, axis=0))
    zp = jnp.where(row == n - 1, next_row, pltpu.roll(z, n - 1, axis=0))
    return w[0:1] * zm + w[1:2] * z + w[2:3] * zp


def _rwkv_prep_kernel(z_ref, zp_ref, zn_ref, l_ref, lp_ref, ln_ref, sw_ref, swl_ref, w0_ref, a0_ref, kk_ref, ka_ref,
                      rk_ref, wl_ref, gup_ref, ones_ref, tril_ref, triu_ref,
                      at_ref, bt_ref, kt_ref, rt_ref, bh_ref, kh_ref, v_ref, pl_ref, bonus_ref, g_ref, *, n_tiles):
    i = pl.program_id(1)
    inner_lo = 1.0 - (i == 0).astype(jnp.float32)
    inner_hi = 1.0 - (i == n_tiles - 1).astype(jnp.float32)
    zs = _shift3(z_ref[0], zp_ref[0, 7:8, :] * inner_lo, zn_ref[0, 0:1, :] * inner_hi, sw_ref[...])
    ls = _shift3(l_ref[0], lp_ref[0, 7:8, :] * inner_lo, ln_ref[0, 0:1, :] * inner_hi, swl_ref[...])
    r, k, v = zs[:, 0:RW_WIDTH], zs[:, RW_WIDTH:2 * RW_WIDTH], zs[:, 2 * RW_WIDTH:3 * RW_WIDTH]
    lane = lax.broadcasted_iota(jnp.int32, (1, LANE), 1)
    wa_in = jnp.where(lane < RW_W_LORA, jnp.tanh(ls[:, 0:LANE]), ls[:, 0:LANE]).astype(jnp.bfloat16)
    lora = jnp.dot(wa_in, wl_ref[...], preferred_element_type=jnp.float32)
    ones_bd = ones_ref[...]
    kn = k * kk_ref[...]
    kk = kn * lax.rsqrt(_head_sum(kn * kn, ones_bd) + L2_EPS)
    bonus_ref[0] = _head_sum(r * k * rk_ref[...], ones_bd) * v
    g_ref[0] = jnp.dot(jax.nn.sigmoid(ls[:, LANE:2 * LANE]).astype(jnp.bfloat16), gup_ref[...],
                       preferred_element_type=jnp.float32)
    v_ref[0] = v.astype(v_ref.dtype)
    tris = (tril_ref[...], triu_ref[...])
    for d in range(2):
        lw = lora[:, d * RW_WIDTH:(d + 1) * RW_WIDTH] + w0_ref[d:d + 1, :]
        logw = -jnp.exp(-jax.nn.softplus(-lw) - 0.5)
        a = jax.nn.sigmoid(a0_ref[d:d + 1, :] + lora[:, (2 + d) * RW_WIDTH:(3 + d) * RW_WIDTH])
        k_eff = k * (1.0 + (a - 1.0) * ka_ref[...])
        b = kk * a
        parts = _split3(logw)
        incl = _dot3(tris[d], parts)
        rest = _dot3(tris[1 - d], parts) - logw
        at_ref[d, 0] = (kk * jnp.exp(incl - logw)).astype(at_ref.dtype)
        rt_ref[d, 0] = (r * jnp.exp(incl)).astype(rt_ref.dtype)
        inv = jnp.exp(-incl)
        bt_ref[d, 0] = (b * inv).astype(bt_ref.dtype)
        kt_ref[d, 0] = (k_eff * inv).astype(kt_ref.dtype)
        to_end = jnp.exp(rest)
        bh_ref[d, 0] = (b * to_end).astype(bh_ref.dtype)
        kh_ref[d, 0] = (k_eff * to_end).astype(kh_ref.dtype)
        tot = incl + rest
        for c in range(tot.shape[0] // CHUNK):
            pl_ref[d, 0, c] = jnp.exp(tot[c * CHUNK:c * CHUNK + 1])


def _rwkv_prepare(z, sw, swl, w0, a0, k_k, k_a, r_k, wl, g_up, ones_bd, tril, triu, tt):
    B, T, _ = z.shape
    nt = T // tt
    rb, lb = COL_RW // (3 * RW_WIDTH), COL_LORA // (2 * LANE)
    tb = tt // 8
    prev = lambda i: jnp.maximum(i * tb - 1, 0)
    nxt = lambda i: jnp.minimum((i + 1) * tb, T // 8 - 1)
    const2 = lambda b, i: (0, 0)
    tok = lambda b, i: (b, i, 0)
    dtok = lambda b, i: (0, b, i, 0)
    bf = jnp.bfloat16
    kern = functools.partial(_rwkv_prep_kernel, n_tiles=nt)
    return pl.pallas_call(
        kern,
        grid=(B, nt),
        in_specs=[
            pl.BlockSpec((1, tt, 3 * RW_WIDTH), lambda b, i: (b, i, rb)),
            pl.BlockSpec((1, 8, 3 * RW_WIDTH), lambda b, i: (b, prev(i), rb)),
            pl.BlockSpec((1, 8, 3 * RW_WIDTH), lambda b, i: (b, nxt(i), rb)),
            pl.BlockSpec((1, tt, 2 * LANE), lambda b, i: (b, i, lb)),
            pl.BlockSpec((1, 8, 2 * LANE), lambda b, i: (b, prev(i), lb)),
            pl.BlockSpec((1, 8, 2 * LANE), lambda b, i: (b, nxt(i), lb)),
            pl.BlockSpec((3, 3 * RW_WIDTH), const2),
            pl.BlockSpec((3, 2 * LANE), const2),
            pl.BlockSpec((2, RW_WIDTH), const2),
            pl.BlockSpec((2, RW_WIDTH), const2),
            pl.BlockSpec((1, RW_WIDTH), const2),
            pl.BlockSpec((1, RW_WIDTH), const2),
            pl.BlockSpec((1, RW_WIDTH), const2),
            pl.BlockSpec((LANE, 4 * RW_WIDTH), const2),
            pl.BlockSpec((LANE, RW_WIDTH), const2),
            pl.BlockSpec((RW_WIDTH, RW_WIDTH), const2),
            pl.BlockSpec((tt, tt), const2),
            pl.BlockSpec((tt, tt), const2),
        ],
        out_specs=[pl.BlockSpec((2, 1, tt, RW_WIDTH), dtok)] * 6 + [
            pl.BlockSpec((1, tt, RW_WIDTH), tok),
            pl.BlockSpec((2, 1, tt // CHUNK, 1, RW_WIDTH), lambda b, i: (0, b, i, 0, 0)),
            pl.BlockSpec((1, tt, RW_WIDTH), tok),
            pl.BlockSpec((1, tt, RW_WIDTH), tok),
        ],
        out_shape=[jax.ShapeDtypeStruct((2, B, T, RW_WIDTH), bf)] * 6 + [
            jax.ShapeDtypeStruct((B, T, RW_WIDTH), bf),
            jax.ShapeDtypeStruct((2, B, T // CHUNK, 1, RW_WIDTH), jnp.float32),
            jax.ShapeDtypeStruct((B, T, RW_WIDTH), jnp.float32),
            jax.ShapeDtypeStruct((B, T, RW_WIDTH), jnp.float32),
        ],
        compiler_params=_cparams(("parallel", "parallel")),
        name="rwkv_prepare",
    )(z, z, z, z, z, z, sw, swl, w0, a0, k_k, k_a, r_k, wl, g_up, ones_bd, tril, triu)


_NT = (((1,), (1,)), ((), ()))
_TN = (((0,), (0,)), ((), ()))


def _wkv_scan_kernel(at_ref, bt_ref, kt_ref, rt_ref, bh_ref, kh_ref, v_ref, pl_ref, s0_ref, y_ref, sf_ref, s_scr,
                     *, n_chunks):
    d = pl.program_id(0)
    c = pl.program_id(3)
    f32, bf = jnp.float32, jnp.bfloat16

    @pl.when(c == 0)
    def _():
        s_scr[...] = s0_ref[0]

    row = lax.broadcasted_iota(jnp.int32, (QW, QW), 0)
    col = lax.broadcasted_iota(jnp.int32, (QW, QW), 1)
    same_head = (row // RW_HEAD) == (col // RW_HEAD)
    tok = lax.broadcasted_iota(jnp.int32, (CHUNK, QW), 0)
    src = lax.broadcasted_iota(jnp.int32, (CHUNK, QW), 1) % RW_HEAD
    order = (tok - src) * (1 - 2 * d)
    strict = order > 0
    incl = order >= 0
    eye = (tok == src).astype(f32)
    blk_sizes = [4 << i for i in range(CHUNK.bit_length() - 2)]
    same_blk = [(tok // bs) == (src // bs) for bs in blk_sizes]

    def spread(t):
        return jnp.where(same_head, jnp.concatenate([t] * QUAD, axis=0), jnp.zeros((), t.dtype))

    def mm(lhs, rhs):
        return jnp.dot(lhs.astype(bf), rhs.astype(bf), preferred_element_type=f32)

    for bi, q in [(bi, q) for bi in range(at_ref.shape[1]) for q in range(SCAN_QUADS)]:
        sl = slice(q * QW, (q + 1) * QW)
        a, b, k, r = at_ref[0, bi, :, sl], bt_ref[0, bi, :, sl], kt_ref[0, bi, :, sl], rt_ref[0, bi, :, sl]
        bh, kh, v = bh_ref[0, bi, :, sl], kh_ref[0, bi, :, sl], v_ref[bi, :, sl]
        ar = jnp.concatenate([a, r], axis=0)
        xb = lax.dot_general(ar, spread(b), _NT, preferred_element_type=f32)
        xk = lax.dot_general(ar, spread(k), _NT, preferred_element_type=f32)
        m_ab = jnp.where(strict, xb[:CHUNK], 0.0)
        m_ak = jnp.where(strict, xk[:CHUNK], 0.0)
        m_rb = jnp.where(incl, xb[CHUNK:], 0.0).astype(bf)
        m_rk = jnp.where(incl, xk[CHUNK:], 0.0)
        n0 = jnp.where(same_blk[0], m_ab, 0.0).astype(bf)
        x = eye - n0.astype(f32)
        x = x + mm(x, spread(mm(n0, spread(n0)).astype(bf)))
        for lvl in range(1, len(same_blk)):
            e = jnp.where(same_blk[lvl] & ~same_blk[lvl - 1], m_ab, 0.0).astype(bf)
            x = x - mm(mm(x, spread(e)), spread(x.astype(bf)))
        tb = x.astype(bf)
        w = mm(jnp.concatenate([m_ak, m_rk], axis=0), spread(v))
        uu = mm(tb, jnp.concatenate([spread(a), spread(w[:CHUNK].astype(bf))], axis=1))
        ua, u0 = uu[:, :QW].astype(bf), uu[:, QW:].astype(bf)
        rr = mm(m_rb, jnp.concatenate([spread(ua), spread(u0)], axis=1))
        ry = r.astype(f32) - rr[:, :QW]
        y0 = w[CHUNK:] - rr[:, QW:]
        gm = jnp.where(same_head, lax.dot_general(ua, bh, _TN, preferred_element_type=f32), 0.0)
        hm = jnp.where(same_head, lax.dot_general(v, kh, _TN, preferred_element_type=f32)
                       - lax.dot_general(u0, bh, _TN, preferred_element_type=f32), 0.0)
        s = s_scr[bi, q]
        sb = s.astype(bf)
        y_ref[0, bi, :, sl] = lax.dot_general(ry.astype(bf), sb, _NT, preferred_element_type=f32) + y0
        s_scr[bi, q] = s * pl_ref[0, bi, 0, :, sl] - jnp.dot(sb, gm.astype(bf), preferred_element_type=f32) + hm

    @pl.when(c == n_chunks - 1)
    def _():
        sf_ref[0] = s_scr[...]


def _wkv_scan(ops, v, pl_arr, s0):
    _, B, T, W = ops[0].shape
    nc = T // CHUNK
    hw = SCAN_QUADS * QW
    ng = W // hw
    nb = SCAN_BATCH if B % SCAN_BATCH == 0 else 1
    cidx = lambda d, c: c + d * (nc - 1 - 2 * c)
    op_spec = pl.BlockSpec((1, nb, CHUNK, hw), lambda d, b, g, c: (d, b, cidx(d, c), g))
    st_spec = pl.BlockSpec((1, nb, SCAN_QUADS, QW, QW), lambda d, b, g, c: (d, b, g, 0, 0))
    kern = functools.partial(_wkv_scan_kernel, n_chunks=nc)
    return pl.pallas_call(
        kern,
        grid=(2, B // nb, ng, nc),
        in_specs=[op_spec] * 6 + [
            pl.BlockSpec((nb, CHUNK, hw), lambda d, b, g, c: (b, cidx(d, c), g)),
            pl.BlockSpec((1, nb, 1, 1, hw), lambda d, b, g, c: (d, b, cidx(d, c), 0, g)),
            st_spec,
        ],
        out_specs=[op_spec, st_spec],
        out_shape=[
            jax.ShapeDtypeStruct((2, B, T, W), jnp.float32),
            jax.ShapeDtypeStruct(s0.shape, jnp.float32),
        ],
        scratch_shapes=[pltpu.VMEM((nb, SCAN_QUADS, QW, QW), jnp.float32)],
        compiler_params=_cparams(("parallel", "parallel", "parallel", "arbitrary")),
        name="wkv_scan",
    )(*ops, v, pl_arr, s0)


def _rwkv_constants(rw_shift, rw_w_up, rw_a_up, tt):
    bf = jnp.bfloat16
    sw, swl = rw_shift[:, :3 * RW_WIDTH], rw_shift[:, 3 * RW_WIDTH:]
    zero = jnp.zeros((RW_W_LORA, RW_WIDTH), jnp.float32)
    wl = jnp.concatenate([
        jnp.concatenate([rw_w_up[0], rw_w_up[1], zero, zero], axis=1),
        jnp.concatenate([zero, zero, rw_a_up[0], rw_a_up[1]], axis=1)], axis=0).astype(bf)
    t = jnp.arange(tt)
    same = (t[:, None] // CHUNK) == (t[None, :] // CHUNK)
    tril = (same & (t[None, :] <= t[:, None])).astype(bf)
    triu = (same & (t[None, :] >= t[:, None])).astype(bf)
    return sw, swl, wl, tril, triu


def _split_bf16(a):
    hi = a.astype(jnp.bfloat16)
    return hi, (a - hi.astype(jnp.float32)).astype(jnp.bfloat16)


def _top_rows(s, k):
    n = s.shape[0]
    row = lax.broadcasted_iota(jnp.int32, s.shape, 0)
    vals, poss = [], []
    for _ in range(k):
        m = jnp.max(s, axis=0, keepdims=True)
        pos = jnp.min(jnp.where(s == m, row, n), axis=0, keepdims=True)
        vals.append(m)
        poss.append(pos)
        s = jnp.where(row == pos, -jnp.inf, s)
    return jnp.concatenate(vals, axis=0), jnp.concatenate(poss, axis=0)


def _take_rows(table, idx):
    out = jnp.zeros(idx.shape, table.dtype)
    for a in range(table.shape[0]):
        out = jnp.where(idx == a, table[a:a + 1], out)
    return out


def _peer_topk_kernel(q_ref, keys_ref, eidx_ref, gate_ref):
    nt_dims = (((1,), (1,)), ((), ()))
    sv, si = [], []
    for p in range(2):
        q_hi, q_lo = _split_bf16(q_ref[0, :, p * N_KEYS:(p + 1) * N_KEYS])
        k_hi, k_lo = _split_bf16(keys_ref[0, p])
        s = (lax.dot_general(k_hi, q_hi, nt_dims, preferred_element_type=jnp.float32)
             + lax.dot_general(k_hi, q_lo, nt_dims, preferred_element_type=jnp.float32)
             + lax.dot_general(k_lo, q_hi, nt_dims, preferred_element_type=jnp.float32))
        v, i = _top_rows(s, PEER_TOPK)
        sv.append(v)
        si.append(i)
    pairs = [(a, b) for a in range(PEER_TOPK) for b in range(PEER_TOPK) if (a + 1) * (b + 1) <= PEER_TOPK]
    pad = -len(pairs) % SUBLANES
    neg = jnp.full((pad, sv[0].shape[1]), -jnp.inf, jnp.float32)
    cand = jnp.concatenate([sv[0][a:a + 1] + sv[1][b:b + 1] for a, b in pairs] + [neg], axis=0)
    cidx = jnp.concatenate([si[0][a:a + 1] * N_KEYS + si[1][b:b + 1] for a, b in pairs], axis=0)
    top_s, pos = _top_rows(cand, PEER_TOPK)
    eidx_ref[...] = _take_rows(cidx, pos)
    p = jnp.exp(top_s - top_s[0:1])
    gate_ref[...] = p / jnp.sum(p, axis=0, keepdims=True)


def _peer_topk(qp, keys, tt):
    B, S, _ = qp.shape
    nt = S // tt
    rows = PEER_HEADS * PEER_TOPK
    out_map = lambda b, i, h: (h, b * nt + i)
    return pl.pallas_call(
        _peer_topk_kernel,
        grid=(B, nt, PEER_HEADS),
        in_specs=[
            pl.BlockSpec((1, tt, PEER_QDIM), lambda b, i, h: (b, i, h)),
            pl.BlockSpec((1, 2, N_KEYS, PEER_QDIM // 2), lambda b, i, h: (h, 0, 0, 0)),
        ],
        out_specs=[pl.BlockSpec((PEER_TOPK, tt), out_map), pl.BlockSpec((PEER_TOPK, tt), out_map)],
        out_shape=[jax.ShapeDtypeStruct((rows, B * S), jnp.int32), jax.ShapeDtypeStruct((rows, B * S), jnp.float32)],
        compiler_params=_cparams(("parallel", "parallel", "arbitrary")),
        name="peer_topk",
    )(qp, keys)


PEER_PICKS = PEER_HEADS * PEER_TOPK
GATHER_SLOTS = 4


def _gelu_tanh(a):
    return 0.5 * a * (1.0 + jnp.tanh(math.sqrt(2.0 / math.pi) * (a + 0.044715 * (a * a * a))))


def _peer_gather_kernel(eidx_hbm, h2_ref, gate_ref, x1_ref, g2_ref, tab_hbm, out_ref, idx_smem, buf, idx_sem, row_sem,
                        *, tt):
    tile = pl.program_id(0) * pl.num_programs(1) + pl.program_id(1)
    idx_copy = pltpu.make_async_copy(eidx_hbm.at[tile], idx_smem, idx_sem)
    idx_copy.start()
    idx_copy.wait()

    def issue(t, slot):
        for k in range(PEER_PICKS):
            e = idx_smem[t * PEER_PICKS + k]
            pltpu.make_async_copy(tab_hbm.at[e], buf.at[slot, :, k, :], row_sem.at[slot]).start(priority=k % 2)

    def wait_rows(slot):
        pltpu.make_async_copy(buf.at[slot], buf.at[slot], row_sem.at[slot]).wait()

    lane = lax.broadcasted_iota(jnp.int32, (PEER_PICKS, tt), 1)
    n_seg = buf.shape[1]

    def compute(t, slot):
        xrow = h2_ref[0, pl.ds(t, 1), :]
        prod = None
        for s in range(n_seg):
            u = pltpu.bitcast(buf[slot, s] & jnp.uint32(0xFFFF0000), jnp.float32)
            term = u * xrow[:, s * LANE:(s + 1) * LANE]
            prod = term if prod is None else prod + term
        act = jnp.sum(prod, axis=1, keepdims=True)
        gate = jnp.sum(jnp.where(lane == t, gate_ref[...], 0.0), axis=1, keepdims=True)
        coef = gate * _gelu_tanh(act)
        ys = [jnp.sum(coef * pltpu.bitcast(buf[slot, s] << 16, jnp.float32), axis=0, keepdims=True)
              for s in range(n_seg)]
        out_ref[0, pl.ds(t, 1), :] = x1_ref[0, pl.ds(t, 1), :] + g2_ref[0] * jnp.concatenate(ys, axis=1)

    for j in range(GATHER_SLOTS):
        issue(j, j)

    def step(t, j, prefetch):
        wait_rows(j)
        compute(t, j)
        if prefetch:
            issue(t + GATHER_SLOTS, j)

    def group(g, carry):
        for j in range(GATHER_SLOTS):
            step(g * GATHER_SLOTS + j, j, True)
        return carry

    n_groups = tt // GATHER_SLOTS
    lax.fori_loop(0, n_groups - 1, group, 0)
    for j in range(GATHER_SLOTS):
        step((n_groups - 1) * GATHER_SLOTS + j, j, False)


def _peer_gather(eidx_t, h2, gate_t, x1, g2, tab, tt):
    B, S, D = x1.shape
    nt = S // tt
    tok = lambda b, i: (b, i, 0)
    kern = functools.partial(_peer_gather_kernel, tt=tt)
    return pl.pallas_call(
        kern,
        grid=(B, nt),
        in_specs=[
            pl.BlockSpec(memory_space=pl.ANY),
            pl.BlockSpec((1, tt, D), tok),
            pl.BlockSpec((PEER_PICKS, tt), lambda b, i: (0, b * nt + i)),
            pl.BlockSpec((1, tt, D), tok),
            pl.BlockSpec((1, 1, D), lambda b, i: (b, 0, 0)),
            pl.BlockSpec(memory_space=pl.ANY),
        ],
        out_specs=pl.BlockSpec((1, tt, D), tok),
        out_shape=jax.ShapeDtypeStruct((B, S, D), jnp.float32),
        scratch_shapes=[
            pltpu.SMEM((tt * PEER_PICKS,), jnp.int32),
            pltpu.VMEM((GATHER_SLOTS, D // LANE, PEER_PICKS, LANE), jnp.uint32),
            pltpu.SemaphoreType.DMA(()),
            pltpu.SemaphoreType.DMA((GATHER_SLOTS,)),
        ],
        compiler_params=_cparams(("arbitrary", "arbitrary")),
        name="peer_gather",
    )(eidx_t, h2, gate_t, x1, g2, tab)


def _pack_expert_table(peer_u, peer_v):
    u16 = lax.bitcast_convert_type(peer_u.astype(jnp.bfloat16), jnp.uint16).astype(jnp.uint32)
    v16 = lax.bitcast_convert_type(peer_v.astype(jnp.bfloat16), jnp.uint16).astype(jnp.uint32)
    return ((u16 << 16) | v16).reshape(peer_u.shape[0], peer_u.shape[1] // LANE, LANE)


def _rope_tables(n_tokens):
    rows = n_tokens // GRID_W
    row = jnp.repeat(jnp.arange(rows, dtype=jnp.float32), GRID_W)
    col = jnp.tile(jnp.arange(GRID_W, dtype=jnp.float32), rows)
    inv = ROPE_THETA ** (-jnp.arange(ROPE_FREQS, dtype=jnp.float32) / ROPE_FREQS)
    ang = jnp.stack([row[:, None] * inv, col[:, None] * inv], axis=1)
    cos = jnp.cos(ang)[:, None, :, None, :]
    sin = jnp.sin(ang)[:, None, :, None, :]
    cos = jnp.broadcast_to(cos, (n_tokens, 2, 2, 2, ROPE_FREQS)).reshape(n_tokens, LANE)
    sgn = jnp.array([-1.0, 1.0], jnp.float32)[None, None, None, :, None]
    sin = jnp.broadcast_to(sin * sgn, (n_tokens, 2, 2, 2, ROPE_FREQS)).reshape(n_tokens, LANE)
    return cos, sin


def kernel(x, c, ctx, c_ctx, w_mod, b_mod, norm1_g, w_in, q_norm_g, k_norm_g, diff_lambda, diff_out_g, rw_shift,
           rw_w0, rw_w_up, rw_a0, rw_a_up, rw_g_up, rw_k_k, rw_k_a, rw_r_k, rw_ln_g, rw_ln_b, w_branch_a,
           w_branch_b, w_out, norm2_g, peer_wq, peer_keys, peer_u, peer_v):
    assert w_mod.shape[0] == 1, "single-layer trunk only"
    B, S, D = x.shape
    C = ctx.shape[1]
    bf = jnp.bfloat16
    lam_init = 0.8 - 0.6 * math.exp(-0.3 * 0)

    mod_x = (jax.nn.silu(c) @ w_mod[0] + b_mod[0]).reshape(B, N_MOD, 1, D)
    mod_c = (jax.nn.silu(c_ctx) @ w_mod[0] + b_mod[0]).reshape(1, N_MOD, 1, D)
    sc1x, sh1x = norm1_g[0] * (1 + mod_x[:, 1]), mod_x[:, 0]
    sc1c, sh1c = norm1_g[0] * (1 + mod_c[:, 1]), mod_c[:, 0]
    sc2x, sh2x = norm2_g[0] * (1 + mod_x[:, 4]), mod_x[:, 3]
    g1x, g2x = mod_x[:, 2], mod_x[:, 5]

    w_in_p = jnp.concatenate([w_in[0][:, :6144], w_in[0][:, 6400:], w_in[0][:, 6144:6400]], axis=1).astype(bf)
    tm = min(1024, S)
    zx = _norm_matmul(x, sc1x, sh1x, w_in_p, tm, 768)
    zc = _norm_matmul(ctx, sc1c, sh1c, w_in_p, min(tm, C), 768)

    lam = (jnp.exp(jnp.sum(diff_lambda[0, 0] * diff_lambda[0, 1])) - jnp.exp(jnp.sum(diff_lambda[0, 2] * diff_lambda[0, 3]))
           + lam_init).reshape(1)
    cos, sin_signed = _rope_tables(S)
    qg = jnp.tile(q_norm_g[0], 2).reshape(1, LANE)
    kg = jnp.tile(k_norm_g[0], 2).reshape(1, LANE)
    og = diff_out_g[0].reshape(1, LANE)
    grp = jnp.arange(LANE) // DA_QK
    avg64 = jnp.where(grp[:, None] == grp[None, :], 1.0 / DA_QK, 0.0).astype(bf)
    o_a = _diff_attention(lam, zx, zc, cos, sin_signed, qg, kg, og, avg64, min(256, S), 1.0 - lam_init)

    hd = jnp.arange(RW_WIDTH) // RW_HEAD
    ones_bd = (hd[:, None] == hd[None, :]).astype(bf)
    row1 = lambda t: t.reshape(1, RW_WIDTH)
    prep = {}
    for name, z, T in (("ctx", zc, C), ("x", zx, S)):
        tt = min(256, T)
        sw, swl, wl, tril, triu = _rwkv_constants(rw_shift[0], rw_w_up[0], rw_a_up[0], tt)
        prep[name] = _rwkv_prepare(z, sw, swl, rw_w0[0], rw_a0[0], row1(rw_k_k[0]), row1(rw_k_a[0]), row1(rw_r_k[0]),
                                   wl, rw_g_up[0].astype(bf), ones_bd, tril, triu, tt)
    s0 = jnp.zeros((2, B, RW_WIDTH // QW, QW, QW), jnp.float32)
    pc, px = prep["ctx"], prep["x"]
    _, s_ctx = _wkv_scan(pc[0:6], pc[6], pc[7], s0)
    y_rw, _ = _wkv_scan(px[0:6], px[6], px[7], s_ctx)

    x1, h2, qp = _merge(x, o_a, y_rw, px[8], px[9], row1(rw_ln_g[0]), row1(rw_ln_b[0]), ones_bd, zx, g1x,
                        w_branch_a[0].astype(bf), w_branch_b[0].astype(bf), w_out[0].astype(bf),
                        sc2x, sh2x, peer_wq[0].astype(bf), min(256, S), jnp.float32)

    tt_k = min(256, S)
    eidx_t, gate_t = _peer_topk(qp, peer_keys[0], tt_k)
    tt_g = min(256, S)
    eidx = eidx_t.T.reshape(B * S // tt_g, tt_g * PEER_PICKS)
    tab = _pack_expert_table(peer_u[0], peer_v[0])
    return _peer_gather(eidx, h2, gate_t, x1, g2x, tab, tt_g)
```

```python
import functools
import math

import jax
import jax.numpy as jnp
from jax import lax
from jax.experimental import pallas as pl
from jax.experimental.pallas import tpu as pltpu

D_MODEL = 1024
N_MOD = 6
EPS = 1e-6
GRID_W = 64
DA_HEADS = 8
DA_QK = 64
DA_V = 2 * DA_QK
ROPE_THETA = 10000.0
ROPE_FREQS = DA_QK // 4
RW_HEADS = 16
RW_HEAD = 64
RW_WIDTH = RW_HEADS * RW_HEAD
RW_W_LORA = 64
RW_A_LORA = 64
RW_G_LORA = 128
RW_LN_EPS = 64e-5
L2_EPS = 1e-12
PEER_HEADS = 8
N_KEYS = 128
PEER_TOPK = 16
PEER_QDIM = 256
PEER_CHUNK = 128

COL_Q = 0
COL_K = 1024
COL_V = 2048
COL_RW = 3072
COL_GATE = 6144
COL_LORA = 8192
IN_WIDTH = 8448

LANE = 128
SUBLANES = 8
VMEM_LIMIT = 56 * 1024 * 1024


def _cparams(sem):
    return pltpu.CompilerParams(dimension_semantics=sem, vmem_limit_bytes=VMEM_LIMIT)


def _norm_matmul_kernel(x_ref, sc_ref, sh_ref, w_ref, z_ref, hn_ref):
    @pl.when(pl.program_id(2) == 0)
    def _():
        x = x_ref[0]
        ms = jnp.mean(x * x, axis=-1, keepdims=True)
        h = x * lax.rsqrt(ms + EPS) * sc_ref[0] + sh_ref[0]
        hn_ref[...] = h.astype(jnp.bfloat16)

    z_ref[0] = jnp.dot(hn_ref[...], w_ref[...], preferred_element_type=jnp.float32)


def _norm_matmul(x, sc, sh, w, tm, tn):
    B, T, D = x.shape
    N = w.shape[1]
    bm = (lambda b, i, j: (b, 0, 0)) if sc.shape[0] == B else (lambda b, i, j: (0, 0, 0))
    return pl.pallas_call(
        _norm_matmul_kernel,
        grid=(B, T // tm, N // tn),
        in_specs=[
            pl.BlockSpec((1, tm, D), lambda b, i, j: (b, i, 0)),
            pl.BlockSpec((1, 1, D), bm),
            pl.BlockSpec((1, 1, D), bm),
            pl.BlockSpec((D, tn), lambda b, i, j: (0, j)),
        ],
        out_specs=pl.BlockSpec((1, tm, tn), lambda b, i, j: (b, i, j)),
        out_shape=jax.ShapeDtypeStruct((B, T, N), jnp.float32),
        scratch_shapes=[pltpu.VMEM((tm, D), jnp.bfloat16)],
        compiler_params=_cparams(("parallel", "parallel", "arbitrary")),
        name="norm_matmul",
    )(x, sc, sh, w)


def _group_mean_sq(x, avg):
    sq = x * x
    hi = sq.astype(jnp.bfloat16)
    lo = (sq - hi.astype(jnp.float32)).astype(jnp.bfloat16)
    return (jnp.dot(hi, avg, preferred_element_type=jnp.float32)
            + jnp.dot(lo, avg, preferred_element_type=jnp.float32))


def _head_sum(x, ones_bd):
    hi = x.astype(jnp.bfloat16)
    lo = (x - hi.astype(jnp.float32)).astype(jnp.bfloat16)
    return (jnp.dot(hi, ones_bd, preferred_element_type=jnp.float32)
            + jnp.dot(lo, ones_bd, preferred_element_type=jnp.float32))


def _rope(x, cos, sin_signed, first_half):
    partner = jnp.where(first_half, pltpu.roll(x, LANE - ROPE_FREQS, axis=1), pltpu.roll(x, ROPE_FREQS, axis=1))
    return x * cos + partner * sin_signed


def _diff_attn_kernel(lam_ref, q_ref, kx_ref, vx_ref, kc_ref, vc_ref, cosq_ref, sinq_ref, cosk_ref, sink_ref,
                      qg_ref, kg_ref, og_ref, avg64_ref, o_ref, k_s, v_s, *, n_ctx, out_scale):
    lane = lax.broadcasted_iota(jnp.int32, (1, LANE), 1)
    first_half = (lane % (2 * ROPE_FREQS)) < ROPE_FREQS
    avg64 = avg64_ref[...]

    @pl.when(pl.program_id(2) == 0)
    def _():
        kc = kc_ref[0]
        kc = kc * lax.rsqrt(_group_mean_sq(kc, avg64) + EPS) * kg_ref[...]
        k_s[0:n_ctx, :] = kc.astype(jnp.bfloat16)
        kx = kx_ref[0]
        kx = kx * lax.rsqrt(_group_mean_sq(kx, avg64) + EPS) * kg_ref[...]
        kx = _rope(kx, cosk_ref[...], sink_ref[...], first_half)
        k_s[n_ctx:, :] = kx.astype(jnp.bfloat16)
        v_s[0:n_ctx, :] = vc_ref[0].astype(jnp.bfloat16)
        v_s[n_ctx:, :] = vx_ref[0].astype(jnp.bfloat16)

    q = q_ref[0]
    q = q * lax.rsqrt(_group_mean_sq(q, avg64) + EPS) * qg_ref[...]
    q = _rope(q, cosq_ref[...], sinq_ref[...], first_half) * (DA_QK ** -0.5 * math.log2(math.e))
    k = k_s[...]
    ps = []
    for m in range(2):
        in_map = (lane // DA_QK) == m
        qm = jnp.where(in_map, q, 0.0).astype(jnp.bfloat16)
        s = lax.dot_general(qm, k, (((1,), (1,)), ((), ())), preferred_element_type=jnp.float32)
        p = jnp.exp2(s - jnp.max(s, axis=-1, keepdims=True))
        scale = (1.0 if m == 0 else lam_ref[0]) / jnp.sum(p, axis=-1, keepdims=True)
        ps.append(p * scale)
    pd = (ps[0] - ps[1]).astype(jnp.bfloat16)
    o = jnp.dot(pd, v_s[...], preferred_element_type=jnp.float32)
    o = o * lax.rsqrt(jnp.mean(o * o, axis=-1, keepdims=True) + EPS) * og_ref[...] * out_scale
    o_ref[0] = o.astype(o_ref.dtype)


def _diff_attention(lam, zx, zc, cos, sin_signed, qg, kg, og, avg64, tq, out_scale):
    B, S, _ = zx.shape
    C = zc.shape[1]
    qb, kb, vb = COL_Q // LANE, COL_K // LANE, COL_V // LANE
    kern = functools.partial(_diff_attn_kernel, n_ctx=C, out_scale=out_scale)
    const = lambda b, h, i: (0, 0)
    return pl.pallas_call(
        kern,
        grid=(B, DA_HEADS, S // tq),
        in_specs=[
            pl.BlockSpec(memory_space=pltpu.SMEM),
            pl.BlockSpec((1, tq, LANE), lambda b, h, i: (b, i, qb + h)),
            pl.BlockSpec((1, S, LANE), lambda b, h, i: (b, 0, kb + h)),
            pl.BlockSpec((1, S, LANE), lambda b, h, i: (b, 0, vb + h)),
            pl.BlockSpec((1, C, LANE), lambda b, h, i: (b, 0, kb + h)),
            pl.BlockSpec((1, C, LANE), lambda b, h, i: (b, 0, vb + h)),
            pl.BlockSpec((tq, LANE), lambda b, h, i: (i, 0)),
            pl.BlockSpec((tq, LANE), lambda b, h, i: (i, 0)),
            pl.BlockSpec((S, LANE), const),
            pl.BlockSpec((S, LANE), const),
            pl.BlockSpec((1, LANE), const),
            pl.BlockSpec((1, LANE), const),
            pl.BlockSpec((1, LANE), const),
            pl.BlockSpec((LANE, LANE), const),
        ],
        out_specs=pl.BlockSpec((1, tq, LANE), lambda b, h, i: (b, i, h)),
        out_shape=jax.ShapeDtypeStruct((B, S, DA_HEADS * DA_V), jnp.bfloat16),
        scratch_shapes=[pltpu.VMEM((C + S, LANE), jnp.bfloat16), pltpu.VMEM((C + S, LANE), jnp.bfloat16)],
        compiler_params=_cparams(("parallel", "parallel", "arbitrary")),
        name="diff_attention",
    )(lam, zx, zx, zx, zc, zc, cos, sin_signed, cos, sin_signed, qg, kg, og, avg64)


def _merge_kernel(x_ref, oa_ref, y0_ref, y1_ref, bonus_ref, g_ref, lng_ref, lnb_ref, ones_ref, ga_ref, gb_ref, g1_ref,
                  wa_ref, wb_ref, wo_ref, sc2_ref, sh2_ref, wq_ref, x1_ref, h2_ref, qp_ref):
    ones_bd = ones_ref[...]
    yy = y0_ref[0, 0] + y1_ref[0, 0]
    dlt = yy - _head_sum(yy, ones_bd) * (1.0 / RW_HEAD)
    var = _head_sum(dlt * dlt, ones_bd) * (1.0 / RW_HEAD)
    ob = (dlt * lax.rsqrt(var + RW_LN_EPS) * lng_ref[...] + lnb_ref[...] + bonus_ref[0]) * g_ref[0]
    ta = jnp.dot(oa_ref[0], wa_ref[...], preferred_element_type=jnp.float32)
    tb = jnp.dot(ob.astype(jnp.bfloat16), wb_ref[...], preferred_element_type=jnp.float32)
    mix = jax.nn.sigmoid(ga_ref[0]) * ta + jax.nn.sigmoid(gb_ref[0]) * tb
    y = jnp.dot(mix.astype(jnp.bfloat16), wo_ref[...], preferred_element_type=jnp.float32)
    x1 = x_ref[0] + g1_ref[0] * y
    x1_ref[0] = x1
    ms = jnp.mean(x1 * x1, axis=-1, keepdims=True)
    h2 = x1 * lax.rsqrt(ms + EPS) * sc2_ref[0] + sh2_ref[0]
    h2_ref[0] = h2.astype(h2_ref.dtype)
    qp_ref[0] = jnp.dot(h2.astype(jnp.bfloat16), wq_ref[...], preferred_element_type=jnp.float32)


def _merge(x, oa, y, bonus, g, ln_g, ln_b, ones_bd, zx, g1, wa, wb, wo, sc2, sh2, wq, tm, h2_dtype):
    B, S, D = x.shape
    NQ = wq.shape[1]
    tok = lambda b, i: (b, i, 0)
    per_b = lambda b, i: (b, 0, 0)
    const = lambda b, i: (0, 0)
    ga_blk, gb_blk = COL_GATE // D, COL_GATE // D + 1
    return pl.pallas_call(
        _merge_kernel,
        grid=(B, S // tm),
        in_specs=[
            pl.BlockSpec((1, tm, D), tok),
            pl.BlockSpec((1, tm, D), tok),
            pl.BlockSpec((1, 1, tm, D), lambda b, i: (0, b, i, 0)),
            pl.BlockSpec((1, 1, tm, D), lambda b, i: (1, b, i, 0)),
            pl.BlockSpec((1, tm, D), tok),
            pl.BlockSpec((1, tm, D), tok),
            pl.BlockSpec((1, D), const),
            pl.BlockSpec((1, D), const),
            pl.BlockSpec((D, D), const),
            pl.BlockSpec((1, tm, D), lambda b, i: (b, i, ga_blk)),
            pl.BlockSpec((1, tm, D), lambda b, i: (b, i, gb_blk)),
            pl.BlockSpec((1, 1, D), per_b),
            pl.BlockSpec((D, D), const),
            pl.BlockSpec((D, D), const),
            pl.BlockSpec((D, D), const),
            pl.BlockSpec((1, 1, D), per_b),
            pl.BlockSpec((1, 1, D), per_b),
            pl.BlockSpec((D, NQ), const),
        ],
        out_specs=[
            pl.BlockSpec((1, tm, D), tok),
            pl.BlockSpec((1, tm, D), tok),
            pl.BlockSpec((1, tm, NQ), tok),
        ],
        out_shape=[
            jax.ShapeDtypeStruct((B, S, D), jnp.float32),
            jax.ShapeDtypeStruct((B, S, D), h2_dtype),
            jax.ShapeDtypeStruct((B, S, NQ), jnp.float32),
        ],
        compiler_params=_cparams(("parallel", "parallel")),
        name="merge_peerq",
    )(x, oa, y, y, bonus, g, ln_g, ln_b, ones_bd, zx, zx, g1, wa, wb, wo, sc2, sh2, wq)


CHUNK = 64
QUAD = 4
QW = QUAD * RW_HEAD
SCAN_QUADS = 4
SCAN_BATCH = 2


def _split3(a):
    hi = a.astype(jnp.bfloat16)
    r1 = a - hi.astype(jnp.float32)
    mid = r1.astype(jnp.bfloat16)
    lo = (r1 - mid.astype(jnp.float32)).astype(jnp.bfloat16)
    return hi, mid, lo


def _dot3(m, parts):
    acc = jnp.dot(m, parts[0], preferred_element_type=jnp.float32)
    acc = acc + jnp.dot(m, parts[1], preferred_element_type=jnp.float32)
    return acc + jnp.dot(m, parts[2], preferred_element_type=jnp.float32)


def _shift3(z, prev_row, next_row, w):
    n = z.shape[0]
    row = lax.broadcasted_iota(jnp.int32, z.shape, 0)
    zm = jnp.where(row == 0, prev_row, pltpu.roll(z, 1, axis=0))
    zp = jnp.where(row == n - 1, next_row, pltpu.roll(z, n - 1, axis=0))
    return w[0:1] * zm + w[1:2] * z + w[2:3] * zp


def _rwkv_prep_kernel(z_ref, zp_ref, zn_ref, l_ref, lp_ref, ln_ref, sw_ref, swl_ref, w0_ref, a0_ref, kk_ref, ka_ref,
                      rk_ref, wl_ref, gup_ref, ones_ref, tril_ref, triu_ref,
                      at_ref, bt_ref, kt_ref, rt_ref, bh_ref, kh_ref, v_ref, pl_ref, bonus_ref, g_ref, *, n_tiles):
    i = pl.program_id(1)
    inner_lo = 1.0 - (i == 0).astype(jnp.float32)
    inner_hi = 1.0 - (i == n_tiles - 1).astype(jnp.float32)
    zs = _shift3(z_ref[0], zp_ref[0, 7:8, :] * inner_lo, zn_ref[0, 0:1, :] * inner_hi, sw_ref[...])
    ls = _shift3(l_ref[0], lp_ref[0, 7:8, :] * inner_lo, ln_ref[0, 0:1, :] * inner_hi, swl_ref[...])
    r, k, v = zs[:, 0:RW_WIDTH], zs[:, RW_WIDTH:2 * RW_WIDTH], zs[:, 2 * RW_WIDTH:3 * RW_WIDTH]
    lane = lax.broadcasted_iota(jnp.int32, (1, LANE), 1)
    wa_in = jnp.where(lane < RW_W_LORA, jnp.tanh(ls[:, 0:LANE]), ls[:, 0:LANE]).astype(jnp.bfloat16)
    lora = jnp.dot(wa_in, wl_ref[...], preferred_element_type=jnp.float32)
    ones_bd = ones_ref[...]
    kn = k * kk_ref[...]
    kk = kn * lax.rsqrt(_head_sum(kn * kn, ones_bd) + L2_EPS)
    bonus_ref[0] = _head_sum(r * k * rk_ref[...], ones_bd) * v
    g_ref[0] = jnp.dot(jax.nn.sigmoid(ls[:, LANE:2 * LANE]).astype(jnp.bfloat16), gup_ref[...],
                       preferred_element_type=jnp.float32)
    v_ref[0] = v.astype(v_ref.dtype)
    tris = (tril_ref[...], triu_ref[...])
    for d in range(2):
        lw = lora[:, d * RW_WIDTH:(d + 1) * RW_WIDTH] + w0_ref[d:d + 1, :]
        logw = -jnp.exp(-jax.nn.softplus(-lw) - 0.5)
        a = jax.nn.sigmoid(a0_ref[d:d + 1, :] + lora[:, (2 + d) * RW_WIDTH:(3 + d) * RW_WIDTH])
        k_eff = k * (1.0 + (a - 1.0) * ka_ref[...])
        b = kk * a
        parts = _split3(logw)
        incl = _dot3(tris[d], parts)
        rest = _dot3(tris[1 - d], parts) - logw
        at_ref[d, 0] = (kk * jnp.exp(incl - logw)).astype(at_ref.dtype)
        rt_ref[d, 0] = (r * jnp.exp(incl)).astype(rt_ref.dtype)
        inv = jnp.exp(-incl)
        bt_ref[d, 0] = (b * inv).astype(bt_ref.dtype)
        kt_ref[d, 0] = (k_eff * inv).astype(kt_ref.dtype)
        to_end = jnp.exp(rest)
        bh_ref[d, 0] = (b * to_end).astype(bh_ref.dtype)
        kh_ref[d, 0] = (k_eff * to_end).astype(kh_ref.dtype)
        tot = incl + rest
        for c in range(tot.shape[0] // CHUNK):
            pl_ref[d, 0, c] = jnp.exp(tot[c * CHUNK:c * CHUNK + 1])


def _rwkv_prepare(z, sw, swl, w0, a0, k_k, k_a, r_k, wl, g_up, ones_bd, tril, triu, tt):
    B, T, _ = z.shape
    nt = T // tt
    rb, lb = COL_RW // (3 * RW_WIDTH), COL_LORA // (2 * LANE)
    tb = tt // 8
    prev = lambda i: jnp.maximum(i * tb - 1, 0)
    nxt = lambda i: jnp.minimum((i + 1) * tb, T // 8 - 1)
    const2 = lambda b, i: (0, 0)
    tok = lambda b, i: (b, i, 0)
    dtok = lambda b, i: (0, b, i, 0)
    bf = jnp.bfloat16
    kern = functools.partial(_rwkv_prep_kernel, n_tiles=nt)
    return pl.pallas_call(
        kern,
        grid=(B, nt),
        in_specs=[
            pl.BlockSpec((1, tt, 3 * RW_WIDTH), lambda b, i: (b, i, rb)),
            pl.BlockSpec((1, 8, 3 * RW_WIDTH), lambda b, i: (b, prev(i), rb)),
            pl.BlockSpec((1, 8, 3 * RW_WIDTH), lambda b, i: (b, nxt(i), rb)),
            pl.BlockSpec((1, tt, 2 * LANE), lambda b, i: (b, i, lb)),
            pl.BlockSpec((1, 8, 2 * LANE), lambda b, i: (b, prev(i), lb)),
            pl.BlockSpec((1, 8, 2 * LANE), lambda b, i: (b, nxt(i), lb)),
            pl.BlockSpec((3, 3 * RW_WIDTH), const2),
            pl.BlockSpec((3, 2 * LANE), const2),
            pl.BlockSpec((2, RW_WIDTH), const2),
            pl.BlockSpec((2, RW_WIDTH), const2),
            pl.BlockSpec((1, RW_WIDTH), const2),
            pl.BlockSpec((1, RW_WIDTH), const2),
            pl.BlockSpec((1, RW_WIDTH), const2),
            pl.BlockSpec((LANE, 4 * RW_WIDTH), const2),
            pl.BlockSpec((LANE, RW_WIDTH), const2),
            pl.BlockSpec((RW_WIDTH, RW_WIDTH), const2),
            pl.BlockSpec((tt, tt), const2),
            pl.BlockSpec((tt, tt), const2),
        ],
        out_specs=[pl.BlockSpec((2, 1, tt, RW_WIDTH), dtok)] * 6 + [
            pl.BlockSpec((1, tt, RW_WIDTH), tok),
            pl.BlockSpec((2, 1, tt // CHUNK, 1, RW_WIDTH), lambda b, i: (0, b, i, 0, 0)),
            pl.BlockSpec((1, tt, RW_WIDTH), tok),
            pl.BlockSpec((1, tt, RW_WIDTH), tok),
        ],
        out_shape=[jax.ShapeDtypeStruct((2, B, T, RW_WIDTH), bf)] * 6 + [
            jax.ShapeDtypeStruct((B, T, RW_WIDTH), bf),
            jax.ShapeDtypeStruct((2, B, T // CHUNK, 1, RW_WIDTH), jnp.float32),
            jax.ShapeDtypeStruct((B, T, RW_WIDTH), jnp.float32),
            jax.ShapeDtypeStruct((B, T, RW_WIDTH), jnp.float32),
        ],
        compiler_params=_cparams(("parallel", "parallel")),
        name="rwkv_prepare",
    )(z, z, z, z, z, z, sw, swl, w0, a0, k_k, k_a, r_k, wl, g_up, ones_bd, tril, triu)


_NT = (((1,), (1,)), ((), ()))
_TN = (((0,), (0,)), ((), ()))


def _wkv_scan_kernel(at_ref, bt_ref, kt_ref, rt_ref, bh_ref, kh_ref, v_ref, pl_ref, s0_ref, y_ref, sf_ref, s_scr,
                     *, n_chunks):
    d = pl.program_id(0)
    c = pl.program_id(3)
    f32, bf = jnp.float32, jnp.bfloat16

    @pl.when(c == 0)
    def _():
        s_scr[...] = s0_ref[0]

    row = lax.broadcasted_iota(jnp.int32, (QW, QW), 0)
    col = lax.broadcasted_iota(jnp.int32, (QW, QW), 1)
    same_head = (row // RW_HEAD) == (col // RW_HEAD)
    tok = lax.broadcasted_iota(jnp.int32, (CHUNK, QW), 0)
    src = lax.broadcasted_iota(jnp.int32, (CHUNK, QW), 1) % RW_HEAD
    order = (tok - src) * (1 - 2 * d)
    strict = order > 0
    incl = order >= 0
    eye = (tok == src).astype(f32)
    blk_sizes = [4 << i for i in range(CHUNK.bit_length() - 2)]
    same_blk = [(tok // bs) == (src // bs) for bs in blk_sizes]

    def spread(t):
        return jnp.where(same_head, jnp.concatenate([t] * QUAD, axis=0), jnp.zeros((), t.dtype))

    def mm(lhs, rhs):
        return jnp.dot(lhs.astype(bf), rhs.astype(bf), preferred_element_type=f32)

    for bi, q in [(bi, q) for bi in range(at_ref.shape[1]) for q in range(SCAN_QUADS)]:
        sl = slice(q * QW, (q + 1) * QW)
        a, b, k, r = at_ref[0, bi, :, sl], bt_ref[0, bi, :, sl], kt_ref[0, bi, :, sl], rt_ref[0, bi, :, sl]
        bh, kh, v = bh_ref[0, bi, :, sl], kh_ref[0, bi, :, sl], v_ref[bi, :, sl]
        ar = jnp.concatenate([a, r], axis=0)
        xb = lax.dot_general(ar, spread(b), _NT, preferred_element_type=f32)
        xk = lax.dot_general(ar, spread(k), _NT, preferred_element_type=f32)
        m_ab = jnp.where(strict, xb[:CHUNK], 0.0)
        m_ak = jnp.where(strict, xk[:CHUNK], 0.0)
        m_rb = jnp.where(incl, xb[CHUNK:], 0.0).astype(bf)
        m_rk = jnp.where(incl, xk[CHUNK:], 0.0)
        n0 = jnp.where(same_blk[0], m_ab, 0.0).astype(bf)
        x = eye - n0.astype(f32)
        x = x + mm(x, spread(mm(n0, spread(n0)).astype(bf)))
        for lvl in range(1, len(same_blk)):
            e = jnp.where(same_blk[lvl] & ~same_blk[lvl - 1], m_ab, 0.0).astype(bf)
            x = x - mm(mm(x, spread(e)), spread(x.astype(bf)))
        tb = x.astype(bf)
        w = mm(jnp.concatenate([m_ak, m_rk], axis=0), spread(v))
        uu = mm(tb, jnp.concatenate([spread(a), spread(w[:CHUNK].astype(bf))], axis=1))
        ua, u0 = uu[:, :QW].astype(bf), uu[:, QW:].astype(bf)
        rr = mm(m_rb, jnp.concatenate([spread(ua), spread(u0)], axis=1))
        ry = r.astype(f32) - rr[:, :QW]
        y0 = w[CHUNK:] - rr[:, QW:]
        gm = jnp.where(same_head, lax.dot_general(ua, bh, _TN, preferred_element_type=f32), 0.0)
        hm = jnp.where(same_head, lax.dot_general(v, kh, _TN, preferred_element_type=f32)
                       - lax.dot_general(u0, bh, _TN, preferred_element_type=f32), 0.0)
        s = s_scr[bi, q]
        sb = s.astype(bf)
        y_ref[0, bi, :, sl] = lax.dot_general(ry.astype(bf), sb, _NT, preferred_element_type=f32) + y0
        s_scr[bi, q] = s * pl_ref[0, bi, 0, :, sl] - jnp.dot(sb, gm.astype(bf), preferred_element_type=f32) + hm

    @pl.when(c == n_chunks - 1)
    def _():
        sf_ref[0] = s_scr[...]


def _wkv_scan(ops, v, pl_arr, s0):
    _, B, T, W = ops[0].shape
    nc = T // CHUNK
    hw = SCAN_QUADS * QW
    ng = W // hw
    nb = SCAN_BATCH if B % SCAN_BATCH == 0 else 1
    cidx = lambda d, c: c + d * (nc - 1 - 2 * c)
    op_spec = pl.BlockSpec((1, nb, CHUNK, hw), lambda d, b, g, c: (d, b, cidx(d, c), g))
    st_spec = pl.BlockSpec((1, nb, SCAN_QUADS, QW, QW), lambda d, b, g, c: (d, b, g, 0, 0))
    kern = functools.partial(_wkv_scan_kernel, n_chunks=nc)
    return pl.pallas_call(
        kern,
        grid=(2, B // nb, ng, nc),
        in_specs=[op_spec] * 6 + [
            pl.BlockSpec((nb, CHUNK, hw), lambda d, b, g, c: (b, cidx(d, c), g)),
            pl.BlockSpec((1, nb, 1, 1, hw), lambda d, b, g, c: (d, b, cidx(d, c), 0, g)),
            st_spec,
        ],
        out_specs=[op_spec, st_spec],
        out_shape=[
            jax.ShapeDtypeStruct((2, B, T, W), jnp.float32),
            jax.ShapeDtypeStruct(s0.shape, jnp.float32),
        ],
        scratch_shapes=[pltpu.VMEM((nb, SCAN_QUADS, QW, QW), jnp.float32)],
        compiler_params=_cparams(("parallel", "parallel", "parallel", "arbitrary")),
        name="wkv_scan",
    )(*ops, v, pl_arr, s0)


def _rwkv_constants(rw_shift, rw_w_up, rw_a_up, tt):
    bf = jnp.bfloat16
    sw, swl = rw_shift[:, :3 * RW_WIDTH], rw_shift[:, 3 * RW_WIDTH:]
    zero = jnp.zeros((RW_W_LORA, RW_WIDTH), jnp.float32)
    wl = jnp.concatenate([
        jnp.concatenate([rw_w_up[0], rw_w_up[1], zero, zero], axis=1),
        jnp.concatenate([zero, zero, rw_a_up[0], rw_a_up[1]], axis=1)], axis=0).astype(bf)
    t = jnp.arange(tt)
    same = (t[:, None] // CHUNK) == (t[None, :] // CHUNK)
    tril = (same & (t[None, :] <= t[:, None])).astype(bf)
    triu = (same & (t[None, :] >= t[:, None])).astype(bf)
    return sw, swl, wl, tril, triu


def _split_bf16(a):
    hi = a.astype(jnp.bfloat16)
    return hi, (a - hi.astype(jnp.float32)).astype(jnp.bfloat16)


def _top_rows(s, k):
    n = s.shape[0]
    row = lax.broadcasted_iota(jnp.int32, s.shape, 0)
    vals, poss = [], []
    for _ in range(k):
        m = jnp.max(s, axis=0, keepdims=True)
        pos = jnp.min(jnp.where(s == m, row, n), axis=0, keepdims=True)
        vals.append(m)
        poss.append(pos)
        s = jnp.where(row == pos, -jnp.inf, s)
    return jnp.concatenate(vals, axis=0), jnp.concatenate(poss, axis=0)


def _take_rows(table, idx):
    out = jnp.zeros(idx.shape, table.dtype)
    for a in range(table.shape[0]):
        out = jnp.where(idx == a, table[a:a + 1], out)
    return out


def _peer_topk_kernel(q_ref, keys_ref, eidx_ref, gate_ref):
    nt_dims = (((1,), (1,)), ((), ()))
    sv, si = [], []
    for p in range(2):
        q_hi, q_lo = _split_bf16(q_ref[0, :, p * N_KEYS:(p + 1) * N_KEYS])
        k_hi, k_lo = _split_bf16(keys_ref[0, p])
        s = (lax.dot_general(k_hi, q_hi, nt_dims, preferred_element_type=jnp.float32)
             + lax.dot_general(k_hi, q_lo, nt_dims, preferred_element_type=jnp.float32)
             + lax.dot_general(k_lo, q_hi, nt_dims, preferred_element_type=jnp.float32))
        v, i = _top_rows(s, PEER_TOPK)
        sv.append(v)
        si.append(i)
    pairs = [(a, b) for a in range(PEER_TOPK) for b in range(PEER_TOPK) if (a + 1) * (b + 1) <= PEER_TOPK]
    pad = -len(pairs) % SUBLANES
    neg = jnp.full((pad, sv[0].shape[1]), -jnp.inf, jnp.float32)
    cand = jnp.concatenate([sv[0][a:a + 1] + sv[1][b:b + 1] for a, b in pairs] + [neg], axis=0)
    cidx = jnp.concatenate([si[0][a:a + 1] * N_KEYS + si[1][b:b + 1] for a, b in pairs], axis=0)
    top_s, pos = _top_rows(cand, PEER_TOPK)
    eidx_ref[...] = _take_rows(cidx, pos)
    p = jnp.exp(top_s - top_s[0:1])
    gate_ref[...] = p / jnp.sum(p, axis=0, keepdims=True)


def _peer_topk(qp, keys, tt):
    B, S, _ = qp.shape
    nt = S // tt
    rows = PEER_HEADS * PEER_TOPK
    out_map = lambda b, i, h: (h, b * nt + i)
    return pl.pallas_call(
        _peer_topk_kernel,
        grid=(B, nt, PEER_HEADS),
        in_specs=[
            pl.BlockSpec((1, tt, PEER_QDIM), lambda b, i, h: (b, i, h)),
            pl.BlockSpec((1, 2, N_KEYS, PEER_QDIM // 2), lambda b, i, h: (h, 0, 0, 0)),
        ],
        out_specs=[pl.BlockSpec((PEER_TOPK, tt), out_map), pl.BlockSpec((PEER_TOPK, tt), out_map)],
        out_shape=[jax.ShapeDtypeStruct((rows, B * S), jnp.int32), jax.ShapeDtypeStruct((rows, B * S), jnp.float32)],
        compiler_params=_cparams(("parallel", "parallel", "arbitrary")),
        name="peer_topk",
    )(qp, keys)


PEER_PICKS = PEER_HEADS * PEER_TOPK
GATHER_SLOTS = 4


def _gelu_tanh(a):
    return 0.5 * a * (1.0 + jnp.tanh(math.sqrt(2.0 / math.pi) * (a + 0.044715 * (a * a * a))))


def _peer_gather_kernel(eidx_hbm, h2_ref, gate_ref, x1_ref, g2_ref, tab_hbm, out_ref, idx_smem, buf, idx_sem, row_sem,
                        *, tt):
    n_seg = tab_hbm.shape[1]
    tile = pl.program_id(0) * pl.num_programs(1) + pl.program_id(1)
    idx_copy = pltpu.make_async_copy(eidx_hbm.at[tile], idx_smem, idx_sem)
    idx_copy.start()
    idx_copy.wait()

    def issue(t, slot):
        for k in range(PEER_PICKS):
            e = idx_smem[t * PEER_PICKS + k]
            pltpu.make_async_copy(tab_hbm.at[e], buf.at[slot, pl.ds(k * n_seg, n_seg), :],
                                  row_sem.at[slot]).start(priority=k % 2)

    def wait_rows(slot):
        pltpu.make_async_copy(buf.at[slot], buf.at[slot], row_sem.at[slot]).wait()

    lane = lax.broadcasted_iota(jnp.int32, (PEER_PICKS, tt), 1)

    def words(slot, s):
        return buf[slot, pl.ds(s, PEER_PICKS, stride=n_seg), :]

    def compute(t, slot):
        xrow = h2_ref[0, pl.ds(t, 1), :]
        prod = None
        for s in range(n_seg):
            u = pltpu.bitcast(words(slot, s) & jnp.uint32(0xFFFF0000), jnp.float32)
            term = u * xrow[:, s * LANE:(s + 1) * LANE]
            prod = term if prod is None else prod + term
        act = jnp.sum(prod, axis=1, keepdims=True)
        gate = jnp.sum(jnp.where(lane == t, gate_ref[...], 0.0), axis=1, keepdims=True)
        coef = gate * _gelu_tanh(act)
        ys = [jnp.sum(coef * pltpu.bitcast(words(slot, s) << 16, jnp.float32), axis=0, keepdims=True)
              for s in range(n_seg)]
        out_ref[0, pl.ds(t, 1), :] = x1_ref[0, pl.ds(t, 1), :] + g2_ref[0] * jnp.concatenate(ys, axis=1)

    for j in range(GATHER_SLOTS):
        issue(j, j)

    def step(t, j, prefetch):
        wait_rows(j)
        compute(t, j)
        if prefetch:
            issue(t + GATHER_SLOTS, j)

    def group(g, carry):
        for j in range(GATHER_SLOTS):
            step(g * GATHER_SLOTS + j, j, True)
        return carry

    n_groups = tt // GATHER_SLOTS
    lax.fori_loop(0, n_groups - 1, group, 0)
    for j in range(GATHER_SLOTS):
        step((n_groups - 1) * GATHER_SLOTS + j, j, False)


def _peer_gather(eidx_t, h2, gate_t, x1, g2, tab, tt):
    B, S, D = x1.shape
    nt = S // tt
    tok = lambda b, i: (b, i, 0)
    kern = functools.partial(_peer_gather_kernel, tt=tt)
    return pl.pallas_call(
        kern,
        grid=(B, nt),
        in_specs=[
            pl.BlockSpec(memory_space=pl.ANY),
            pl.BlockSpec((1, tt, D), tok),
            pl.BlockSpec((PEER_PICKS, tt), lambda b, i: (0, b * nt + i)),
            pl.BlockSpec((1, tt, D), tok),
            pl.BlockSpec((1, 1, D), lambda b, i: (b, 0, 0)),
            pl.BlockSpec(memory_space=pl.ANY),
        ],
        out_specs=pl.BlockSpec((1, tt, D), tok),
        out_shape=jax.ShapeDtypeStruct((B, S, D), jnp.float32),
        scratch_shapes=[
            pltpu.SMEM((tt * PEER_PICKS,), jnp.int32),
            pltpu.VMEM((GATHER_SLOTS, PEER_PICKS * (D // LANE), LANE), jnp.uint32),
            pltpu.SemaphoreType.DMA(()),
            pltpu.SemaphoreType.DMA((GATHER_SLOTS,)),
        ],
        compiler_params=_cparams(("arbitrary", "arbitrary")),
        name="peer_gather",
    )(eidx_t, h2, gate_t, x1, g2, tab)


def _pack_expert_table(peer_u, peer_v):
    u16 = lax.bitcast_convert_type(peer_u.astype(jnp.bfloat16), jnp.uint16).astype(jnp.uint32)
    v16 = lax.bitcast_convert_type(peer_v.astype(jnp.bfloat16), jnp.uint16).astype(jnp.uint32)
    return ((u16 << 16) | v16).reshape(peer_u.shape[0], peer_u.shape[1] // LANE, LANE)


def _rope_tables(n_tokens):
    rows = n_tokens // GRID_W
    row = jnp.repeat(jnp.arange(rows, dtype=jnp.float32), GRID_W)
    col = jnp.tile(jnp.arange(GRID_W, dtype=jnp.float32), rows)
    inv = ROPE_THETA ** (-jnp.arange(ROPE_FREQS, dtype=jnp.float32) / ROPE_FREQS)
    ang = jnp.stack([row[:, None] * inv, col[:, None] * inv], axis=1)
    cos = jnp.cos(ang)[:, None, :, None, :]
    sin = jnp.sin(ang)[:, None, :, None, :]
    cos = jnp.broadcast_to(cos, (n_tokens, 2, 2, 2, ROPE_FREQS)).reshape(n_tokens, LANE)
    sgn = jnp.array([-1.0, 1.0], jnp.float32)[None, None, None, :, None]
    sin = jnp.broadcast_to(sin * sgn, (n_tokens, 2, 2, 2, ROPE_FREQS)).reshape(n_tokens, LANE)
    return cos, sin


def kernel(x, c, ctx, c_ctx, w_mod, b_mod, norm1_g, w_in, q_norm_g, k_norm_g, diff_lambda, diff_out_g, rw_shift,
           rw_w0, rw_w_up, rw_a0, rw_a_up, rw_g_up, rw_k_k, rw_k_a, rw_r_k, rw_ln_g, rw_ln_b, w_branch_a,
           w_branch_b, w_out, norm2_g, peer_wq, peer_keys, peer_u, peer_v):
    assert w_mod.shape[0] == 1, "single-layer trunk only"
    B, S, D = x.shape
    C = ctx.shape[1]
    bf = jnp.bfloat16
    lam_init = 0.8 - 0.6 * math.exp(-0.3 * 0)

    mod_x = (jax.nn.silu(c) @ w_mod[0] + b_mod[0]).reshape(B, N_MOD, 1, D)
    mod_c = (jax.nn.silu(c_ctx) @ w_mod[0] + b_mod[0]).reshape(1, N_MOD, 1, D)
    sc1x, sh1x = norm1_g[0] * (1 + mod_x[:, 1]), mod_x[:, 0]
    sc1c, sh1c = norm1_g[0] * (1 + mod_c[:, 1]), mod_c[:, 0]
    sc2x, sh2x = norm2_g[0] * (1 + mod_x[:, 4]), mod_x[:, 3]
    g1x, g2x = mod_x[:, 2], mod_x[:, 5]

    w_in_p = jnp.concatenate([w_in[0][:, :6144], w_in[0][:, 6400:], w_in[0][:, 6144:6400]], axis=1).astype(bf)
    tm = min(1024, S)
    zx = _norm_matmul(x, sc1x, sh1x, w_in_p, tm, 768)
    zc = _norm_matmul(ctx, sc1c, sh1c, w_in_p, min(tm, C), 768)

    lam = (jnp.exp(jnp.sum(diff_lambda[0, 0] * diff_lambda[0, 1])) - jnp.exp(jnp.sum(diff_lambda[0, 2] * diff_lambda[0, 3]))
           + lam_init).reshape(1)
    cos, sin_signed = _rope_tables(S)
    qg = jnp.tile(q_norm_g[0], 2).reshape(1, LANE)
    kg = jnp.tile(k_norm_g[0], 2).reshape(1, LANE)
    og = diff_out_g[0].reshape(1, LANE)
    grp = jnp.arange(LANE) // DA_QK
    avg64 = jnp.where(grp[:, None] == grp[None, :], 1.0 / DA_QK, 0.0).astype(bf)
    o_a = _diff_attention(lam, zx, zc, cos, sin_signed, qg, kg, og, avg64, min(256, S), 1.0 - lam_init)

    hd = jnp.arange(RW_WIDTH) // RW_HEAD
    ones_bd = (hd[:, None] == hd[None, :]).astype(bf)
    row1 = lambda t: t.reshape(1, RW_WIDTH)
    prep = {}
    for name, z, T in (("ctx", zc, C), ("x", zx, S)):
        tt = min(256, T)
        sw, swl, wl, tril, triu = _rwkv_constants(rw_shift[0], rw_w_up[0], rw_a_up[0], tt)
        prep[name] = _rwkv_prepare(z, sw, swl, rw_w0[0], rw_a0[0], row1(rw_k_k[0]), row1(rw_k_a[0]), row1(rw_r_k[0]),
                                   wl, rw_g_up[0].astype(bf), ones_bd, tril, triu, tt)
    s0 = jnp.zeros((2, B, RW_WIDTH // QW, QW, QW), jnp.float32)
    pc, px = prep["ctx"], prep["x"]
    _, s_ctx = _wkv_scan(pc[0:6], pc[6], pc[7], s0)
    y_rw, _ = _wkv_scan(px[0:6], px[6], px[7], s_ctx)

    x1, h2, qp = _merge(x, o_a, y_rw, px[8], px[9], row1(rw_ln_g[0]), row1(rw_ln_b[0]), ones_bd, zx, g1x,
                        w_branch_a[0].astype(bf), w_branch_b[0].astype(bf), w_out[0].astype(bf),
                        sc2x, sh2x, peer_wq[0].astype(bf), min(256, S), jnp.float32)

    tt_k = min(256, S)
    eidx_t, gate_t = _peer_topk(qp, peer_keys[0], tt_k)
    tt_g = min(256, S)
    eidx = eidx_t.T.reshape(B * S // tt_g, tt_g * PEER_PICKS)
    tab = _pack_expert_table(peer_u[0], peer_v[0])
    return _peer_gather(eidx, h2, gate_t, x1, g2x, tab, tt_g)
```

```python
import functools
import math
from typing import NamedTuple

import jax
import jax.numpy as jnp
from jax import lax
from jax.experimental import pallas as pl
from jax.experimental.pallas import tpu as pltpu

N_MOD = 6
EPS = 1e-6
GRID_W = 64
DA_HEADS = 8
DA_QK = 64
DA_V = 2 * DA_QK
ROPE_THETA = 10000.0
ROPE_FREQS = DA_QK // 4
RW_HEADS = 16
RW_HEAD = 64
RW_WIDTH = RW_HEADS * RW_HEAD
RW_W_LORA = 64
RW_LN_EPS = 64e-5
L2_EPS = 1e-12
PEER_HEADS = 8
N_KEYS = 128
PEER_TOPK = 16
PEER_QDIM = 256

COL_Q = 0
COL_K = 1024
COL_V = 2048
COL_RW = 3072
COL_GATE = 6144
COL_LORA = 8192

LANE = 128
SUBLANES = 8
VMEM_LIMIT = 56 * 1024 * 1024


class Tiles(NamedTuple):
    proj_rows: int
    proj_cols: int
    attn_q: int
    rwkv: int
    merge: int
    topk: int
    gather: int


def _tiles(seq_len):
    return Tiles(proj_rows=min(1024, seq_len), proj_cols=768, attn_q=min(256, seq_len), rwkv=min(256, seq_len),
                 merge=min(256, seq_len), topk=min(256, seq_len), gather=min(512, seq_len))


def _cparams(sem):
    return pltpu.CompilerParams(dimension_semantics=sem, vmem_limit_bytes=VMEM_LIMIT)


def _adaln_kernel(c_ref, w_ref, b_ref, o_ref):
    a_hi, a_lo = _split_bf16(jax.nn.silu(c_ref[...]))
    w_hi, w_lo = _split_bf16(w_ref[...])
    dot = functools.partial(jnp.dot, preferred_element_type=jnp.float32)
    o_ref[...] = dot(a_hi, w_hi) + dot(a_hi, w_lo) + dot(a_lo, w_hi) + b_ref[...]


def _adaln(c_rows, w_mod, b_mod, tn):
    R, D = c_rows.shape
    N = w_mod.shape[1]
    return pl.pallas_call(
        _adaln_kernel,
        grid=(N // tn,),
        in_specs=[pl.BlockSpec((R, D), lambda j: (0, 0)), pl.BlockSpec((D, tn), lambda j: (0, j)),
                  pl.BlockSpec((1, tn), lambda j: (0, j))],
        out_specs=pl.BlockSpec((R, tn), lambda j: (0, j)),
        out_shape=jax.ShapeDtypeStruct((R, N), jnp.float32),
        compiler_params=_cparams(("parallel",)),
        name="adaln",
    )(c_rows, w_mod, b_mod)


def _norm_matmul_kernel(x_ref, sc_ref, sh_ref, w_ref, z_ref, hn_ref):
    @pl.when(pl.program_id(2) == 0)
    def _():
        x = x_ref[0]
        ms = jnp.mean(x * x, axis=-1, keepdims=True)
        h = x * lax.rsqrt(ms + EPS) * sc_ref[0] + sh_ref[0]
        hn_ref[...] = h.astype(jnp.bfloat16)

    z_ref[0] = jnp.dot(hn_ref[...], w_ref[...], preferred_element_type=jnp.float32)


def _norm_matmul(x, sc, sh, w, tm, tn):
    B, T, D = x.shape
    N = w.shape[1]
    bm = (lambda b, i, j: (b, 0, 0)) if sc.shape[0] == B else (lambda b, i, j: (0, 0, 0))
    return pl.pallas_call(
        _norm_matmul_kernel,
        grid=(B, T // tm, N // tn),
        in_specs=[
            pl.BlockSpec((1, tm, D), lambda b, i, j: (b, i, 0)),
            pl.BlockSpec((1, 1, D), bm),
            pl.BlockSpec((1, 1, D), bm),
            pl.BlockSpec((D, tn), lambda b, i, j: (0, j)),
        ],
        out_specs=pl.BlockSpec((1, tm, tn), lambda b, i, j: (b, i, j)),
        out_shape=jax.ShapeDtypeStruct((B, T, N), jnp.float32),
        scratch_shapes=[pltpu.VMEM((tm, D), jnp.bfloat16)],
        compiler_params=_cparams(("parallel", "parallel", "arbitrary")),
        name="norm_matmul",
    )(x, sc, sh, w)


def _group_mean_sq(x, avg):
    sq = x * x
    hi = sq.astype(jnp.bfloat16)
    lo = (sq - hi.astype(jnp.float32)).astype(jnp.bfloat16)
    return (jnp.dot(hi, avg, preferred_element_type=jnp.float32)
            + jnp.dot(lo, avg, preferred_element_type=jnp.float32))


def _head_sum(x, ones_bd):
    hi = x.astype(jnp.bfloat16)
    lo = (x - hi.astype(jnp.float32)).astype(jnp.bfloat16)
    return (jnp.dot(hi, ones_bd, preferred_element_type=jnp.float32)
            + jnp.dot(lo, ones_bd, preferred_element_type=jnp.float32))


def _rope(x, cos, sin_signed, first_half):
    partner = jnp.where(first_half, pltpu.roll(x, LANE - ROPE_FREQS, axis=1), pltpu.roll(x, ROPE_FREQS, axis=1))
    return x * cos + partner * sin_signed


def _diff_attn_kernel(lam_ref, q_ref, kx_ref, vx_ref, kc_ref, vc_ref, cosq_ref, sinq_ref, cosk_ref, sink_ref,
                      qg_ref, kg_ref, og_ref, avg64_ref, o_ref, k_s, v_s, *, n_ctx, out_scale):
    lane = lax.broadcasted_iota(jnp.int32, (1, LANE), 1)
    first_half = (lane % (2 * ROPE_FREQS)) < ROPE_FREQS
    avg64 = avg64_ref[...]

    @pl.when(pl.program_id(2) == 0)
    def _():
        kc = kc_ref[0]
        kc = kc * lax.rsqrt(_group_mean_sq(kc, avg64) + EPS) * kg_ref[...]
        k_s[0:n_ctx, :] = kc.astype(jnp.bfloat16)
        kx = kx_ref[0]
        kx = kx * lax.rsqrt(_group_mean_sq(kx, avg64) + EPS) * kg_ref[...]
        kx = _rope(kx, cosk_ref[...], sink_ref[...], first_half)
        k_s[n_ctx:, :] = kx.astype(jnp.bfloat16)
        v_s[0:n_ctx, :] = vc_ref[0].astype(jnp.bfloat16)
        v_s[n_ctx:, :] = vx_ref[0].astype(jnp.bfloat16)

    q = q_ref[0]
    q = q * lax.rsqrt(_group_mean_sq(q, avg64) + EPS) * qg_ref[...]
    q = _rope(q, cosq_ref[...], sinq_ref[...], first_half) * (DA_QK ** -0.5 * math.log2(math.e))
    k = k_s[...]
    ps = []
    for m in range(2):
        in_map = (lane // DA_QK) == m
        qm = jnp.where(in_map, q, 0.0).astype(jnp.bfloat16)
        s = lax.dot_general(qm, k, (((1,), (1,)), ((), ())), preferred_element_type=jnp.float32)
        p = jnp.exp2(s - jnp.max(s, axis=-1, keepdims=True))
        scale = (1.0 if m == 0 else lam_ref[0]) / jnp.sum(p, axis=-1, keepdims=True)
        ps.append(p * scale)
    pd = (ps[0] - ps[1]).astype(jnp.bfloat16)
    o = jnp.dot(pd, v_s[...], preferred_element_type=jnp.float32)
    o = o * lax.rsqrt(jnp.mean(o * o, axis=-1, keepdims=True) + EPS) * og_ref[...] * out_scale
    o_ref[0] = o.astype(o_ref.dtype)


def _diff_attention(lam, zx, zc, cos, sin_signed, qg, kg, og, avg64, tq, out_scale):
    B, S, _ = zx.shape
    C = zc.shape[1]
    qb, kb, vb = COL_Q // LANE, COL_K // LANE, COL_V // LANE
    kern = functools.partial(_diff_attn_kernel, n_ctx=C, out_scale=out_scale)
    const = lambda b, h, i: (0, 0)
    return pl.pallas_call(
        kern,
        grid=(B, DA_HEADS, S // tq),
        in_specs=[
            pl.BlockSpec(memory_space=pltpu.SMEM),
            pl.BlockSpec((1, tq, LANE), lambda b, h, i: (b, i, qb + h)),
            pl.BlockSpec((1, S, LANE), lambda b, h, i: (b, 0, kb + h)),
            pl.BlockSpec((1, S, LANE), lambda b, h, i: (b, 0, vb + h)),
            pl.BlockSpec((1, C, LANE), lambda b, h, i: (b, 0, kb + h)),
            pl.BlockSpec((1, C, LANE), lambda b, h, i: (b, 0, vb + h)),
            pl.BlockSpec((tq, LANE), lambda b, h, i: (i, 0)),
            pl.BlockSpec((tq, LANE), lambda b, h, i: (i, 0)),
            pl.BlockSpec((S, LANE), const),
            pl.BlockSpec((S, LANE), const),
            pl.BlockSpec((1, LANE), const),
            pl.BlockSpec((1, LANE), const),
            pl.BlockSpec((1, LANE), const),
            pl.BlockSpec((LANE, LANE), const),
        ],
        out_specs=pl.BlockSpec((1, tq, LANE), lambda b, h, i: (b, i, h)),
        out_shape=jax.ShapeDtypeStruct((B, S, DA_HEADS * DA_V), jnp.bfloat16),
        scratch_shapes=[pltpu.VMEM((C + S, LANE), jnp.bfloat16), pltpu.VMEM((C + S, LANE), jnp.bfloat16)],
        compiler_params=_cparams(("parallel", "parallel", "arbitrary")),
        name="diff_attention",
    )(lam, zx, zx, zx, zc, zc, cos, sin_signed, cos, sin_signed, qg, kg, og, avg64)


def _merge_kernel(x_ref, oa_ref, y0_ref, y1_ref, bonus_ref, g_ref, lng_ref, lnb_ref, ones_ref, ga_ref, gb_ref, g1_ref,
                  wa_ref, wb_ref, wo_ref, sc2_ref, sh2_ref, wq_ref, x1_ref, h2_ref, qp_ref):
    ones_bd = ones_ref[...]
    yy = y0_ref[0, 0] + y1_ref[0, 0]
    dlt = yy - _head_sum(yy, ones_bd) * (1.0 / RW_HEAD)
    var = _head_sum(dlt * dlt, ones_bd) * (1.0 / RW_HEAD)
    ob = (dlt * lax.rsqrt(var + RW_LN_EPS) * lng_ref[...] + lnb_ref[...] + bonus_ref[0]) * g_ref[0]
    ta = jnp.dot(oa_ref[0], wa_ref[...], preferred_element_type=jnp.float32)
    tb = jnp.dot(ob.astype(jnp.bfloat16), wb_ref[...], preferred_element_type=jnp.float32)
    mix = jax.nn.sigmoid(ga_ref[0]) * ta + jax.nn.sigmoid(gb_ref[0]) * tb
    y = jnp.dot(mix.astype(jnp.bfloat16), wo_ref[...], preferred_element_type=jnp.float32)
    x1 = x_ref[0] + g1_ref[0] * y
    x1_ref[0] = x1
    ms = jnp.mean(x1 * x1, axis=-1, keepdims=True)
    h2 = x1 * lax.rsqrt(ms + EPS) * sc2_ref[0] + sh2_ref[0]
    h2_ref[0] = h2.astype(h2_ref.dtype)
    qp_ref[0] = jnp.dot(h2.astype(jnp.bfloat16), wq_ref[...], preferred_element_type=jnp.float32)


def _merge(x, oa, y, bonus, g, ln_g, ln_b, ones_bd, zx, g1, wa, wb, wo, sc2, sh2, wq, tm, h2_dtype):
    B, S, D = x.shape
    NQ = wq.shape[1]
    tok = lambda b, i: (b, i, 0)
    per_b = lambda b, i: (b, 0, 0)
    const = lambda b, i: (0, 0)
    ga_blk, gb_blk = COL_GATE // D, COL_GATE // D + 1
    return pl.pallas_call(
        _merge_kernel,
        grid=(B, S // tm),
        in_specs=[
            pl.BlockSpec((1, tm, D), tok),
            pl.BlockSpec((1, tm, D), tok),
            pl.BlockSpec((1, 1, tm, D), lambda b, i: (0, b, i, 0)),
            pl.BlockSpec((1, 1, tm, D), lambda b, i: (1, b, i, 0)),
            pl.BlockSpec((1, tm, D), tok),
            pl.BlockSpec((1, tm, D), tok),
            pl.BlockSpec((1, D), const),
            pl.BlockSpec((1, D), const),
            pl.BlockSpec((D, D), const),
            pl.BlockSpec((1, tm, D), lambda b, i: (b, i, ga_blk)),
            pl.BlockSpec((1, tm, D), lambda b, i: (b, i, gb_blk)),
            pl.BlockSpec((1, 1, D), per_b),
            pl.BlockSpec((D, D), const),
            pl.BlockSpec((D, D), const),
            pl.BlockSpec((D, D), const),
            pl.BlockSpec((1, 1, D), per_b),
            pl.BlockSpec((1, 1, D), per_b),
            pl.BlockSpec((D, NQ), const),
        ],
        out_specs=[
            pl.BlockSpec((1, tm, D), tok),
            pl.BlockSpec((1, tm, D), tok),
            pl.BlockSpec((1, tm, NQ), tok),
        ],
        out_shape=[
            jax.ShapeDtypeStruct((B, S, D), jnp.float32),
            jax.ShapeDtypeStruct((B, S, D), h2_dtype),
            jax.ShapeDtypeStruct((B, S, NQ), jnp.float32),
        ],
        compiler_params=_cparams(("parallel", "parallel")),
        name="merge_peerq",
    )(x, oa, y, y, bonus, g, ln_g, ln_b, ones_bd, zx, zx, g1, wa, wb, wo, sc2, sh2, wq)


CHUNK = 64
QUAD = 4
QW = QUAD * RW_HEAD
SCAN_QUADS = 4
SCAN_BATCH = 2


def _split3(a):
    hi = a.astype(jnp.bfloat16)
    r1 = a - hi.astype(jnp.float32)
    mid = r1.astype(jnp.bfloat16)
    lo = (r1 - mid.astype(jnp.float32)).astype(jnp.bfloat16)
    return hi, mid, lo


def _dot3(m, parts):
    acc = jnp.dot(m, parts[0], preferred_element_type=jnp.float32)
    acc = acc + jnp.dot(m, parts[1], preferred_element_type=jnp.float32)
    return acc + jnp.dot(m, parts[2], preferred_element_type=jnp.float32)


def _shift3(z, prev_row, next_row, w):
    n = z.shape[0]
    row = lax.broadcasted_iota(jnp.int32, z.shape, 0)
    zm = jnp.where(row == 0, prev_row, pltpu.roll(z, 1, axis=0))
    zp = jnp.where(row == n - 1, next_row, pltpu.roll(z, n - 1, axis=0))
    return w[0:1] * zm + w[1:2] * z + w[2:3] * zp


def _rwkv_prep_kernel(z_ref, zp_ref, zn_ref, l_ref, lp_ref, ln_ref, sw_ref, swl_ref, w0_ref, a0_ref, kk_ref, ka_ref,
                      rk_ref, wl_ref, gup_ref, ones_ref, tril_ref, triu_ref,
                      at_ref, bt_ref, kt_ref, rt_ref, bh_ref, kh_ref, v_ref, pl_ref, bonus_ref, g_ref, *, n_tiles):
    i = pl.program_id(1)
    inner_lo = 1.0 - (i == 0).astype(jnp.float32)
    inner_hi = 1.0 - (i == n_tiles - 1).astype(jnp.float32)
    zs = _shift3(z_ref[0], zp_ref[0, 7:8, :] * inner_lo, zn_ref[0, 0:1, :] * inner_hi, sw_ref[...])
    ls = _shift3(l_ref[0], lp_ref[0, 7:8, :] * inner_lo, ln_ref[0, 0:1, :] * inner_hi, swl_ref[...])
    r, k, v = zs[:, 0:RW_WIDTH], zs[:, RW_WIDTH:2 * RW_WIDTH], zs[:, 2 * RW_WIDTH:3 * RW_WIDTH]
    lane = lax.broadcasted_iota(jnp.int32, (1, LANE), 1)
    wa_in = jnp.where(lane < RW_W_LORA, jnp.tanh(ls[:, 0:LANE]), ls[:, 0:LANE]).astype(jnp.bfloat16)
    lora = jnp.dot(wa_in, wl_ref[...], preferred_element_type=jnp.float32)
    ones_bd = ones_ref[...]
    kn = k * kk_ref[...]
    kk = kn * lax.rsqrt(_head_sum(kn * kn, ones_bd) + L2_EPS)
    bonus_ref[0] = _head_sum(r * k * rk_ref[...], ones_bd) * v
    g_ref[0] = jnp.dot(jax.nn.sigmoid(ls[:, LANE:2 * LANE]).astype(jnp.bfloat16), gup_ref[...],
                       preferred_element_type=jnp.float32)
    v_ref[0] = v.astype(v_ref.dtype)
    tris = (tril_ref[...], triu_ref[...])
    for d in range(2):
        lw = lora[:, d * RW_WIDTH:(d + 1) * RW_WIDTH] + w0_ref[d:d + 1, :]
        logw = -jnp.exp(-jax.nn.softplus(-lw) - 0.5)
        a = jax.nn.sigmoid(a0_ref[d:d + 1, :] + lora[:, (2 + d) * RW_WIDTH:(3 + d) * RW_WIDTH])
        k_eff = k * (1.0 + (a - 1.0) * ka_ref[...])
        b = kk * a
        parts = _split3(logw)
        incl = _dot3(tris[d], parts)
        rest = _dot3(tris[1 - d], parts) - logw
        at_ref[d, 0] = (kk * jnp.exp(incl - logw)).astype(at_ref.dtype)
        rt_ref[d, 0] = (r * jnp.exp(incl)).astype(rt_ref.dtype)
        inv = jnp.exp(-incl)
        bt_ref[d, 0] = (b * inv).astype(bt_ref.dtype)
        kt_ref[d, 0] = (k_eff * inv).astype(kt_ref.dtype)
        to_end = jnp.exp(rest)
        bh_ref[d, 0] = (b * to_end).astype(bh_ref.dtype)
        kh_ref[d, 0] = (k_eff * to_end).astype(kh_ref.dtype)
        tot = incl + rest
        for c in range(tot.shape[0] // CHUNK):
            pl_ref[d, 0, c] = jnp.exp(tot[c * CHUNK:c * CHUNK + 1])


def _rwkv_prepare(z, sw, swl, w0, a0, k_k, k_a, r_k, wl, g_up, ones_bd, tril, triu, tt):
    B, T, _ = z.shape
    nt = T // tt
    rb, lb = COL_RW // (3 * RW_WIDTH), COL_LORA // (2 * LANE)
    tb = tt // 8
    prev = lambda i: jnp.maximum(i * tb - 1, 0)
    nxt = lambda i: jnp.minimum((i + 1) * tb, T // 8 - 1)
    const2 = lambda b, i: (0, 0)
    tok = lambda b, i: (b, i, 0)
    dtok = lambda b, i: (0, b, i, 0)
    bf = jnp.bfloat16
    kern = functools.partial(_rwkv_prep_kernel, n_tiles=nt)
    return pl.pallas_call(
        kern,
        grid=(B, nt),
        in_specs=[
            pl.BlockSpec((1, tt, 3 * RW_WIDTH), lambda b, i: (b, i, rb)),
            pl.BlockSpec((1, 8, 3 * RW_WIDTH), lambda b, i: (b, prev(i), rb)),
            pl.BlockSpec((1, 8, 3 * RW_WIDTH), lambda b, i: (b, nxt(i), rb)),
            pl.BlockSpec((1, tt, 2 * LANE), lambda b, i: (b, i, lb)),
            pl.BlockSpec((1, 8, 2 * LANE), lambda b, i: (b, prev(i), lb)),
            pl.BlockSpec((1, 8, 2 * LANE), lambda b, i: (b, nxt(i), lb)),
            pl.BlockSpec((3, 3 * RW_WIDTH), const2),
            pl.BlockSpec((3, 2 * LANE), const2),
            pl.BlockSpec((2, RW_WIDTH), const2),
            pl.BlockSpec((2, RW_WIDTH), const2),
            pl.BlockSpec((1, RW_WIDTH), const2),
            pl.BlockSpec((1, RW_WIDTH), const2),
            pl.BlockSpec((1, RW_WIDTH), const2),
            pl.BlockSpec((LANE, 4 * RW_WIDTH), const2),
            pl.BlockSpec((LANE, RW_WIDTH), const2),
            pl.BlockSpec((RW_WIDTH, RW_WIDTH), const2),
            pl.BlockSpec((tt, tt), const2),
            pl.BlockSpec((tt, tt), const2),
        ],
        out_specs=[pl.BlockSpec((2, 1, tt, RW_WIDTH), dtok)] * 6 + [
            pl.BlockSpec((1, tt, RW_WIDTH), tok),
            pl.BlockSpec((2, 1, tt // CHUNK, 1, RW_WIDTH), lambda b, i: (0, b, i, 0, 0)),
            pl.BlockSpec((1, tt, RW_WIDTH), tok),
            pl.BlockSpec((1, tt, RW_WIDTH), tok),
        ],
        out_shape=[jax.ShapeDtypeStruct((2, B, T, RW_WIDTH), bf)] * 6 + [
            jax.ShapeDtypeStruct((B, T, RW_WIDTH), bf),
            jax.ShapeDtypeStruct((2, B, T // CHUNK, 1, RW_WIDTH), jnp.float32),
            jax.ShapeDtypeStruct((B, T, RW_WIDTH), jnp.float32),
            jax.ShapeDtypeStruct((B, T, RW_WIDTH), jnp.float32),
        ],
        compiler_params=_cparams(("parallel", "parallel")),
        name="rwkv_prepare",
    )(z, z, z, z, z, z, sw, swl, w0, a0, k_k, k_a, r_k, wl, g_up, ones_bd, tril, triu)


_NT = (((1,), (1,)), ((), ()))
_TN = (((0,), (0,)), ((), ()))


def _wkv_scan_kernel(at_ref, bt_ref, kt_ref, rt_ref, bh_ref, kh_ref, v_ref, pl_ref, s0_ref, y_ref, sf_ref, s_scr,
                     *, n_chunks):
    d = pl.program_id(0)
    c = pl.program_id(3)
    f32, bf = jnp.float32, jnp.bfloat16

    @pl.when(c == 0)
    def _():
        s_scr[...] = s0_ref[0]

    row = lax.broadcasted_iota(jnp.int32, (QW, QW), 0)
    col = lax.broadcasted_iota(jnp.int32, (QW, QW), 1)
    same_head = (row // RW_HEAD) == (col // RW_HEAD)
    tok = lax.broadcasted_iota(jnp.int32, (CHUNK, QW), 0)
    src = lax.broadcasted_iota(jnp.int32, (CHUNK, QW), 1) % RW_HEAD
    order = (tok - src) * (1 - 2 * d)
    strict = order > 0
    incl = order >= 0
    eye = (tok == src).astype(f32)
    blk_sizes = [4 << i for i in range(CHUNK.bit_length() - 2)]
    same_blk = [(tok // bs) == (src // bs) for bs in blk_sizes]

    def spread(t):
        return jnp.where(same_head, jnp.concatenate([t] * QUAD, axis=0), jnp.zeros((), t.dtype))

    def mm(lhs, rhs):
        return jnp.dot(lhs.astype(bf), rhs.astype(bf), preferred_element_type=f32)

    for bi, q in [(bi, q) for bi in range(at_ref.shape[1]) for q in range(SCAN_QUADS)]:
        sl = slice(q * QW, (q + 1) * QW)
        a, b, k, r = at_ref[0, bi, :, sl], bt_ref[0, bi, :, sl], kt_ref[0, bi, :, sl], rt_ref[0, bi, :, sl]
        bh, kh, v = bh_ref[0, bi, :, sl], kh_ref[0, bi, :, sl], v_ref[bi, :, sl]
        ar = jnp.concatenate([a, r], axis=0)
        xb = lax.dot_general(ar, spread(b), _NT, preferred_element_type=f32)
        xk = lax.dot_general(ar, spread(k), _NT, preferred_element_type=f32)
        m_ab = jnp.where(strict, xb[:CHUNK], 0.0)
        m_ak = jnp.where(strict, xk[:CHUNK], 0.0)
        m_rb = jnp.where(incl, xb[CHUNK:], 0.0).astype(bf)
        m_rk = jnp.where(incl, xk[CHUNK:], 0.0)
        n0 = jnp.where(same_blk[0], m_ab, 0.0).astype(bf)
        x = eye - n0.astype(f32)
        x = x + mm(x, spread(mm(n0, spread(n0)).astype(bf)))
        for lvl in range(1, len(same_blk)):
            e = jnp.where(same_blk[lvl] & ~same_blk[lvl - 1], m_ab, 0.0).astype(bf)
            x = x - mm(mm(x, spread(e)), spread(x.astype(bf)))
        tb = x.astype(bf)
        w = mm(jnp.concatenate([m_ak, m_rk], axis=0), spread(v))
        uu = mm(tb, jnp.concatenate([spread(a), spread(w[:CHUNK].astype(bf))], axis=1))
        ua, u0 = uu[:, :QW].astype(bf), uu[:, QW:].astype(bf)
        rr = mm(m_rb, jnp.concatenate([spread(ua), spread(u0)], axis=1))
        ry = r.astype(f32) - rr[:, :QW]
        y0 = w[CHUNK:] - rr[:, QW:]
        gm = jnp.where(same_head, lax.dot_general(ua, bh, _TN, preferred_element_type=f32), 0.0)
        hm = jnp.where(same_head, lax.dot_general(v, kh, _TN, preferred_element_type=f32)
                       - lax.dot_general(u0, bh, _TN, preferred_element_type=f32), 0.0)
        s = s_scr[bi, q]
        sb = s.astype(bf)
        y_ref[0, bi, :, sl] = lax.dot_general(ry.astype(bf), sb, _NT, preferred_element_type=f32) + y0
        s_scr[bi, q] = s * pl_ref[0, bi, 0, :, sl] - jnp.dot(sb, gm.astype(bf), preferred_element_type=f32) + hm

    @pl.when(c == n_chunks - 1)
    def _():
        sf_ref[0] = s_scr[...]


def _wkv_scan(ops, v, pl_arr, s0):
    _, B, T, W = ops[0].shape
    nc = T // CHUNK
    hw = SCAN_QUADS * QW
    ng = W // hw
    nb = SCAN_BATCH if B % SCAN_BATCH == 0 else 1
    cidx = lambda d, c: c + d * (nc - 1 - 2 * c)
    op_spec = pl.BlockSpec((1, nb, CHUNK, hw), lambda d, b, g, c: (d, b, cidx(d, c), g))
    st_spec = pl.BlockSpec((1, nb, SCAN_QUADS, QW, QW), lambda d, b, g, c: (d, b, g, 0, 0))
    kern = functools.partial(_wkv_scan_kernel, n_chunks=nc)
    return pl.pallas_call(
        kern,
        grid=(2, B // nb, ng, nc),
        in_specs=[op_spec] * 6 + [
            pl.BlockSpec((nb, CHUNK, hw), lambda d, b, g, c: (b, cidx(d, c), g)),
            pl.BlockSpec((1, nb, 1, 1, hw), lambda d, b, g, c: (d, b, cidx(d, c), 0, g)),
            st_spec,
        ],
        out_specs=[op_spec, st_spec],
        out_shape=[
            jax.ShapeDtypeStruct((2, B, T, W), jnp.float32),
            jax.ShapeDtypeStruct(s0.shape, jnp.float32),
        ],
        scratch_shapes=[pltpu.VMEM((nb, SCAN_QUADS, QW, QW), jnp.float32)],
        compiler_params=_cparams(("parallel", "parallel", "parallel", "arbitrary")),
        name="wkv_scan",
    )(*ops, v, pl_arr, s0)


def _rwkv_constants(rw_shift, rw_w_up, rw_a_up, tt):
    bf = jnp.bfloat16
    sw, swl = rw_shift[:, :3 * RW_WIDTH], rw_shift[:, 3 * RW_WIDTH:]
    zero = jnp.zeros((RW_W_LORA, RW_WIDTH), jnp.float32)
    wl = jnp.concatenate([
        jnp.concatenate([rw_w_up[0], rw_w_up[1], zero, zero], axis=1),
        jnp.concatenate([zero, zero, rw_a_up[0], rw_a_up[1]], axis=1)], axis=0).astype(bf)
    t = jnp.arange(tt)
    same = (t[:, None] // CHUNK) == (t[None, :] // CHUNK)
    tril = (same & (t[None, :] <= t[:, None])).astype(bf)
    triu = (same & (t[None, :] >= t[:, None])).astype(bf)
    return sw, swl, wl, tril, triu


def _split_bf16(a):
    hi = a.astype(jnp.bfloat16)
    return hi, (a - hi.astype(jnp.float32)).astype(jnp.bfloat16)


def _top_rows(s, k):
    n = s.shape[0]
    row = lax.broadcasted_iota(jnp.int32, s.shape, 0)
    vals, poss = [], []
    for _ in range(k):
        m = jnp.max(s, axis=0, keepdims=True)
        pos = jnp.min(jnp.where(s == m, row, n), axis=0, keepdims=True)
        vals.append(m)
        poss.append(pos)
        s = jnp.where(row == pos, -jnp.inf, s)
    return jnp.concatenate(vals, axis=0), jnp.concatenate(poss, axis=0)


def _take_rows(table, idx):
    out = jnp.zeros(idx.shape, table.dtype)
    for a in range(table.shape[0]):
        out = jnp.where(idx == a, table[a:a + 1], out)
    return out


def _peer_topk_kernel(q_ref, keys_ref, eidx_ref, gate_ref):
    nt_dims = (((1,), (1,)), ((), ()))
    sv, si = [], []
    for p in range(2):
        q_hi, q_lo = _split_bf16(q_ref[0, :, p * N_KEYS:(p + 1) * N_KEYS])
        k_hi, k_lo = _split_bf16(keys_ref[0, p])
        s = (lax.dot_general(k_hi, q_hi, nt_dims, preferred_element_type=jnp.float32)
             + lax.dot_general(k_hi, q_lo, nt_dims, preferred_element_type=jnp.float32)
             + lax.dot_general(k_lo, q_hi, nt_dims, preferred_element_type=jnp.float32))
        v, i = _top_rows(s, PEER_TOPK)
        sv.append(v)
        si.append(i)
    pairs = [(a, b) for a in range(PEER_TOPK) for b in range(PEER_TOPK) if (a + 1) * (b + 1) <= PEER_TOPK]
    pad = -len(pairs) % SUBLANES
    neg = jnp.full((pad, sv[0].shape[1]), -jnp.inf, jnp.float32)
    cand = jnp.concatenate([sv[0][a:a + 1] + sv[1][b:b + 1] for a, b in pairs] + [neg], axis=0)
    cidx = jnp.concatenate([si[0][a:a + 1] * N_KEYS + si[1][b:b + 1] for a, b in pairs], axis=0)
    top_s, pos = _top_rows(cand, PEER_TOPK)
    eidx_ref[...] = _take_rows(cidx, pos)
    p = jnp.exp(top_s - top_s[0:1])
    gate_ref[...] = p / jnp.sum(p, axis=0, keepdims=True)


def _peer_topk(qp, keys, tt):
    B, S, _ = qp.shape
    nt = S // tt
    rows = PEER_HEADS * PEER_TOPK
    out_map = lambda b, i, h: (h, b * nt + i)
    return pl.pallas_call(
        _peer_topk_kernel,
        grid=(B, nt, PEER_HEADS),
        in_specs=[
            pl.BlockSpec((1, tt, PEER_QDIM), lambda b, i, h: (b, i, h)),
            pl.BlockSpec((1, 2, N_KEYS, PEER_QDIM // 2), lambda b, i, h: (h, 0, 0, 0)),
        ],
        out_specs=[pl.BlockSpec((PEER_TOPK, tt), out_map), pl.BlockSpec((PEER_TOPK, tt), out_map)],
        out_shape=[jax.ShapeDtypeStruct((rows, B * S), jnp.int32), jax.ShapeDtypeStruct((rows, B * S), jnp.float32)],
        compiler_params=_cparams(("parallel", "parallel", "arbitrary")),
        name="peer_topk",
    )(qp, keys)


PEER_PICKS = PEER_HEADS * PEER_TOPK
GATHER_SLOTS = 4


def _gelu_tanh(a):
    return 0.5 * a * (1.0 + jnp.tanh(math.sqrt(2.0 / math.pi) * (a + 0.044715 * (a * a * a))))


def _peer_gather_kernel(eidx_hbm, h2_ref, gate_ref, x1_ref, g2_ref, tab_hbm, out_ref, idx_smem, buf, idx_sem, row_sem,
                        *, tt):
    n_seg = tab_hbm.shape[1]
    tile = pl.program_id(0) * pl.num_programs(1) + pl.program_id(1)
    idx_copy = pltpu.make_async_copy(eidx_hbm.at[tile], idx_smem, idx_sem)
    idx_copy.start()
    idx_copy.wait()

    def issue(t, slot):
        for k in range(PEER_PICKS):
            e = idx_smem[t * PEER_PICKS + k]
            pltpu.make_async_copy(tab_hbm.at[e], buf.at[slot, pl.ds(k * n_seg, n_seg), :],
                                  row_sem.at[slot]).start(priority=k % 2)

    def wait_rows(slot):
        pltpu.make_async_copy(buf.at[slot], buf.at[slot], row_sem.at[slot]).wait()

    lane = lax.broadcasted_iota(jnp.int32, (PEER_PICKS, tt), 1)

    def words(slot, s):
        return buf[slot, pl.ds(s, PEER_PICKS, stride=n_seg), :]

    def compute(t, slot):
        xrow = h2_ref[0, pl.ds(t, 1), :]
        prod = None
        for s in range(n_seg):
            u = pltpu.bitcast(words(slot, s) & jnp.uint32(0xFFFF0000), jnp.float32)
            term = u * xrow[:, s * LANE:(s + 1) * LANE]
            prod = term if prod is None else prod + term
        act = jnp.sum(prod, axis=1, keepdims=True)
        gate = jnp.sum(jnp.where(lane == t, gate_ref[...], 0.0), axis=1, keepdims=True)
        coef = gate * _gelu_tanh(act)
        ys = [jnp.sum(coef * pltpu.bitcast(words(slot, s) << 16, jnp.float32), axis=0, keepdims=True)
              for s in range(n_seg)]
        out_ref[0, pl.ds(t, 1), :] = x1_ref[0, pl.ds(t, 1), :] + g2_ref[0] * jnp.concatenate(ys, axis=1)

    for j in range(GATHER_SLOTS):
        issue(j, j)

    def step(t, j, prefetch):
        wait_rows(j)
        compute(t, j)
        if prefetch:
            issue(t + GATHER_SLOTS, j)

    def group(g, carry):
        for j in range(GATHER_SLOTS):
            step(g * GATHER_SLOTS + j, j, True)
        return carry

    n_groups = tt // GATHER_SLOTS
    lax.fori_loop(0, n_groups - 1, group, 0)
    for j in range(GATHER_SLOTS):
        step((n_groups - 1) * GATHER_SLOTS + j, j, False)


def _peer_gather(eidx_t, h2, gate_t, x1, g2, tab, tt):
    B, S, D = x1.shape
    nt = S // tt
    tok = lambda b, i: (b, i, 0)
    kern = functools.partial(_peer_gather_kernel, tt=tt)
    return pl.pallas_call(
        kern,
        grid=(B, nt),
        in_specs=[
            pl.BlockSpec(memory_space=pl.ANY),
            pl.BlockSpec((1, tt, D), tok),
            pl.BlockSpec((PEER_PICKS, tt), lambda b, i: (0, b * nt + i)),
            pl.BlockSpec((1, tt, D), tok),
            pl.BlockSpec((1, 1, D), lambda b, i: (b, 0, 0)),
            pl.BlockSpec(memory_space=pl.ANY),
        ],
        out_specs=pl.BlockSpec((1, tt, D), tok),
        out_shape=jax.ShapeDtypeStruct((B, S, D), jnp.float32),
        scratch_shapes=[
            pltpu.SMEM((tt * PEER_PICKS,), jnp.int32),
            pltpu.VMEM((GATHER_SLOTS, PEER_PICKS * (D // LANE), LANE), jnp.uint32),
            pltpu.SemaphoreType.DMA(()),
            pltpu.SemaphoreType.DMA((GATHER_SLOTS,)),
        ],
        compiler_params=_cparams(("arbitrary", "arbitrary")),
        name="peer_gather",
    )(eidx_t, h2, gate_t, x1, g2, tab)


def _pack_expert_table(peer_u, peer_v):
    u16 = lax.bitcast_convert_type(peer_u.astype(jnp.bfloat16), jnp.uint16).astype(jnp.uint32)
    v16 = lax.bitcast_convert_type(peer_v.astype(jnp.bfloat16), jnp.uint16).astype(jnp.uint32)
    return ((u16 << 16) | v16).reshape(peer_u.shape[0], peer_u.shape[1] // LANE, LANE)


def _rope_tables(n_tokens):
    rows = n_tokens // GRID_W
    row = jnp.repeat(jnp.arange(rows, dtype=jnp.float32), GRID_W)
    col = jnp.tile(jnp.arange(GRID_W, dtype=jnp.float32), rows)
    inv = ROPE_THETA ** (-jnp.arange(ROPE_FREQS, dtype=jnp.float32) / ROPE_FREQS)
    ang = jnp.stack([row[:, None] * inv, col[:, None] * inv], axis=1)
    cos = jnp.cos(ang)[:, None, :, None, :]
    sin = jnp.sin(ang)[:, None, :, None, :]
    cos = jnp.broadcast_to(cos, (n_tokens, 2, 2, 2, ROPE_FREQS)).reshape(n_tokens, LANE)
    sgn = jnp.array([-1.0, 1.0], jnp.float32)[None, None, None, :, None]
    sin = jnp.broadcast_to(sin * sgn, (n_tokens, 2, 2, 2, ROPE_FREQS)).reshape(n_tokens, LANE)
    return cos, sin


def kernel(x, c, ctx, c_ctx, w_mod, b_mod, norm1_g, w_in, q_norm_g, k_norm_g, diff_lambda, diff_out_g, rw_shift,
           rw_w0, rw_w_up, rw_a0, rw_a_up, rw_g_up, rw_k_k, rw_k_a, rw_r_k, rw_ln_g, rw_ln_b, w_branch_a,
           w_branch_b, w_out, norm2_g, peer_wq, peer_keys, peer_u, peer_v):
    assert w_mod.shape[0] == 1, "single-layer trunk only"
    B, S, D = x.shape
    C = ctx.shape[1]
    bf = jnp.bfloat16
    lam_init = 0.8 - 0.6 * math.exp(-0.3 * 0)

    tiles = _tiles(S)

    c_rows = jnp.concatenate([c, c_ctx[None, :]], axis=0)
    c_rows = jnp.pad(c_rows, ((0, -c_rows.shape[0] % SUBLANES), (0, 0)))
    mod = _adaln(c_rows, w_mod[0], b_mod[0][None, :], tiles.proj_cols)
    mod_x = mod[:B].reshape(B, N_MOD, 1, D)
    mod_c = mod[B:B + 1].reshape(1, N_MOD, 1, D)
    sc1x, sh1x = norm1_g[0] * (1 + mod_x[:, 1]), mod_x[:, 0]
    sc1c, sh1c = norm1_g[0] * (1 + mod_c[:, 1]), mod_c[:, 0]
    sc2x, sh2x = norm2_g[0] * (1 + mod_x[:, 4]), mod_x[:, 3]
    g1x, g2x = mod_x[:, 2], mod_x[:, 5]

    w_in_p = jnp.concatenate([w_in[0][:, :6144], w_in[0][:, 6400:], w_in[0][:, 6144:6400]], axis=1).astype(bf)
    zx = _norm_matmul(x, sc1x, sh1x, w_in_p, tiles.proj_rows, tiles.proj_cols)
    zc = _norm_matmul(ctx, sc1c, sh1c, w_in_p, min(tiles.proj_rows, C), tiles.proj_cols)

    lam = (jnp.exp(jnp.sum(diff_lambda[0, 0] * diff_lambda[0, 1])) - jnp.exp(jnp.sum(diff_lambda[0, 2] * diff_lambda[0, 3]))
           + lam_init).reshape(1)
    cos, sin_signed = _rope_tables(S)
    qg = jnp.tile(q_norm_g[0], 2).reshape(1, LANE)
    kg = jnp.tile(k_norm_g[0], 2).reshape(1, LANE)
    og = diff_out_g[0].reshape(1, LANE)
    grp = jnp.arange(LANE) // DA_QK
    avg64 = jnp.where(grp[:, None] == grp[None, :], 1.0 / DA_QK, 0.0).astype(bf)
    o_a = _diff_attention(lam, zx, zc, cos, sin_signed, qg, kg, og, avg64, tiles.attn_q, 1.0 - lam_init)

    hd = jnp.arange(RW_WIDTH) // RW_HEAD
    ones_bd = (hd[:, None] == hd[None, :]).astype(bf)
    row1 = lambda t: t.reshape(1, RW_WIDTH)
    prep = {}
    for name, z, T in (("ctx", zc, C), ("x", zx, S)):
        tt = min(tiles.rwkv, T)
        sw, swl, wl, tril, triu = _rwkv_constants(rw_shift[0], rw_w_up[0], rw_a_up[0], tt)
        prep[name] = _rwkv_prepare(z, sw, swl, rw_w0[0], rw_a0[0], row1(rw_k_k[0]), row1(rw_k_a[0]), row1(rw_r_k[0]),
                                   wl, rw_g_up[0].astype(bf), ones_bd, tril, triu, tt)
    s0 = jnp.zeros((2, B, RW_WIDTH // QW, QW, QW), jnp.float32)
    pc, px = prep["ctx"], prep["x"]
    _, s_ctx = _wkv_scan(pc[0:6], pc[6], pc[7], s0)
    y_rw, _ = _wkv_scan(px[0:6], px[6], px[7], s_ctx)

    x1, h2, qp = _merge(x, o_a, y_rw, px[8], px[9], row1(rw_ln_g[0]), row1(rw_ln_b[0]), ones_bd, zx, g1x,
                        w_branch_a[0].astype(bf), w_branch_b[0].astype(bf), w_out[0].astype(bf),
                        sc2x, sh2x, peer_wq[0].astype(bf), tiles.merge, jnp.float32)

    eidx_t, gate_t = _peer_topk(qp, peer_keys[0], tiles.topk)
    eidx = eidx_t.T.reshape(B * S // tiles.gather, tiles.gather * PEER_PICKS)
    tab = _pack_expert_table(peer_u[0], peer_v[0])
    return _peer_gather(eidx, h2, gate_t, x1, g2x, tab, tiles.gather)
```

```python
import functools
import math
from typing import NamedTuple

import jax
import jax.numpy as jnp
from jax import lax
from jax.experimental import pallas as pl
from jax.experimental.pallas import tpu as pltpu

N_MOD = 6
EPS = 1e-6
GRID_W = 64
DA_HEADS = 8
DA_QK = 64
DA_V = 2 * DA_QK
ROPE_THETA = 10000.0
ROPE_FREQS = DA_QK // 4
RW_HEADS = 16
RW_HEAD = 64
RW_WIDTH = RW_HEADS * RW_HEAD
RW_W_LORA = 64
RW_LN_EPS = 64e-5
L2_EPS = 1e-12
PEER_HEADS = 8
N_KEYS = 128
PEER_TOPK = 16
PEER_QDIM = 256

COL_Q = 0
COL_K = 1024
COL_V = 2048
COL_RW = 3072
COL_GATE = 6144
COL_LORA = 8192

LANE = 128
SUBLANES = 8
VMEM_LIMIT = 56 * 1024 * 1024


class Tiles(NamedTuple):
    proj_rows: int
    proj_cols: int
    attn_q: int
    rwkv: int
    merge: int
    topk: int
    gather: int


def _tiles(seq_len):
    return Tiles(proj_rows=min(1024, seq_len), proj_cols=768, attn_q=min(256, seq_len), rwkv=min(256, seq_len),
                 merge=min(256, seq_len), topk=min(256, seq_len), gather=min(512, seq_len))


def _cparams(sem):
    return pltpu.CompilerParams(dimension_semantics=sem, vmem_limit_bytes=VMEM_LIMIT)


def _adaln_kernel(c_ref, w_ref, b_ref, o_ref):
    a_hi, a_lo = _split_bf16(jax.nn.silu(c_ref[...]))
    w_hi, w_lo = _split_bf16(w_ref[...])
    dot = functools.partial(jnp.dot, preferred_element_type=jnp.float32)
    o_ref[...] = dot(a_hi, w_hi) + dot(a_hi, w_lo) + dot(a_lo, w_hi) + b_ref[...]


def _adaln(c_rows, w_mod, b_mod, tn):
    R, D = c_rows.shape
    N = w_mod.shape[1]
    return pl.pallas_call(
        _adaln_kernel,
        grid=(N // tn,),
        in_specs=[pl.BlockSpec((R, D), lambda j: (0, 0)), pl.BlockSpec((D, tn), lambda j: (0, j)),
                  pl.BlockSpec((1, tn), lambda j: (0, j))],
        out_specs=pl.BlockSpec((R, tn), lambda j: (0, j)),
        out_shape=jax.ShapeDtypeStruct((R, N), jnp.float32),
        compiler_params=_cparams(("parallel",)),
        name="adaln",
    )(c_rows, w_mod, b_mod)


def _norm_matmul_kernel(x_ref, sc_ref, sh_ref, w_ref, z_ref, hn_ref):
    @pl.when(pl.program_id(2) == 0)
    def _():
        x = x_ref[0]
        ms = jnp.mean(x * x, axis=-1, keepdims=True)
        h = x * lax.rsqrt(ms + EPS) * sc_ref[0] + sh_ref[0]
        hn_ref[...] = h.astype(jnp.bfloat16)

    z_ref[0] = jnp.dot(hn_ref[...], w_ref[...], preferred_element_type=jnp.float32)


def _norm_matmul(x, sc, sh, w, tm, tn):
    B, T, D = x.shape
    N = w.shape[1]
    bm = (lambda b, i, j: (b, 0, 0)) if sc.shape[0] == B else (lambda b, i, j: (0, 0, 0))
    return pl.pallas_call(
        _norm_matmul_kernel,
        grid=(B, T // tm, N // tn),
        in_specs=[
            pl.BlockSpec((1, tm, D), lambda b, i, j: (b, i, 0)),
            pl.BlockSpec((1, 1, D), bm),
            pl.BlockSpec((1, 1, D), bm),
            pl.BlockSpec((D, tn), lambda b, i, j: (0, j)),
        ],
        out_specs=pl.BlockSpec((1, tm, tn), lambda b, i, j: (b, i, j)),
        out_shape=jax.ShapeDtypeStruct((B, T, N), jnp.float32),
        scratch_shapes=[pltpu.VMEM((tm, D), jnp.bfloat16)],
        compiler_params=_cparams(("parallel", "parallel", "arbitrary")),
        name="norm_matmul",
    )(x, sc, sh, w)


def _group_mean_sq(x, avg):
    sq = x * x
    hi = sq.astype(jnp.bfloat16)
    lo = (sq - hi.astype(jnp.float32)).astype(jnp.bfloat16)
    return (jnp.dot(hi, avg, preferred_element_type=jnp.float32)
            + jnp.dot(lo, avg, preferred_element_type=jnp.float32))


def _head_sum(x, head_sel):
    f32 = jnp.float32
    hi, lo = _split_bf16(x)
    sums = jnp.dot(hi, head_sel, preferred_element_type=f32) + jnp.dot(lo, head_sel, preferred_element_type=f32)
    s_hi, s_lo = _split_bf16(sums)
    nt = (((1,), (1,)), ((), ()))
    return (lax.dot_general(s_hi, head_sel, nt, preferred_element_type=f32)
            + lax.dot_general(s_lo, head_sel, nt, preferred_element_type=f32))


def _rope(x, cos, sin_signed, first_half):
    partner = jnp.where(first_half, pltpu.roll(x, LANE - ROPE_FREQS, axis=1), pltpu.roll(x, ROPE_FREQS, axis=1))
    return x * cos + partner * sin_signed


def _diff_attn_kernel(lam_ref, q_ref, kx_ref, vx_ref, kc_ref, vc_ref, cosq_ref, sinq_ref, cosk_ref, sink_ref,
                      qg_ref, kg_ref, og_ref, avg64_ref, o_ref, k_s, v_s, *, n_ctx, out_scale):
    lane = lax.broadcasted_iota(jnp.int32, (1, LANE), 1)
    first_half = (lane % (2 * ROPE_FREQS)) < ROPE_FREQS
    avg64 = avg64_ref[...]

    @pl.when(pl.program_id(2) == 0)
    def _():
        kc = kc_ref[0]
        kc = kc * lax.rsqrt(_group_mean_sq(kc, avg64) + EPS) * kg_ref[...]
        k_s[0:n_ctx, :] = kc.astype(jnp.bfloat16)
        kx = kx_ref[0]
        kx = kx * lax.rsqrt(_group_mean_sq(kx, avg64) + EPS) * kg_ref[...]
        kx = _rope(kx, cosk_ref[...], sink_ref[...], first_half)
        k_s[n_ctx:, :] = kx.astype(jnp.bfloat16)
        v_s[0:n_ctx, :] = vc_ref[0].astype(jnp.bfloat16)
        v_s[n_ctx:, :] = vx_ref[0].astype(jnp.bfloat16)

    q = q_ref[0]
    q = q * lax.rsqrt(_group_mean_sq(q, avg64) + EPS) * qg_ref[...]
    q = _rope(q, cosq_ref[...], sinq_ref[...], first_half) * (DA_QK ** -0.5 * math.log2(math.e))
    k = k_s[...]
    ps = []
    for m in range(2):
        in_map = (lane // DA_QK) == m
        qm = jnp.where(in_map, q, 0.0).astype(jnp.bfloat16)
        s = lax.dot_general(qm, k, (((1,), (1,)), ((), ())), preferred_element_type=jnp.float32)
        p = jnp.exp2(s - jnp.max(s, axis=-1, keepdims=True))
        scale = (1.0 if m == 0 else lam_ref[0]) / jnp.sum(p, axis=-1, keepdims=True)
        ps.append(p * scale)
    pd = (ps[0] - ps[1]).astype(jnp.bfloat16)
    o = jnp.dot(pd, v_s[...], preferred_element_type=jnp.float32)
    o = o * lax.rsqrt(jnp.mean(o * o, axis=-1, keepdims=True) + EPS) * og_ref[...] * out_scale
    o_ref[0] = o.astype(o_ref.dtype)


def _diff_attention(lam, zx, zc, cos, sin_signed, qg, kg, og, avg64, tq, out_scale):
    B, S, _ = zx.shape
    C = zc.shape[1]
    qb, kb, vb = COL_Q // LANE, COL_K // LANE, COL_V // LANE
    kern = functools.partial(_diff_attn_kernel, n_ctx=C, out_scale=out_scale)
    const = lambda b, h, i: (0, 0)
    return pl.pallas_call(
        kern,
        grid=(B, DA_HEADS, S // tq),
        in_specs=[
            pl.BlockSpec(memory_space=pltpu.SMEM),
            pl.BlockSpec((1, tq, LANE), lambda b, h, i: (b, i, qb + h)),
            pl.BlockSpec((1, S, LANE), lambda b, h, i: (b, 0, kb + h)),
            pl.BlockSpec((1, S, LANE), lambda b, h, i: (b, 0, vb + h)),
            pl.BlockSpec((1, C, LANE), lambda b, h, i: (b, 0, kb + h)),
            pl.BlockSpec((1, C, LANE), lambda b, h, i: (b, 0, vb + h)),
            pl.BlockSpec((tq, LANE), lambda b, h, i: (i, 0)),
            pl.BlockSpec((tq, LANE), lambda b, h, i: (i, 0)),
            pl.BlockSpec((S, LANE), const),
            pl.BlockSpec((S, LANE), const),
            pl.BlockSpec((1, LANE), const),
            pl.BlockSpec((1, LANE), const),
            pl.BlockSpec((1, LANE), const),
            pl.BlockSpec((LANE, LANE), const),
        ],
        out_specs=pl.BlockSpec((1, tq, LANE), lambda b, h, i: (b, i, h)),
        out_shape=jax.ShapeDtypeStruct((B, S, DA_HEADS * DA_V), jnp.bfloat16),
        scratch_shapes=[pltpu.VMEM((C + S, LANE), jnp.bfloat16), pltpu.VMEM((C + S, LANE), jnp.bfloat16)],
        compiler_params=_cparams(("parallel", "parallel", "arbitrary")),
        name="diff_attention",
    )(lam, zx, zx, zx, zc, zc, cos, sin_signed, cos, sin_signed, qg, kg, og, avg64)


def _merge_kernel(x_ref, oa_ref, y0_ref, y1_ref, bonus_ref, g_ref, lng_ref, lnb_ref, ones_ref, ga_ref, gb_ref, g1_ref,
                  wa_ref, wb_ref, wo_ref, sc2_ref, sh2_ref, wq_ref, x1_ref, h2_ref, qp_ref):
    ones_bd = ones_ref[...]
    yy = y0_ref[0, 0] + y1_ref[0, 0]
    dlt = yy - _head_sum(yy, ones_bd) * (1.0 / RW_HEAD)
    var = _head_sum(dlt * dlt, ones_bd) * (1.0 / RW_HEAD)
    ob = (dlt * lax.rsqrt(var + RW_LN_EPS) * lng_ref[...] + lnb_ref[...] + bonus_ref[0]) * g_ref[0]
    ta = jnp.dot(oa_ref[0], wa_ref[...], preferred_element_type=jnp.float32)
    tb = jnp.dot(ob.astype(jnp.bfloat16), wb_ref[...], preferred_element_type=jnp.float32)
    mix = jax.nn.sigmoid(ga_ref[0]) * ta + jax.nn.sigmoid(gb_ref[0]) * tb
    y = jnp.dot(mix.astype(jnp.bfloat16), wo_ref[...], preferred_element_type=jnp.float32)
    x1 = x_ref[0] + g1_ref[0] * y
    x1_ref[0] = x1
    ms = jnp.mean(x1 * x1, axis=-1, keepdims=True)
    h2 = x1 * lax.rsqrt(ms + EPS) * sc2_ref[0] + sh2_ref[0]
    h2_ref[0] = h2.astype(h2_ref.dtype)
    qp_ref[0] = jnp.dot(h2.astype(jnp.bfloat16), wq_ref[...], preferred_element_type=jnp.float32)


def _merge(x, oa, y, bonus, g, ln_g, ln_b, ones_bd, zx, g1, wa, wb, wo, sc2, sh2, wq, tm, h2_dtype):
    B, S, D = x.shape
    NQ = wq.shape[1]
    tok = lambda b, i: (b, i, 0)
    per_b = lambda b, i: (b, 0, 0)
    const = lambda b, i: (0, 0)
    ga_blk, gb_blk = COL_GATE // D, COL_GATE // D + 1
    return pl.pallas_call(
        _merge_kernel,
        grid=(B, S // tm),
        in_specs=[
            pl.BlockSpec((1, tm, D), tok),
            pl.BlockSpec((1, tm, D), tok),
            pl.BlockSpec((1, 1, tm, D), lambda b, i: (0, b, i, 0)),
            pl.BlockSpec((1, 1, tm, D), lambda b, i: (1, b, i, 0)),
            pl.BlockSpec((1, tm, D), tok),
            pl.BlockSpec((1, tm, D), tok),
            pl.BlockSpec((1, D), const),
            pl.BlockSpec((1, D), const),
            pl.BlockSpec((D, LANE), const),
            pl.BlockSpec((1, tm, D), lambda b, i: (b, i, ga_blk)),
            pl.BlockSpec((1, tm, D), lambda b, i: (b, i, gb_blk)),
            pl.BlockSpec((1, 1, D), per_b),
            pl.BlockSpec((D, D), const),
            pl.BlockSpec((D, D), const),
            pl.BlockSpec((D, D), const),
            pl.BlockSpec((1, 1, D), per_b),
            pl.BlockSpec((1, 1, D), per_b),
            pl.BlockSpec((D, NQ), const),
        ],
        out_specs=[
            pl.BlockSpec((1, tm, D), tok),
            pl.BlockSpec((1, tm, D), tok),
            pl.BlockSpec((1, tm, NQ), tok),
        ],
        out_shape=[
            jax.ShapeDtypeStruct((B, S, D), jnp.float32),
            jax.ShapeDtypeStruct((B, S, D), h2_dtype),
            jax.ShapeDtypeStruct((B, S, NQ), jnp.float32),
        ],
        compiler_params=_cparams(("parallel", "parallel")),
        name="merge_peerq",
    )(x, oa, y, y, bonus, g, ln_g, ln_b, ones_bd, zx, zx, g1, wa, wb, wo, sc2, sh2, wq)


CHUNK = 64
QUAD = 4
QW = QUAD * RW_HEAD
SCAN_QUADS = 4
SCAN_BATCH = 2


def _split3(a):
    hi = a.astype(jnp.bfloat16)
    r1 = a - hi.astype(jnp.float32)
    mid = r1.astype(jnp.bfloat16)
    lo = (r1 - mid.astype(jnp.float32)).astype(jnp.bfloat16)
    return hi, mid, lo


def _dot3(m, parts):
    acc = jnp.dot(m, parts[0], preferred_element_type=jnp.float32)
    acc = acc + jnp.dot(m, parts[1], preferred_element_type=jnp.float32)
    return acc + jnp.dot(m, parts[2], preferred_element_type=jnp.float32)


def _shift3(z, prev_row, next_row, w):
    n = z.shape[0]
    row = lax.broadcasted_iota(jnp.int32, z.shape, 0)
    zm = jnp.where(row == 0, prev_row, pltpu.roll(z, 1, axis=0))
    zp = jnp.where(row == n - 1, next_row, pltpu.roll(z, n - 1, axis=0))
    return w[0:1] * zm + w[1:2] * z + w[2:3] * zp


def _rwkv_prep_kernel(z_ref, zp_ref, zn_ref, l_ref, lp_ref, ln_ref, sw_ref, swl_ref, w0_ref, a0_ref, kk_ref, ka_ref,
                      rk_ref, wl_ref, gup_ref, ones_ref, tril_ref, triu_ref,
                      at_ref, bt_ref, kt_ref, rt_ref, bh_ref, kh_ref, v_ref, pl_ref, bonus_ref, g_ref, *, n_tiles):
    i = pl.program_id(1)
    inner_lo = 1.0 - (i == 0).astype(jnp.float32)
    inner_hi = 1.0 - (i == n_tiles - 1).astype(jnp.float32)
    zs = _shift3(z_ref[0], zp_ref[0, 7:8, :] * inner_lo, zn_ref[0, 0:1, :] * inner_hi, sw_ref[...])
    ls = _shift3(l_ref[0], lp_ref[0, 7:8, :] * inner_lo, ln_ref[0, 0:1, :] * inner_hi, swl_ref[...])
    r, k, v = zs[:, 0:RW_WIDTH], zs[:, RW_WIDTH:2 * RW_WIDTH], zs[:, 2 * RW_WIDTH:3 * RW_WIDTH]
    lane = lax.broadcasted_iota(jnp.int32, (1, LANE), 1)
    wa_in = jnp.where(lane < RW_W_LORA, jnp.tanh(ls[:, 0:LANE]), ls[:, 0:LANE]).astype(jnp.bfloat16)
    lora = jnp.dot(wa_in, wl_ref[...], preferred_element_type=jnp.float32)
    ones_bd = ones_ref[...]
    kn = k * kk_ref[...]
    kk = kn * lax.rsqrt(_head_sum(kn * kn, ones_bd) + L2_EPS)
    bonus_ref[0] = _head_sum(r * k * rk_ref[...], ones_bd) * v
    g_ref[0] = jnp.dot(jax.nn.sigmoid(ls[:, LANE:2 * LANE]).astype(jnp.bfloat16), gup_ref[...],
                       preferred_element_type=jnp.float32)
    v_ref[0] = v.astype(v_ref.dtype)
    tris = (tril_ref[...], triu_ref[...])
    for d in range(2):
        lw = lora[:, d * RW_WIDTH:(d + 1) * RW_WIDTH] + w0_ref[d:d + 1, :]
        logw = -jnp.exp(-jax.nn.softplus(-lw) - 0.5)
        a = jax.nn.sigmoid(a0_ref[d:d + 1, :] + lora[:, (2 + d) * RW_WIDTH:(3 + d) * RW_WIDTH])
        k_eff = k * (1.0 + (a - 1.0) * ka_ref[...])
        b = kk * a
        parts = _split3(logw)
        incl = _dot3(tris[d], parts)
        rest = _dot3(tris[1 - d], parts) - logw
        at_ref[d, 0] = (kk * jnp.exp(incl - logw)).astype(at_ref.dtype)
        rt_ref[d, 0] = (r * jnp.exp(incl)).astype(rt_ref.dtype)
        inv = jnp.exp(-incl)
        bt_ref[d, 0] = (b * inv).astype(bt_ref.dtype)
        kt_ref[d, 0] = (k_eff * inv).astype(kt_ref.dtype)
        to_end = jnp.exp(rest)
        bh_ref[d, 0] = (b * to_end).astype(bh_ref.dtype)
        kh_ref[d, 0] = (k_eff * to_end).astype(kh_ref.dtype)
        tot = incl + rest
        for c in range(tot.shape[0] // CHUNK):
            pl_ref[d, 0, c] = jnp.exp(tot[c * CHUNK:c * CHUNK + 1])


def _rwkv_prepare(z, sw, swl, w0, a0, k_k, k_a, r_k, wl, g_up, ones_bd, tril, triu, tt):
    B, T, _ = z.shape
    nt = T // tt
    rb, lb = COL_RW // (3 * RW_WIDTH), COL_LORA // (2 * LANE)
    tb = tt // 8
    prev = lambda i: jnp.maximum(i * tb - 1, 0)
    nxt = lambda i: jnp.minimum((i + 1) * tb, T // 8 - 1)
    const2 = lambda b, i: (0, 0)
    tok = lambda b, i: (b, i, 0)
    dtok = lambda b, i: (0, b, i, 0)
    bf = jnp.bfloat16
    kern = functools.partial(_rwkv_prep_kernel, n_tiles=nt)
    return pl.pallas_call(
        kern,
        grid=(B, nt),
        in_specs=[
            pl.BlockSpec((1, tt, 3 * RW_WIDTH), lambda b, i: (b, i, rb)),
            pl.BlockSpec((1, 8, 3 * RW_WIDTH), lambda b, i: (b, prev(i), rb)),
            pl.BlockSpec((1, 8, 3 * RW_WIDTH), lambda b, i: (b, nxt(i), rb)),
            pl.BlockSpec((1, tt, 2 * LANE), lambda b, i: (b, i, lb)),
            pl.BlockSpec((1, 8, 2 * LANE), lambda b, i: (b, prev(i), lb)),
            pl.BlockSpec((1, 8, 2 * LANE), lambda b, i: (b, nxt(i), lb)),
            pl.BlockSpec((3, 3 * RW_WIDTH), const2),
            pl.BlockSpec((3, 2 * LANE), const2),
            pl.BlockSpec((2, RW_WIDTH), const2),
            pl.BlockSpec((2, RW_WIDTH), const2),
            pl.BlockSpec((1, RW_WIDTH), const2),
            pl.BlockSpec((1, RW_WIDTH), const2),
            pl.BlockSpec((1, RW_WIDTH), const2),
            pl.BlockSpec((LANE, 4 * RW_WIDTH), const2),
            pl.BlockSpec((LANE, RW_WIDTH), const2),
            pl.BlockSpec((RW_WIDTH, LANE), const2),
            pl.BlockSpec((tt, tt), const2),
            pl.BlockSpec((tt, tt), const2),
        ],
        out_specs=[pl.BlockSpec((2, 1, tt, RW_WIDTH), dtok)] * 6 + [
            pl.BlockSpec((1, tt, RW_WIDTH), tok),
            pl.BlockSpec((2, 1, tt // CHUNK, 1, RW_WIDTH), lambda b, i: (0, b, i, 0, 0)),
            pl.BlockSpec((1, tt, RW_WIDTH), tok),
            pl.BlockSpec((1, tt, RW_WIDTH), tok),
        ],
        out_shape=[jax.ShapeDtypeStruct((2, B, T, RW_WIDTH), bf)] * 6 + [
            jax.ShapeDtypeStruct((B, T, RW_WIDTH), bf),
            jax.ShapeDtypeStruct((2, B, T // CHUNK, 1, RW_WIDTH), jnp.float32),
            jax.ShapeDtypeStruct((B, T, RW_WIDTH), jnp.float32),
            jax.ShapeDtypeStruct((B, T, RW_WIDTH), jnp.float32),
        ],
        compiler_params=_cparams(("parallel", "parallel")),
        name="rwkv_prepare",
    )(z, z, z, z, z, z, sw, swl, w0, a0, k_k, k_a, r_k, wl, g_up, ones_bd, tril, triu)


_NT = (((1,), (1,)), ((), ()))
_TN = (((0,), (0,)), ((), ()))


def _wkv_scan_kernel(at_ref, bt_ref, kt_ref, rt_ref, bh_ref, kh_ref, v_ref, pl_ref, s0_ref, y_ref, sf_ref, s_scr,
                     *, n_chunks):
    d = pl.program_id(0)
    c = pl.program_id(3)
    f32, bf = jnp.float32, jnp.bfloat16

    @pl.when(c == 0)
    def _():
        s_scr[...] = s0_ref[0]

    row = lax.broadcasted_iota(jnp.int32, (QW, QW), 0)
    col = lax.broadcasted_iota(jnp.int32, (QW, QW), 1)
    same_head = (row // RW_HEAD) == (col // RW_HEAD)
    tok = lax.broadcasted_iota(jnp.int32, (CHUNK, QW), 0)
    src = lax.broadcasted_iota(jnp.int32, (CHUNK, QW), 1) % RW_HEAD
    order = (tok - src) * (1 - 2 * d)
    strict = order > 0
    incl = order >= 0
    eye = (tok == src).astype(f32)
    blk_sizes = [4 << i for i in range(CHUNK.bit_length() - 2)]
    same_blk = [(tok // bs) == (src // bs) for bs in blk_sizes]

    def spread(t):
        return jnp.where(same_head, jnp.concatenate([t] * QUAD, axis=0), jnp.zeros((), t.dtype))

    def mm(lhs, rhs):
        return jnp.dot(lhs.astype(bf), rhs.astype(bf), preferred_element_type=f32)

    for bi, q in [(bi, q) for bi in range(at_ref.shape[1]) for q in range(SCAN_QUADS)]:
        sl = slice(q * QW, (q + 1) * QW)
        a, b, k, r = at_ref[0, bi, :, sl], bt_ref[0, bi, :, sl], kt_ref[0, bi, :, sl], rt_ref[0, bi, :, sl]
        bh, kh, v = bh_ref[0, bi, :, sl], kh_ref[0, bi, :, sl], v_ref[bi, :, sl]
        ar = jnp.concatenate([a, r], axis=0)
        xb = lax.dot_general(ar, spread(b), _NT, preferred_element_type=f32)
        xk = lax.dot_general(ar, spread(k), _NT, preferred_element_type=f32)
        m_ab = jnp.where(strict, xb[:CHUNK], 0.0)
        m_ak = jnp.where(strict, xk[:CHUNK], 0.0)
        m_rb = jnp.where(incl, xb[CHUNK:], 0.0).astype(bf)
        m_rk = jnp.where(incl, xk[CHUNK:], 0.0)
        n0 = jnp.where(same_blk[0], m_ab, 0.0).astype(bf)
        x = eye - n0.astype(f32)
        x = x + mm(x, spread(mm(n0, spread(n0)).astype(bf)))
        for lvl in range(1, len(same_blk)):
            e = jnp.where(same_blk[lvl] & ~same_blk[lvl - 1], m_ab, 0.0).astype(bf)
            x = x - mm(mm(x, spread(e)), spread(x.astype(bf)))
        tb = x.astype(bf)
        w = mm(jnp.concatenate([m_ak, m_rk], axis=0), spread(v))
        uu = mm(tb, jnp.concatenate([spread(a), spread(w[:CHUNK].astype(bf))], axis=1))
        ua, u0 = uu[:, :QW].astype(bf), uu[:, QW:].astype(bf)
        rr = mm(m_rb, jnp.concatenate([spread(ua), spread(u0)], axis=1))
        ry = r.astype(f32) - rr[:, :QW]
        y0 = w[CHUNK:] - rr[:, QW:]
        gm = jnp.where(same_head, lax.dot_general(ua, bh, _TN, preferred_element_type=f32), 0.0)
        hm = jnp.where(same_head, lax.dot_general(v, kh, _TN, preferred_element_type=f32)
                       - lax.dot_general(u0, bh, _TN, preferred_element_type=f32), 0.0)
        s = s_scr[bi, q]
        sb = s.astype(bf)
        y_ref[0, bi, :, sl] = lax.dot_general(ry.astype(bf), sb, _NT, preferred_element_type=f32) + y0
        s_scr[bi, q] = s * pl_ref[0, bi, 0, :, sl] - jnp.dot(sb, gm.astype(bf), preferred_element_type=f32) + hm

    @pl.when(c == n_chunks - 1)
    def _():
        sf_ref[0] = s_scr[...]


def _wkv_scan(ops, v, pl_arr, s0):
    _, B, T, W = ops[0].shape
    nc = T // CHUNK
    hw = SCAN_QUADS * QW
    ng = W // hw
    nb = SCAN_BATCH if B % SCAN_BATCH == 0 else 1
    cidx = lambda d, c: c + d * (nc - 1 - 2 * c)
    op_spec = pl.BlockSpec((1, nb, CHUNK, hw), lambda d, b, g, c: (d, b, cidx(d, c), g))
    st_spec = pl.BlockSpec((1, nb, SCAN_QUADS, QW, QW), lambda d, b, g, c: (d, b, g, 0, 0))
    kern = functools.partial(_wkv_scan_kernel, n_chunks=nc)
    return pl.pallas_call(
        kern,
        grid=(2, B // nb, ng, nc),
        in_specs=[op_spec] * 6 + [
            pl.BlockSpec((nb, CHUNK, hw), lambda d, b, g, c: (b, cidx(d, c), g)),
            pl.BlockSpec((1, nb, 1, 1, hw), lambda d, b, g, c: (d, b, cidx(d, c), 0, g)),
            st_spec,
        ],
        out_specs=[op_spec, st_spec],
        out_shape=[
            jax.ShapeDtypeStruct((2, B, T, W), jnp.float32),
            jax.ShapeDtypeStruct(s0.shape, jnp.float32),
        ],
        scratch_shapes=[pltpu.VMEM((nb, SCAN_QUADS, QW, QW), jnp.float32)],
        compiler_params=_cparams(("parallel", "parallel", "parallel", "arbitrary")),
        name="wkv_scan",
    )(*ops, v, pl_arr, s0)


def _rwkv_constants(rw_shift, rw_w_up, rw_a_up, tt):
    bf = jnp.bfloat16
    sw, swl = rw_shift[:, :3 * RW_WIDTH], rw_shift[:, 3 * RW_WIDTH:]
    zero = jnp.zeros((RW_W_LORA, RW_WIDTH), jnp.float32)
    wl = jnp.concatenate([
        jnp.concatenate([rw_w_up[0], rw_w_up[1], zero, zero], axis=1),
        jnp.concatenate([zero, zero, rw_a_up[0], rw_a_up[1]], axis=1)], axis=0).astype(bf)
    t = jnp.arange(tt)
    same = (t[:, None] // CHUNK) == (t[None, :] // CHUNK)
    tril = (same & (t[None, :] <= t[:, None])).astype(bf)
    triu = (same & (t[None, :] >= t[:, None])).astype(bf)
    return sw, swl, wl, tril, triu


def _split_bf16(a):
    hi = a.astype(jnp.bfloat16)
    return hi, (a - hi.astype(jnp.float32)).astype(jnp.bfloat16)


def _top_rows(s, k):
    n = s.shape[0]
    rev = (n - 1 - lax.broadcasted_iota(jnp.int32, s.shape, 0)).astype(jnp.float32)
    vals, revs = [], []
    for _ in range(k):
        m = jnp.max(s, axis=0, keepdims=True)
        best = jnp.max(jnp.where(s == m, rev, -1.0), axis=0, keepdims=True)
        vals.append(m)
        revs.append(best)
        s = jnp.where(rev == best, -jnp.inf, s)
    return jnp.concatenate(vals, axis=0), n - 1 - jnp.concatenate(revs, axis=0).astype(jnp.int32)


def _take_rows(table, idx):
    out = jnp.zeros(idx.shape, table.dtype)
    for a in range(table.shape[0]):
        out = jnp.where(idx == a, table[a:a + 1], out)
    return out


def _peer_topk_kernel(q_ref, keys_ref, eidx_ref, gate_ref):
    nt_dims = (((1,), (1,)), ((), ()))
    sv, si = [], []
    for p in range(2):
        q_hi, q_lo = _split_bf16(q_ref[0, :, p * N_KEYS:(p + 1) * N_KEYS])
        k_hi, k_lo = _split_bf16(keys_ref[0, p])
        s = (lax.dot_general(k_hi, q_hi, nt_dims, preferred_element_type=jnp.float32)
             + lax.dot_general(k_hi, q_lo, nt_dims, preferred_element_type=jnp.float32)
             + lax.dot_general(k_lo, q_hi, nt_dims, preferred_element_type=jnp.float32))
        v, i = _top_rows(s, PEER_TOPK)
        sv.append(v)
        si.append(i)
    pairs = [(a, b) for a in range(PEER_TOPK) for b in range(PEER_TOPK) if (a + 1) * (b + 1) <= PEER_TOPK]
    pad = -len(pairs) % SUBLANES
    neg = jnp.full((pad, sv[0].shape[1]), -jnp.inf, jnp.float32)
    cand = jnp.concatenate([sv[0][a:a + 1] + sv[1][b:b + 1] for a, b in pairs] + [neg], axis=0)
    cidx = jnp.concatenate([si[0][a:a + 1] * N_KEYS + si[1][b:b + 1] for a, b in pairs], axis=0)
    top_s, pos = _top_rows(cand, PEER_TOPK)
    eidx_ref[...] = _take_rows(cidx, pos)
    p = jnp.exp(top_s - top_s[0:1])
    gate_ref[...] = p / jnp.sum(p, axis=0, keepdims=True)


def _peer_topk(qp, keys, tt):
    B, S, _ = qp.shape
    nt = S // tt
    rows = PEER_HEADS * PEER_TOPK
    out_map = lambda b, i, h: (h, b * nt + i)
    return pl.pallas_call(
        _peer_topk_kernel,
        grid=(B, nt, PEER_HEADS),
        in_specs=[
            pl.BlockSpec((1, tt, PEER_QDIM), lambda b, i, h: (b, i, h)),
            pl.BlockSpec((1, 2, N_KEYS, PEER_QDIM // 2), lambda b, i, h: (h, 0, 0, 0)),
        ],
        out_specs=[pl.BlockSpec((PEER_TOPK, tt), out_map), pl.BlockSpec((PEER_TOPK, tt), out_map)],
        out_shape=[jax.ShapeDtypeStruct((rows, B * S), jnp.int32), jax.ShapeDtypeStruct((rows, B * S), jnp.float32)],
        compiler_params=_cparams(("parallel", "parallel", "arbitrary")),
        name="peer_topk",
    )(qp, keys)


PEER_PICKS = PEER_HEADS * PEER_TOPK
GATHER_SLOTS = 4


def _gelu_tanh(a):
    return 0.5 * a * (1.0 + jnp.tanh(math.sqrt(2.0 / math.pi) * (a + 0.044715 * (a * a * a))))


def _peer_gather_kernel(eidx_hbm, h2_ref, gate_ref, x1_ref, g2_ref, tab_hbm, out_ref, idx_smem, buf, idx_sem, row_sem,
                        *, tt):
    n_seg = tab_hbm.shape[1]
    tile = pl.program_id(0) * pl.num_programs(1) + pl.program_id(1)
    idx_copy = pltpu.make_async_copy(eidx_hbm.at[tile], idx_smem, idx_sem)
    idx_copy.start()
    idx_copy.wait()

    def issue(t, slot):
        for k in range(PEER_PICKS):
            e = idx_smem[t * PEER_PICKS + k]
            pltpu.make_async_copy(tab_hbm.at[e], buf.at[slot, pl.ds(k * n_seg, n_seg), :],
                                  row_sem.at[slot]).start(priority=k % 2)

    def wait_rows(slot):
        pltpu.make_async_copy(buf.at[slot], buf.at[slot], row_sem.at[slot]).wait()

    lane = lax.broadcasted_iota(jnp.int32, (PEER_PICKS, tt), 1)

    def words(slot, s):
        return buf[slot, pl.ds(s, PEER_PICKS, stride=n_seg), :]

    def compute(t, slot):
        xrow = h2_ref[0, pl.ds(t, 1), :]
        prod = None
        for s in range(n_seg):
            u = pltpu.bitcast(words(slot, s) & jnp.uint32(0xFFFF0000), jnp.float32)
            term = u * xrow[:, s * LANE:(s + 1) * LANE]
            prod = term if prod is None else prod + term
        act = jnp.sum(prod, axis=1, keepdims=True)
        gate = jnp.sum(jnp.where(lane == t, gate_ref[...], 0.0), axis=1, keepdims=True)
        coef = gate * _gelu_tanh(act)
        ys = [jnp.sum(coef * pltpu.bitcast(words(slot, s) << 16, jnp.float32), axis=0, keepdims=True)
              for s in range(n_seg)]
        out_ref[0, pl.ds(t, 1), :] = x1_ref[0, pl.ds(t, 1), :] + g2_ref[0] * jnp.concatenate(ys, axis=1)

    for j in range(GATHER_SLOTS):
        issue(j, j)

    def step(t, j, prefetch):
        wait_rows(j)
        compute(t, j)
        if prefetch:
            issue(t + GATHER_SLOTS, j)

    def group(g, carry):
        for j in range(GATHER_SLOTS):
            step(g * GATHER_SLOTS + j, j, True)
        return carry

    n_groups = tt // GATHER_SLOTS
    lax.fori_loop(0, n_groups - 1, group, 0)
    for j in range(GATHER_SLOTS):
        step((n_groups - 1) * GATHER_SLOTS + j, j, False)


def _peer_gather(eidx_t, h2, gate_t, x1, g2, tab, tt):
    B, S, D = x1.shape
    nt = S // tt
    tok = lambda b, i: (b, i, 0)
    kern = functools.partial(_peer_gather_kernel, tt=tt)
    return pl.pallas_call(
        kern,
        grid=(B, nt),
        in_specs=[
            pl.BlockSpec(memory_space=pl.ANY),
            pl.BlockSpec((1, tt, D), tok),
            pl.BlockSpec((PEER_PICKS, tt), lambda b, i: (0, b * nt + i)),
            pl.BlockSpec((1, tt, D), tok),
            pl.BlockSpec((1, 1, D), lambda b, i: (b, 0, 0)),
            pl.BlockSpec(memory_space=pl.ANY),
        ],
        out_specs=pl.BlockSpec((1, tt, D), tok),
        out_shape=jax.ShapeDtypeStruct((B, S, D), jnp.float32),
        scratch_shapes=[
            pltpu.SMEM((tt * PEER_PICKS,), jnp.int32),
            pltpu.VMEM((GATHER_SLOTS, PEER_PICKS * (D // LANE), LANE), jnp.uint32),
            pltpu.SemaphoreType.DMA(()),
            pltpu.SemaphoreType.DMA((GATHER_SLOTS,)),
        ],
        compiler_params=_cparams(("arbitrary", "arbitrary")),
        name="peer_gather",
    )(eidx_t, h2, gate_t, x1, g2, tab)


def _pack_expert_table(peer_u, peer_v):
    u16 = lax.bitcast_convert_type(peer_u.astype(jnp.bfloat16), jnp.uint16).astype(jnp.uint32)
    v16 = lax.bitcast_convert_type(peer_v.astype(jnp.bfloat16), jnp.uint16).astype(jnp.uint32)
    return ((u16 << 16) | v16).reshape(peer_u.shape[0], peer_u.shape[1] // LANE, LANE)


def _rope_tables(n_tokens):
    rows = n_tokens // GRID_W
    row = jnp.repeat(jnp.arange(rows, dtype=jnp.float32), GRID_W)
    col = jnp.tile(jnp.arange(GRID_W, dtype=jnp.float32), rows)
    inv = ROPE_THETA ** (-jnp.arange(ROPE_FREQS, dtype=jnp.float32) / ROPE_FREQS)
    ang = jnp.stack([row[:, None] * inv, col[:, None] * inv], axis=1)
    cos = jnp.cos(ang)[:, None, :, None, :]
    sin = jnp.sin(ang)[:, None, :, None, :]
    cos = jnp.broadcast_to(cos, (n_tokens, 2, 2, 2, ROPE_FREQS)).reshape(n_tokens, LANE)
    sgn = jnp.array([-1.0, 1.0], jnp.float32)[None, None, None, :, None]
    sin = jnp.broadcast_to(sin * sgn, (n_tokens, 2, 2, 2, ROPE_FREQS)).reshape(n_tokens, LANE)
    return cos, sin


def kernel(x, c, ctx, c_ctx, w_mod, b_mod, norm1_g, w_in, q_norm_g, k_norm_g, diff_lambda, diff_out_g, rw_shift,
           rw_w0, rw_w_up, rw_a0, rw_a_up, rw_g_up, rw_k_k, rw_k_a, rw_r_k, rw_ln_g, rw_ln_b, w_branch_a,
           w_branch_b, w_out, norm2_g, peer_wq, peer_keys, peer_u, peer_v):
    assert w_mod.shape[0] == 1, "single-layer trunk only"
    B, S, D = x.shape
    C = ctx.shape[1]
    bf = jnp.bfloat16
    lam_init = 0.8 - 0.6 * math.exp(-0.3 * 0)

    tiles = _tiles(S)

    c_rows = jnp.concatenate([c, c_ctx[None, :]], axis=0)
    c_rows = jnp.pad(c_rows, ((0, -c_rows.shape[0] % SUBLANES), (0, 0)))
    mod = _adaln(c_rows, w_mod[0], b_mod[0][None, :], tiles.proj_cols)
    mod_x = mod[:B].reshape(B, N_MOD, 1, D)
    mod_c = mod[B:B + 1].reshape(1, N_MOD, 1, D)
    sc1x, sh1x = norm1_g[0] * (1 + mod_x[:, 1]), mod_x[:, 0]
    sc1c, sh1c = norm1_g[0] * (1 + mod_c[:, 1]), mod_c[:, 0]
    sc2x, sh2x = norm2_g[0] * (1 + mod_x[:, 4]), mod_x[:, 3]
    g1x, g2x = mod_x[:, 2], mod_x[:, 5]

    w_in_p = jnp.concatenate([w_in[0][:, :6144], w_in[0][:, 6400:], w_in[0][:, 6144:6400]], axis=1).astype(bf)
    zx = _norm_matmul(x, sc1x, sh1x, w_in_p, tiles.proj_rows, tiles.proj_cols)
    zc = _norm_matmul(ctx, sc1c, sh1c, w_in_p, min(tiles.proj_rows, C), tiles.proj_cols)

    lam = (jnp.exp(jnp.sum(diff_lambda[0, 0] * diff_lambda[0, 1])) - jnp.exp(jnp.sum(diff_lambda[0, 2] * diff_lambda[0, 3]))
           + lam_init).reshape(1)
    cos, sin_signed = _rope_tables(S)
    qg = jnp.tile(q_norm_g[0], 2).reshape(1, LANE)
    kg = jnp.tile(k_norm_g[0], 2).reshape(1, LANE)
    og = diff_out_g[0].reshape(1, LANE)
    grp = jnp.arange(LANE) // DA_QK
    avg64 = jnp.where(grp[:, None] == grp[None, :], 1.0 / DA_QK, 0.0).astype(bf)
    o_a = _diff_attention(lam, zx, zc, cos, sin_signed, qg, kg, og, avg64, tiles.attn_q, 1.0 - lam_init)

    hd = jnp.arange(RW_WIDTH) // RW_HEAD
    ones_bd = (hd[:, None] == jnp.arange(LANE)[None, :]).astype(bf)
    row1 = lambda t: t.reshape(1, RW_WIDTH)
    prep = {}
    for name, z, T in (("ctx", zc, C), ("x", zx, S)):
        tt = min(tiles.rwkv, T)
        sw, swl, wl, tril, triu = _rwkv_constants(rw_shift[0], rw_w_up[0], rw_a_up[0], tt)
        prep[name] = _rwkv_prepare(z, sw, swl, rw_w0[0], rw_a0[0], row1(rw_k_k[0]), row1(rw_k_a[0]), row1(rw_r_k[0]),
                                   wl, rw_g_up[0].astype(bf), ones_bd, tril, triu, tt)
    s0 = jnp.zeros((2, B, RW_WIDTH // QW, QW, QW), jnp.float32)
    pc, px = prep["ctx"], prep["x"]
    _, s_ctx = _wkv_scan(pc[0:6], pc[6], pc[7], s0)
    y_rw, _ = _wkv_scan(px[0:6], px[6], px[7], s_ctx)

    x1, h2, qp = _merge(x, o_a, y_rw, px[8], px[9], row1(rw_ln_g[0]), row1(rw_ln_b[0]), ones_bd, zx, g1x,
                        w_branch_a[0].astype(bf), w_branch_b[0].astype(bf), w_out[0].astype(bf),
                        sc2x, sh2x, peer_wq[0].astype(bf), tiles.merge, jnp.float32)

    eidx_t, gate_t = _peer_topk(qp, peer_keys[0], tiles.topk)
    eidx = eidx_t.T.reshape(B * S // tiles.gather, tiles.gather * PEER_PICKS)
    tab = _pack_expert_table(peer_u[0], peer_v[0])
    return _peer_gather(eidx, h2, gate_t, x1, g2x, tab, tiles.gather)
```

```python
import functools
import math
from typing import NamedTuple

import jax
import jax.numpy as jnp
from jax import lax
from jax.experimental import pallas as pl
from jax.experimental.pallas import tpu as pltpu

N_MOD = 6
EPS = 1e-6
GRID_W = 64
DA_HEADS = 8
DA_QK = 64
DA_V = 2 * DA_QK
ROPE_THETA = 10000.0
ROPE_FREQS = DA_QK // 4
RW_HEADS = 16
RW_HEAD = 64
RW_WIDTH = RW_HEADS * RW_HEAD
RW_W_LORA = 64
RW_LN_EPS = 64e-5
L2_EPS = 1e-12
PEER_HEADS = 8
N_KEYS = 128
PEER_TOPK = 16
PEER_QDIM = 256

COL_Q = 0
COL_K = 1024
COL_V = 2048
COL_RW = 3072
COL_GATE = 6144
COL_LORA = 8192

LANE = 128
SUBLANES = 8
VMEM_LIMIT = 56 * 1024 * 1024


class Tiles(NamedTuple):
    proj_rows: int
    proj_cols: int
    attn_q: int
    rwkv: int
    merge: int
    topk: int
    gather: int


def _tiles(seq_len):
    return Tiles(proj_rows=min(1024, seq_len), proj_cols=768, attn_q=min(256, seq_len), rwkv=min(256, seq_len),
                 merge=min(256, seq_len), topk=min(512, seq_len), gather=min(512, seq_len))


def _cparams(sem):
    return pltpu.CompilerParams(dimension_semantics=sem, vmem_limit_bytes=VMEM_LIMIT)


def _adaln_kernel(c_ref, w_ref, b_ref, o_ref):
    a_hi, a_lo = _split_bf16(jax.nn.silu(c_ref[...]))
    w_hi, w_lo = _split_bf16(w_ref[...])
    dot = functools.partial(jnp.dot, preferred_element_type=jnp.float32)
    o_ref[...] = dot(a_hi, w_hi) + dot(a_hi, w_lo) + dot(a_lo, w_hi) + b_ref[...]


def _adaln(c_rows, w_mod, b_mod, tn):
    R, D = c_rows.shape
    N = w_mod.shape[1]
    return pl.pallas_call(
        _adaln_kernel,
        grid=(N // tn,),
        in_specs=[pl.BlockSpec((R, D), lambda j: (0, 0)), pl.BlockSpec((D, tn), lambda j: (0, j)),
                  pl.BlockSpec((1, tn), lambda j: (0, j))],
        out_specs=pl.BlockSpec((R, tn), lambda j: (0, j)),
        out_shape=jax.ShapeDtypeStruct((R, N), jnp.float32),
        compiler_params=_cparams(("parallel",)),
        name="adaln",
    )(c_rows, w_mod, b_mod)


def _norm_matmul_kernel(x_ref, sc_ref, sh_ref, w_ref, z_ref, hn_ref):
    @pl.when(pl.program_id(2) == 0)
    def _():
        x = x_ref[0]
        ms = jnp.mean(x * x, axis=-1, keepdims=True)
        h = x * lax.rsqrt(ms + EPS) * sc_ref[0] + sh_ref[0]
        hn_ref[...] = h.astype(jnp.bfloat16)

    z_ref[0] = jnp.dot(hn_ref[...], w_ref[...], preferred_element_type=jnp.float32)


def _norm_matmul(x, sc, sh, w, tm, tn):
    B, T, D = x.shape
    N = w.shape[1]
    bm = (lambda b, i, j: (b, 0, 0)) if sc.shape[0] == B else (lambda b, i, j: (0, 0, 0))
    return pl.pallas_call(
        _norm_matmul_kernel,
        grid=(B, T // tm, N // tn),
        in_specs=[
            pl.BlockSpec((1, tm, D), lambda b, i, j: (b, i, 0)),
            pl.BlockSpec((1, 1, D), bm),
            pl.BlockSpec((1, 1, D), bm),
            pl.BlockSpec((D, tn), lambda b, i, j: (0, j)),
        ],
        out_specs=pl.BlockSpec((1, tm, tn), lambda b, i, j: (b, i, j)),
        out_shape=jax.ShapeDtypeStruct((B, T, N), jnp.float32),
        scratch_shapes=[pltpu.VMEM((tm, D), jnp.bfloat16)],
        compiler_params=_cparams(("parallel", "parallel", "arbitrary")),
        name="norm_matmul",
    )(x, sc, sh, w)


def _group_mean_sq(x, avg):
    sq = x * x
    hi = sq.astype(jnp.bfloat16)
    lo = (sq - hi.astype(jnp.float32)).astype(jnp.bfloat16)
    return (jnp.dot(hi, avg, preferred_element_type=jnp.float32)
            + jnp.dot(lo, avg, preferred_element_type=jnp.float32))


def _head_sum(x, head_sel):
    f32 = jnp.float32
    hi, lo = _split_bf16(x)
    sums = jnp.dot(hi, head_sel, preferred_element_type=f32) + jnp.dot(lo, head_sel, preferred_element_type=f32)
    s_hi, s_lo = _split_bf16(sums)
    nt = (((1,), (1,)), ((), ()))
    return (lax.dot_general(s_hi, head_sel, nt, preferred_element_type=f32)
            + lax.dot_general(s_lo, head_sel, nt, preferred_element_type=f32))


def _rope(x, cos, sin_signed, first_half):
    partner = jnp.where(first_half, pltpu.roll(x, LANE - ROPE_FREQS, axis=1), pltpu.roll(x, ROPE_FREQS, axis=1))
    return x * cos + partner * sin_signed


def _diff_attn_kernel(lam_ref, q_ref, kx_ref, vx_ref, kc_ref, vc_ref, cosq_ref, sinq_ref, cosk_ref, sink_ref,
                      qg_ref, kg_ref, og_ref, avg64_ref, o_ref, k_s, v_s, *, n_ctx, out_scale):
    lane = lax.broadcasted_iota(jnp.int32, (1, LANE), 1)
    first_half = (lane % (2 * ROPE_FREQS)) < ROPE_FREQS
    avg64 = avg64_ref[...]

    @pl.when(pl.program_id(2) == 0)
    def _():
        kc = kc_ref[0]
        kc = kc * lax.rsqrt(_group_mean_sq(kc, avg64) + EPS) * kg_ref[...]
        k_s[0:n_ctx, :] = kc.astype(jnp.bfloat16)
        kx = kx_ref[0]
        kx = kx * lax.rsqrt(_group_mean_sq(kx, avg64) + EPS) * kg_ref[...]
        kx = _rope(kx, cosk_ref[...], sink_ref[...], first_half)
        k_s[n_ctx:, :] = kx.astype(jnp.bfloat16)
        v_s[0:n_ctx, :] = vc_ref[0].astype(jnp.bfloat16)
        v_s[n_ctx:, :] = vx_ref[0].astype(jnp.bfloat16)

    q = q_ref[0]
    q = q * lax.rsqrt(_group_mean_sq(q, avg64) + EPS) * qg_ref[...]
    q = _rope(q, cosq_ref[...], sinq_ref[...], first_half) * (DA_QK ** -0.5 * math.log2(math.e))
    k = k_s[...]
    ps = []
    for m in range(2):
        in_map = (lane // DA_QK) == m
        qm = jnp.where(in_map, q, 0.0).astype(jnp.bfloat16)
        s = lax.dot_general(qm, k, (((1,), (1,)), ((), ())), preferred_element_type=jnp.float32)
        p = jnp.exp2(s - jnp.max(s, axis=-1, keepdims=True))
        scale = (1.0 if m == 0 else lam_ref[0]) / jnp.sum(p, axis=-1, keepdims=True)
        ps.append(p * scale)
    pd = (ps[0] - ps[1]).astype(jnp.bfloat16)
    o = jnp.dot(pd, v_s[...], preferred_element_type=jnp.float32)
    o = o * lax.rsqrt(jnp.mean(o * o, axis=-1, keepdims=True) + EPS) * og_ref[...] * out_scale
    o_ref[0] = o.astype(o_ref.dtype)


def _diff_attention(lam, zx, zc, cos, sin_signed, qg, kg, og, avg64, tq, out_scale):
    B, S, _ = zx.shape
    C = zc.shape[1]
    qb, kb, vb = COL_Q // LANE, COL_K // LANE, COL_V // LANE
    kern = functools.partial(_diff_attn_kernel, n_ctx=C, out_scale=out_scale)
    const = lambda b, h, i: (0, 0)
    return pl.pallas_call(
        kern,
        grid=(B, DA_HEADS, S // tq),
        in_specs=[
            pl.BlockSpec(memory_space=pltpu.SMEM),
            pl.BlockSpec((1, tq, LANE), lambda b, h, i: (b, i, qb + h)),
            pl.BlockSpec((1, S, LANE), lambda b, h, i: (b, 0, kb + h)),
            pl.BlockSpec((1, S, LANE), lambda b, h, i: (b, 0, vb + h)),
            pl.BlockSpec((1, C, LANE), lambda b, h, i: (b, 0, kb + h)),
            pl.BlockSpec((1, C, LANE), lambda b, h, i: (b, 0, vb + h)),
            pl.BlockSpec((tq, LANE), lambda b, h, i: (i, 0)),
            pl.BlockSpec((tq, LANE), lambda b, h, i: (i, 0)),
            pl.BlockSpec((S, LANE), const),
            pl.BlockSpec((S, LANE), const),
            pl.BlockSpec((1, LANE), const),
            pl.BlockSpec((1, LANE), const),
            pl.BlockSpec((1, LANE), const),
            pl.BlockSpec((LANE, LANE), const),
        ],
        out_specs=pl.BlockSpec((1, tq, LANE), lambda b, h, i: (b, i, h)),
        out_shape=jax.ShapeDtypeStruct((B, S, DA_HEADS * DA_V), jnp.bfloat16),
        scratch_shapes=[pltpu.VMEM((C + S, LANE), jnp.bfloat16), pltpu.VMEM((C + S, LANE), jnp.bfloat16)],
        compiler_params=_cparams(("parallel", "parallel", "arbitrary")),
        name="diff_attention",
    )(lam, zx, zx, zx, zc, zc, cos, sin_signed, cos, sin_signed, qg, kg, og, avg64)


def _merge_kernel(x_ref, oa_ref, y0_ref, y1_ref, bonus_ref, g_ref, lng_ref, lnb_ref, ones_ref, ga_ref, gb_ref, g1_ref,
                  wa_ref, wb_ref, wo_ref, sc2_ref, sh2_ref, wq_ref, x1_ref, h2_ref, qp_ref):
    ones_bd = ones_ref[...]
    yy = y0_ref[0, 0] + y1_ref[0, 0]
    dlt = yy - _head_sum(yy, ones_bd) * (1.0 / RW_HEAD)
    var = _head_sum(dlt * dlt, ones_bd) * (1.0 / RW_HEAD)
    ob = (dlt * lax.rsqrt(var + RW_LN_EPS) * lng_ref[...] + lnb_ref[...] + bonus_ref[0]) * g_ref[0]
    ta = jnp.dot(oa_ref[0], wa_ref[...], preferred_element_type=jnp.float32)
    tb = jnp.dot(ob.astype(jnp.bfloat16), wb_ref[...], preferred_element_type=jnp.float32)
    mix = jax.nn.sigmoid(ga_ref[0]) * ta + jax.nn.sigmoid(gb_ref[0]) * tb
    y = jnp.dot(mix.astype(jnp.bfloat16), wo_ref[...], preferred_element_type=jnp.float32)
    x1 = x_ref[0] + g1_ref[0] * y
    x1_ref[0] = x1
    ms = jnp.mean(x1 * x1, axis=-1, keepdims=True)
    h2 = x1 * lax.rsqrt(ms + EPS) * sc2_ref[0] + sh2_ref[0]
    h2_ref[0] = h2.astype(h2_ref.dtype)
    qp_ref[0] = jnp.dot(h2.astype(jnp.bfloat16), wq_ref[...], preferred_element_type=jnp.float32)


def _merge(x, oa, y, bonus, g, ln_g, ln_b, ones_bd, zx, g1, wa, wb, wo, sc2, sh2, wq, tm, h2_dtype):
    B, S, D = x.shape
    NQ = wq.shape[1]
    tok = lambda b, i: (b, i, 0)
    per_b = lambda b, i: (b, 0, 0)
    const = lambda b, i: (0, 0)
    ga_blk, gb_blk = COL_GATE // D, COL_GATE // D + 1
    return pl.pallas_call(
        _merge_kernel,
        grid=(B, S // tm),
        in_specs=[
            pl.BlockSpec((1, tm, D), tok),
            pl.BlockSpec((1, tm, D), tok),
            pl.BlockSpec((1, 1, tm, D), lambda b, i: (0, b, i, 0)),
            pl.BlockSpec((1, 1, tm, D), lambda b, i: (1, b, i, 0)),
            pl.BlockSpec((1, tm, D), tok),
            pl.BlockSpec((1, tm, D), tok),
            pl.BlockSpec((1, D), const),
            pl.BlockSpec((1, D), const),
            pl.BlockSpec((D, LANE), const),
            pl.BlockSpec((1, tm, D), lambda b, i: (b, i, ga_blk)),
            pl.BlockSpec((1, tm, D), lambda b, i: (b, i, gb_blk)),
            pl.BlockSpec((1, 1, D), per_b),
            pl.BlockSpec((D, D), const),
            pl.BlockSpec((D, D), const),
            pl.BlockSpec((D, D), const),
            pl.BlockSpec((1, 1, D), per_b),
            pl.BlockSpec((1, 1, D), per_b),
            pl.BlockSpec((D, NQ), const),
        ],
        out_specs=[
            pl.BlockSpec((1, tm, D), tok),
            pl.BlockSpec((1, tm, D), tok),
            pl.BlockSpec((1, tm, NQ), tok),
        ],
        out_shape=[
            jax.ShapeDtypeStruct((B, S, D), jnp.float32),
            jax.ShapeDtypeStruct((B, S, D), h2_dtype),
            jax.ShapeDtypeStruct((B, S, NQ), jnp.float32),
        ],
        compiler_params=_cparams(("parallel", "parallel")),
        name="merge_peerq",
    )(x, oa, y, y, bonus, g, ln_g, ln_b, ones_bd, zx, zx, g1, wa, wb, wo, sc2, sh2, wq)


CHUNK = 64
QUAD = 4
QW = QUAD * RW_HEAD
SCAN_QUADS = 4
SCAN_BATCH = 2


def _split3(a):
    hi = a.astype(jnp.bfloat16)
    r1 = a - hi.astype(jnp.float32)
    mid = r1.astype(jnp.bfloat16)
    lo = (r1 - mid.astype(jnp.float32)).astype(jnp.bfloat16)
    return hi, mid, lo


def _dot3(m, parts):
    acc = jnp.dot(m, parts[0], preferred_element_type=jnp.float32)
    acc = acc + jnp.dot(m, parts[1], preferred_element_type=jnp.float32)
    return acc + jnp.dot(m, parts[2], preferred_element_type=jnp.float32)


def _shift3(z, prev_row, next_row, w):
    n = z.shape[0]
    row = lax.broadcasted_iota(jnp.int32, z.shape, 0)
    zm = jnp.where(row == 0, prev_row, pltpu.roll(z, 1, axis=0))
    zp = jnp.where(row == n - 1, next_row, pltpu.roll(z, n - 1, axis=0))
    return w[0:1] * zm + w[1:2] * z + w[2:3] * zp


def _rwkv_prep_kernel(z_ref, zp_ref, zn_ref, l_ref, lp_ref, ln_ref, sw_ref, swl_ref, w0_ref, a0_ref, kk_ref, ka_ref,
                      rk_ref, wl_ref, gup_ref, ones_ref, tril_ref, triu_ref,
                      at_ref, bt_ref, kt_ref, rt_ref, bh_ref, kh_ref, v_ref, pl_ref, bonus_ref, g_ref, *, n_tiles):
    i = pl.program_id(1)
    inner_lo = 1.0 - (i == 0).astype(jnp.float32)
    inner_hi = 1.0 - (i == n_tiles - 1).astype(jnp.float32)
    zs = _shift3(z_ref[0], zp_ref[0, 7:8, :] * inner_lo, zn_ref[0, 0:1, :] * inner_hi, sw_ref[...])
    ls = _shift3(l_ref[0], lp_ref[0, 7:8, :] * inner_lo, ln_ref[0, 0:1, :] * inner_hi, swl_ref[...])
    r, k, v = zs[:, 0:RW_WIDTH], zs[:, RW_WIDTH:2 * RW_WIDTH], zs[:, 2 * RW_WIDTH:3 * RW_WIDTH]
    lane = lax.broadcasted_iota(jnp.int32, (1, LANE), 1)
    wa_in = jnp.where(lane < RW_W_LORA, jnp.tanh(ls[:, 0:LANE]), ls[:, 0:LANE]).astype(jnp.bfloat16)
    lora = jnp.dot(wa_in, wl_ref[...], preferred_element_type=jnp.float32)
    ones_bd = ones_ref[...]
    kn = k * kk_ref[...]
    kk = kn * lax.rsqrt(_head_sum(kn * kn, ones_bd) + L2_EPS)
    bonus_ref[0] = _head_sum(r * k * rk_ref[...], ones_bd) * v
    g_ref[0] = jnp.dot(jax.nn.sigmoid(ls[:, LANE:2 * LANE]).astype(jnp.bfloat16), gup_ref[...],
                       preferred_element_type=jnp.float32)
    v_ref[0] = v.astype(v_ref.dtype)
    tris = (tril_ref[...], triu_ref[...])
    for d in range(2):
        lw = lora[:, d * RW_WIDTH:(d + 1) * RW_WIDTH] + w0_ref[d:d + 1, :]
        logw = -jnp.exp(-jax.nn.softplus(-lw) - 0.5)
        a = jax.nn.sigmoid(a0_ref[d:d + 1, :] + lora[:, (2 + d) * RW_WIDTH:(3 + d) * RW_WIDTH])
        k_eff = k * (1.0 + (a - 1.0) * ka_ref[...])
        b = kk * a
        parts = _split3(logw)
        incl = _dot3(tris[d], parts)
        rest = _dot3(tris[1 - d], parts) - logw
        at_ref[d, 0] = (kk * jnp.exp(incl - logw)).astype(at_ref.dtype)
        rt_ref[d, 0] = (r * jnp.exp(incl)).astype(rt_ref.dtype)
        inv = jnp.exp(-incl)
        bt_ref[d, 0] = (b * inv).astype(bt_ref.dtype)
        kt_ref[d, 0] = (k_eff * inv).astype(kt_ref.dtype)
        to_end = jnp.exp(rest)
        bh_ref[d, 0] = (b * to_end).astype(bh_ref.dtype)
        kh_ref[d, 0] = (k_eff * to_end).astype(kh_ref.dtype)
        tot = incl + rest
        for c in range(tot.shape[0] // CHUNK):
            pl_ref[d, 0, c] = jnp.exp(tot[c * CHUNK:c * CHUNK + 1])


def _rwkv_prepare(z, sw, swl, w0, a0, k_k, k_a, r_k, wl, g_up, ones_bd, tril, triu, tt):
    B, T, _ = z.shape
    nt = T // tt
    rb, lb = COL_RW // (3 * RW_WIDTH), COL_LORA // (2 * LANE)
    tb = tt // 8
    prev = lambda i: jnp.maximum(i * tb - 1, 0)
    nxt = lambda i: jnp.minimum((i + 1) * tb, T // 8 - 1)
    const2 = lambda b, i: (0, 0)
    tok = lambda b, i: (b, i, 0)
    dtok = lambda b, i: (0, b, i, 0)
    bf = jnp.bfloat16
    kern = functools.partial(_rwkv_prep_kernel, n_tiles=nt)
    return pl.pallas_call(
        kern,
        grid=(B, nt),
        in_specs=[
            pl.BlockSpec((1, tt, 3 * RW_WIDTH), lambda b, i: (b, i, rb)),
            pl.BlockSpec((1, 8, 3 * RW_WIDTH), lambda b, i: (b, prev(i), rb)),
            pl.BlockSpec((1, 8, 3 * RW_WIDTH), lambda b, i: (b, nxt(i), rb)),
            pl.BlockSpec((1, tt, 2 * LANE), lambda b, i: (b, i, lb)),
            pl.BlockSpec((1, 8, 2 * LANE), lambda b, i: (b, prev(i), lb)),
            pl.BlockSpec((1, 8, 2 * LANE), lambda b, i: (b, nxt(i), lb)),
            pl.BlockSpec((3, 3 * RW_WIDTH), const2),
            pl.BlockSpec((3, 2 * LANE), const2),
            pl.BlockSpec((2, RW_WIDTH), const2),
            pl.BlockSpec((2, RW_WIDTH), const2),
            pl.BlockSpec((1, RW_WIDTH), const2),
            pl.BlockSpec((1, RW_WIDTH), const2),
            pl.BlockSpec((1, RW_WIDTH), const2),
            pl.BlockSpec((LANE, 4 * RW_WIDTH), const2),
            pl.BlockSpec((LANE, RW_WIDTH), const2),
            pl.BlockSpec((RW_WIDTH, LANE), const2),
            pl.BlockSpec((tt, tt), const2),
            pl.BlockSpec((tt, tt), const2),
        ],
        out_specs=[pl.BlockSpec((2, 1, tt, RW_WIDTH), dtok)] * 6 + [
            pl.BlockSpec((1, tt, RW_WIDTH), tok),
            pl.BlockSpec((2, 1, tt // CHUNK, 1, RW_WIDTH), lambda b, i: (0, b, i, 0, 0)),
            pl.BlockSpec((1, tt, RW_WIDTH), tok),
            pl.BlockSpec((1, tt, RW_WIDTH), tok),
        ],
        out_shape=[jax.ShapeDtypeStruct((2, B, T, RW_WIDTH), bf)] * 6 + [
            jax.ShapeDtypeStruct((B, T, RW_WIDTH), bf),
            jax.ShapeDtypeStruct((2, B, T // CHUNK, 1, RW_WIDTH), jnp.float32),
            jax.ShapeDtypeStruct((B, T, RW_WIDTH), jnp.float32),
            jax.ShapeDtypeStruct((B, T, RW_WIDTH), jnp.float32),
        ],
        compiler_params=_cparams(("parallel", "parallel")),
        name="rwkv_prepare",
    )(z, z, z, z, z, z, sw, swl, w0, a0, k_k, k_a, r_k, wl, g_up, ones_bd, tril, triu)


_NT = (((1,), (1,)), ((), ()))
_TN = (((0,), (0,)), ((), ()))


def _wkv_scan_kernel(at_ref, bt_ref, kt_ref, rt_ref, bh_ref, kh_ref, v_ref, pl_ref, s0_ref, y_ref, sf_ref, s_scr,
                     *, n_chunks):
    d = pl.program_id(0)
    c = pl.program_id(3)
    f32, bf = jnp.float32, jnp.bfloat16

    @pl.when(c == 0)
    def _():
        s_scr[...] = s0_ref[0]

    row = lax.broadcasted_iota(jnp.int32, (QW, QW), 0)
    col = lax.broadcasted_iota(jnp.int32, (QW, QW), 1)
    same_head = (row // RW_HEAD) == (col // RW_HEAD)
    tok = lax.broadcasted_iota(jnp.int32, (CHUNK, QW), 0)
    src = lax.broadcasted_iota(jnp.int32, (CHUNK, QW), 1) % RW_HEAD
    order = (tok - src) * (1 - 2 * d)
    strict = order > 0
    incl = order >= 0
    eye = (tok == src).astype(f32)
    blk_sizes = [4 << i for i in range(CHUNK.bit_length() - 2)]
    same_blk = [(tok // bs) == (src // bs) for bs in blk_sizes]

    def spread(t):
        return jnp.where(same_head, jnp.concatenate([t] * QUAD, axis=0), jnp.zeros((), t.dtype))

    def mm(lhs, rhs):
        return jnp.dot(lhs.astype(bf), rhs.astype(bf), preferred_element_type=f32)

    for bi, q in [(bi, q) for bi in range(at_ref.shape[1]) for q in range(SCAN_QUADS)]:
        sl = slice(q * QW, (q + 1) * QW)
        a, b, k, r = at_ref[0, bi, :, sl], bt_ref[0, bi, :, sl], kt_ref[0, bi, :, sl], rt_ref[0, bi, :, sl]
        bh, kh, v = bh_ref[0, bi, :, sl], kh_ref[0, bi, :, sl], v_ref[bi, :, sl]
        ar = jnp.concatenate([a, r], axis=0)
        xb = lax.dot_general(ar, spread(b), _NT, preferred_element_type=f32)
        xk = lax.dot_general(ar, spread(k), _NT, preferred_element_type=f32)
        m_ab = jnp.where(strict, xb[:CHUNK], 0.0)
        m_ak = jnp.where(strict, xk[:CHUNK], 0.0)
        m_rb = jnp.where(incl, xb[CHUNK:], 0.0).astype(bf)
        m_rk = jnp.where(incl, xk[CHUNK:], 0.0)
        n0 = jnp.where(same_blk[0], m_ab, 0.0).astype(bf)
        x = eye - n0.astype(f32)
        x = x + mm(x, spread(mm(n0, spread(n0)).astype(bf)))
        for lvl in range(1, len(same_blk)):
            e = jnp.where(same_blk[lvl] & ~same_blk[lvl - 1], m_ab, 0.0).astype(bf)
            x = x - mm(mm(x, spread(e)), spread(x.astype(bf)))
        tb = x.astype(bf)
        w = mm(jnp.concatenate([m_ak, m_rk], axis=0), spread(v))
        uu = mm(tb, jnp.concatenate([spread(a), spread(w[:CHUNK].astype(bf))], axis=1))
        ua, u0 = uu[:, :QW].astype(bf), uu[:, QW:].astype(bf)
        rr = mm(m_rb, jnp.concatenate([spread(ua), spread(u0)], axis=1))
        ry = r.astype(f32) - rr[:, :QW]
        y0 = w[CHUNK:] - rr[:, QW:]
        gm = jnp.where(same_head, lax.dot_general(ua, bh, _TN, preferred_element_type=f32), 0.0)
        hm = jnp.where(same_head, lax.dot_general(v, kh, _TN, preferred_element_type=f32)
                       - lax.dot_general(u0, bh, _TN, preferred_element_type=f32), 0.0)
        s = s_scr[bi, q]
        sb = s.astype(bf)
        y_ref[0, bi, :, sl] = lax.dot_general(ry.astype(bf), sb, _NT, preferred_element_type=f32) + y0
        s_scr[bi, q] = s * pl_ref[0, bi, 0, :, sl] - jnp.dot(sb, gm.astype(bf), preferred_element_type=f32) + hm

    @pl.when(c == n_chunks - 1)
    def _():
        sf_ref[0] = s_scr[...]


def _wkv_scan(ops, v, pl_arr, s0):
    _, B, T, W = ops[0].shape
    nc = T // CHUNK
    hw = SCAN_QUADS * QW
    ng = W // hw
    nb = SCAN_BATCH if B % SCAN_BATCH == 0 else 1
    cidx = lambda d, c: c + d * (nc - 1 - 2 * c)
    op_spec = pl.BlockSpec((1, nb, CHUNK, hw), lambda d, b, g, c: (d, b, cidx(d, c), g))
    st_spec = pl.BlockSpec((1, nb, SCAN_QUADS, QW, QW), lambda d, b, g, c: (d, b, g, 0, 0))
    kern = functools.partial(_wkv_scan_kernel, n_chunks=nc)
    return pl.pallas_call(
        kern,
        grid=(2, B // nb, ng, nc),
        in_specs=[op_spec] * 6 + [
            pl.BlockSpec((nb, CHUNK, hw), lambda d, b, g, c: (b, cidx(d, c), g)),
            pl.BlockSpec((1, nb, 1, 1, hw), lambda d, b, g, c: (d, b, cidx(d, c), 0, g)),
            st_spec,
        ],
        out_specs=[op_spec, st_spec],
        out_shape=[
            jax.ShapeDtypeStruct((2, B, T, W), jnp.float32),
            jax.ShapeDtypeStruct(s0.shape, jnp.float32),
        ],
        scratch_shapes=[pltpu.VMEM((nb, SCAN_QUADS, QW, QW), jnp.float32)],
        compiler_params=_cparams(("parallel", "parallel", "parallel", "arbitrary")),
        name="wkv_scan",
    )(*ops, v, pl_arr, s0)


def _rwkv_constants(rw_shift, rw_w_up, rw_a_up, tt):
    bf = jnp.bfloat16
    sw, swl = rw_shift[:, :3 * RW_WIDTH], rw_shift[:, 3 * RW_WIDTH:]
    zero = jnp.zeros((RW_W_LORA, RW_WIDTH), jnp.float32)
    wl = jnp.concatenate([
        jnp.concatenate([rw_w_up[0], rw_w_up[1], zero, zero], axis=1),
        jnp.concatenate([zero, zero, rw_a_up[0], rw_a_up[1]], axis=1)], axis=0).astype(bf)
    t = jnp.arange(tt)
    same = (t[:, None] // CHUNK) == (t[None, :] // CHUNK)
    tril = (same & (t[None, :] <= t[:, None])).astype(bf)
    triu = (same & (t[None, :] >= t[:, None])).astype(bf)
    return sw, swl, wl, tril, triu


def _split_bf16(a):
    hi = a.astype(jnp.bfloat16)
    return hi, (a - hi.astype(jnp.float32)).astype(jnp.bfloat16)


def _top_rows(s, k):
    n = s.shape[0]
    rev = (n - 1 - lax.broadcasted_iota(jnp.int32, s.shape, 0)).astype(jnp.float32)
    vals, revs = [], []
    for _ in range(k):
        m = jnp.max(s, axis=0, keepdims=True)
        best = jnp.max(jnp.where(s == m, rev, -1.0), axis=0, keepdims=True)
        vals.append(m)
        revs.append(best)
        s = jnp.where(rev == best, -jnp.inf, s)
    return jnp.concatenate(vals, axis=0), n - 1 - jnp.concatenate(revs, axis=0).astype(jnp.int32)


def _take_rows(table, idx):
    out = jnp.zeros(idx.shape, table.dtype)
    for a in range(table.shape[0]):
        out = jnp.where(idx == a, table[a:a + 1], out)
    return out


def _peer_topk_kernel(q_ref, keys_ref, eidx_ref, gate_ref):
    nt_dims = (((1,), (1,)), ((), ()))
    sv, si = [], []
    for p in range(2):
        q_hi, q_lo = _split_bf16(q_ref[0, :, p * N_KEYS:(p + 1) * N_KEYS])
        k_hi, k_lo = _split_bf16(keys_ref[0, p])
        s = (lax.dot_general(k_hi, q_hi, nt_dims, preferred_element_type=jnp.float32)
             + lax.dot_general(k_hi, q_lo, nt_dims, preferred_element_type=jnp.float32)
             + lax.dot_general(k_lo, q_hi, nt_dims, preferred_element_type=jnp.float32))
        v, i = _top_rows(s, PEER_TOPK)
        sv.append(v)
        si.append(i)
    pairs = [(a, b) for a in range(PEER_TOPK) for b in range(PEER_TOPK) if (a + 1) * (b + 1) <= PEER_TOPK]
    pad = -len(pairs) % SUBLANES
    neg = jnp.full((pad, sv[0].shape[1]), -jnp.inf, jnp.float32)
    cand = jnp.concatenate([sv[0][a:a + 1] + sv[1][b:b + 1] for a, b in pairs] + [neg], axis=0)
    cidx = jnp.concatenate([si[0][a:a + 1] * N_KEYS + si[1][b:b + 1] for a, b in pairs], axis=0)
    top_s, pos = _top_rows(cand, PEER_TOPK)
    eidx_ref[...] = _take_rows(cidx, pos)
    p = jnp.exp(top_s - top_s[0:1])
    gate_ref[...] = p / jnp.sum(p, axis=0, keepdims=True)


def _peer_topk(qp, keys, tt):
    B, S, _ = qp.shape
    nt = S // tt
    rows = PEER_HEADS * PEER_TOPK
    out_map = lambda b, i, h: (h, b * nt + i)
    return pl.pallas_call(
        _peer_topk_kernel,
        grid=(B, nt, PEER_HEADS),
        in_specs=[
            pl.BlockSpec((1, tt, PEER_QDIM), lambda b, i, h: (b, i, h)),
            pl.BlockSpec((1, 2, N_KEYS, PEER_QDIM // 2), lambda b, i, h: (h, 0, 0, 0)),
        ],
        out_specs=[pl.BlockSpec((PEER_TOPK, tt), out_map), pl.BlockSpec((PEER_TOPK, tt), out_map)],
        out_shape=[jax.ShapeDtypeStruct((rows, B * S), jnp.int32), jax.ShapeDtypeStruct((rows, B * S), jnp.float32)],
        compiler_params=_cparams(("parallel", "parallel", "arbitrary")),
        name="peer_topk",
    )(qp, keys)


PEER_PICKS = PEER_HEADS * PEER_TOPK
GATHER_SLOTS = 4


def _gelu_tanh(a):
    return 0.5 * a * (1.0 + jnp.tanh(math.sqrt(2.0 / math.pi) * (a + 0.044715 * (a * a * a))))


def _peer_gather_kernel(eidx_hbm, h2_ref, gate_ref, x1_ref, g2_ref, tab_hbm, out_ref, idx_smem, buf, idx_sem, row_sem,
                        *, tt):
    n_seg = tab_hbm.shape[1]
    tile = pl.program_id(0) * pl.num_programs(1) + pl.program_id(1)
    idx_copy = pltpu.make_async_copy(eidx_hbm.at[tile], idx_smem, idx_sem)
    idx_copy.start()
    idx_copy.wait()

    def issue(t, slot):
        for k in range(PEER_PICKS):
            e = idx_smem[t * PEER_PICKS + k]
            pltpu.make_async_copy(tab_hbm.at[e], buf.at[slot, pl.ds(k * n_seg, n_seg), :],
                                  row_sem.at[slot]).start(priority=k % 2)

    def wait_rows(slot):
        pltpu.make_async_copy(buf.at[slot], buf.at[slot], row_sem.at[slot]).wait()

    lane = lax.broadcasted_iota(jnp.int32, (PEER_PICKS, tt), 1)

    def words(slot, s):
        return buf[slot, pl.ds(s, PEER_PICKS, stride=n_seg), :]

    def compute(t, slot):
        xrow = h2_ref[0, pl.ds(t, 1), :]
        prod = None
        for s in range(n_seg):
            u = pltpu.bitcast(words(slot, s) & jnp.uint32(0xFFFF0000), jnp.float32)
            term = u * xrow[:, s * LANE:(s + 1) * LANE]
            prod = term if prod is None else prod + term
        act = jnp.sum(prod, axis=1, keepdims=True)
        gate = jnp.sum(jnp.where(lane == t, gate_ref[...], 0.0), axis=1, keepdims=True)
        coef = gate * _gelu_tanh(act)
        ys = [jnp.sum(coef * pltpu.bitcast(words(slot, s) << 16, jnp.float32), axis=0, keepdims=True)
              for s in range(n_seg)]
        out_ref[0, pl.ds(t, 1), :] = x1_ref[0, pl.ds(t, 1), :] + g2_ref[0] * jnp.concatenate(ys, axis=1)

    for j in range(GATHER_SLOTS):
        issue(j, j)

    def step(t, j, prefetch):
        wait_rows(j)
        compute(t, j)
        if prefetch:
            issue(t + GATHER_SLOTS, j)

    def group(g, carry):
        for j in range(GATHER_SLOTS):
            step(g * GATHER_SLOTS + j, j, True)
        return carry

    n_groups = tt // GATHER_SLOTS
    lax.fori_loop(0, n_groups - 1, group, 0)
    for j in range(GATHER_SLOTS):
        step((n_groups - 1) * GATHER_SLOTS + j, j, False)


def _peer_gather(eidx_t, h2, gate_t, x1, g2, tab, tt):
    B, S, D = x1.shape
    nt = S // tt
    tok = lambda b, i: (b, i, 0)
    kern = functools.partial(_peer_gather_kernel, tt=tt)
    return pl.pallas_call(
        kern,
        grid=(B, nt),
        in_specs=[
            pl.BlockSpec(memory_space=pl.ANY),
            pl.BlockSpec((1, tt, D), tok),
            pl.BlockSpec((PEER_PICKS, tt), lambda b, i: (0, b * nt + i)),
            pl.BlockSpec((1, tt, D), tok),
            pl.BlockSpec((1, 1, D), lambda b, i: (b, 0, 0)),
            pl.BlockSpec(memory_space=pl.ANY),
        ],
        out_specs=pl.BlockSpec((1, tt, D), tok),
        out_shape=jax.ShapeDtypeStruct((B, S, D), jnp.float32),
        scratch_shapes=[
            pltpu.SMEM((tt * PEER_PICKS,), jnp.int32),
            pltpu.VMEM((GATHER_SLOTS, PEER_PICKS * (D // LANE), LANE), jnp.uint32),
            pltpu.SemaphoreType.DMA(()),
            pltpu.SemaphoreType.DMA((GATHER_SLOTS,)),
        ],
        compiler_params=_cparams(("arbitrary", "arbitrary")),
        name="peer_gather",
    )(eidx_t, h2, gate_t, x1, g2, tab)


def _pack_expert_table(peer_u, peer_v):
    u16 = lax.bitcast_convert_type(peer_u.astype(jnp.bfloat16), jnp.uint16).astype(jnp.uint32)
    v16 = lax.bitcast_convert_type(peer_v.astype(jnp.bfloat16), jnp.uint16).astype(jnp.uint32)
    return ((u16 << 16) | v16).reshape(peer_u.shape[0], peer_u.shape[1] // LANE, LANE)


def _rope_tables(n_tokens):
    rows = n_tokens // GRID_W
    row = jnp.repeat(jnp.arange(rows, dtype=jnp.float32), GRID_W)
    col = jnp.tile(jnp.arange(GRID_W, dtype=jnp.float32), rows)
    inv = ROPE_THETA ** (-jnp.arange(ROPE_FREQS, dtype=jnp.float32) / ROPE_FREQS)
    ang = jnp.stack([row[:, None] * inv, col[:, None] * inv], axis=1)
    cos = jnp.cos(ang)[:, None, :, None, :]
    sin = jnp.sin(ang)[:, None, :, None, :]
    cos = jnp.broadcast_to(cos, (n_tokens, 2, 2, 2, ROPE_FREQS)).reshape(n_tokens, LANE)
    sgn = jnp.array([-1.0, 1.0], jnp.float32)[None, None, None, :, None]
    sin = jnp.broadcast_to(sin * sgn, (n_tokens, 2, 2, 2, ROPE_FREQS)).reshape(n_tokens, LANE)
    return cos, sin


def kernel(x, c, ctx, c_ctx, w_mod, b_mod, norm1_g, w_in, q_norm_g, k_norm_g, diff_lambda, diff_out_g, rw_shift,
           rw_w0, rw_w_up, rw_a0, rw_a_up, rw_g_up, rw_k_k, rw_k_a, rw_r_k, rw_ln_g, rw_ln_b, w_branch_a,
           w_branch_b, w_out, norm2_g, peer_wq, peer_keys, peer_u, peer_v):
    assert w_mod.shape[0] == 1, "single-layer trunk only"
    B, S, D = x.shape
    C = ctx.shape[1]
    bf = jnp.bfloat16
    lam_init = 0.8 - 0.6 * math.exp(-0.3 * 0)

    tiles = _tiles(S)

    c_rows = jnp.concatenate([c, c_ctx[None, :]], axis=0)
    c_rows = jnp.pad(c_rows, ((0, -c_rows.shape[0] % SUBLANES), (0, 0)))
    mod = _adaln(c_rows, w_mod[0], b_mod[0][None, :], tiles.proj_cols)
    mod_x = mod[:B].reshape(B, N_MOD, 1, D)
    mod_c = mod[B:B + 1].reshape(1, N_MOD, 1, D)
    sc1x, sh1x = norm1_g[0] * (1 + mod_x[:, 1]), mod_x[:, 0]
    sc1c, sh1c = norm1_g[0] * (1 + mod_c[:, 1]), mod_c[:, 0]
    sc2x, sh2x = norm2_g[0] * (1 + mod_x[:, 4]), mod_x[:, 3]
    g1x, g2x = mod_x[:, 2], mod_x[:, 5]

    w_in_p = jnp.concatenate([w_in[0][:, :6144], w_in[0][:, 6400:], w_in[0][:, 6144:6400]], axis=1).astype(bf)
    zx = _norm_matmul(x, sc1x, sh1x, w_in_p, tiles.proj_rows, tiles.proj_cols)
    zc = _norm_matmul(ctx, sc1c, sh1c, w_in_p, min(tiles.proj_rows, C), tiles.proj_cols)

    lam = (jnp.exp(jnp.sum(diff_lambda[0, 0] * diff_lambda[0, 1])) - jnp.exp(jnp.sum(diff_lambda[0, 2] * diff_lambda[0, 3]))
           + lam_init).reshape(1)
    cos, sin_signed = _rope_tables(S)
    qg = jnp.tile(q_norm_g[0], 2).reshape(1, LANE)
    kg = jnp.tile(k_norm_g[0], 2).reshape(1, LANE)
    og = diff_out_g[0].reshape(1, LANE)
    grp = jnp.arange(LANE) // DA_QK
    avg64 = jnp.where(grp[:, None] == grp[None, :], 1.0 / DA_QK, 0.0).astype(bf)
    o_a = _diff_attention(lam, zx, zc, cos, sin_signed, qg, kg, og, avg64, tiles.attn_q, 1.0 - lam_init)

    hd = jnp.arange(RW_WIDTH) // RW_HEAD
    ones_bd = (hd[:, None] == jnp.arange(LANE)[None, :]).astype(bf)
    row1 = lambda t: t.reshape(1, RW_WIDTH)
    prep = {}
    for name, z, T in (("ctx", zc, C), ("x", zx, S)):
        tt = min(tiles.rwkv, T)
        sw, swl, wl, tril, triu = _rwkv_constants(rw_shift[0], rw_w_up[0], rw_a_up[0], tt)
        prep[name] = _rwkv_prepare(z, sw, swl, rw_w0[0], rw_a0[0], row1(rw_k_k[0]), row1(rw_k_a[0]), row1(rw_r_k[0]),
                                   wl, rw_g_up[0].astype(bf), ones_bd, tril, triu, tt)
    s0 = jnp.zeros((2, B, RW_WIDTH // QW, QW, QW), jnp.float32)
    pc, px = prep["ctx"], prep["x"]
    _, s_ctx = _wkv_scan(pc[0:6], pc[6], pc[7], s0)
    y_rw, _ = _wkv_scan(px[0:6], px[6], px[7], s_ctx)

    x1, h2, qp = _merge(x, o_a, y_rw, px[8], px[9], row1(rw_ln_g[0]), row1(rw_ln_b[0]), ones_bd, zx, g1x,
                        w_branch_a[0].astype(bf), w_branch_b[0].astype(bf), w_out[0].astype(bf),
                        sc2x, sh2x, peer_wq[0].astype(bf), tiles.merge, jnp.float32)

    eidx_t, gate_t = _peer_topk(qp, peer_keys[0], tiles.topk)
    eidx = eidx_t.T.reshape(B * S // tiles.gather, tiles.gather * PEER_PICKS)
    tab = _pack_expert_table(peer_u[0], peer_v[0])
    return _peer_gather(eidx, h2, gate_t, x1, g2x, tab, tiles.gather)
```

```python
import functools
import math
from typing import NamedTuple

import jax
import jax.numpy as jnp
from jax import lax
from jax.experimental import pallas as pl
from jax.experimental.pallas import tpu as pltpu

N_MOD = 6
EPS = 1e-6
GRID_W = 64
DA_HEADS = 8
DA_QK = 64
DA_V = 2 * DA_QK
ROPE_THETA = 10000.0
ROPE_FREQS = DA_QK // 4
RW_HEADS = 16
RW_HEAD = 64
RW_WIDTH = RW_HEADS * RW_HEAD
RW_W_LORA = 64
RW_LN_EPS = 64e-5
L2_EPS = 1e-12
PEER_HEADS = 8
N_KEYS = 128
PEER_TOPK = 16
PEER_QDIM = 256

COL_Q = 0
COL_K = 1024
COL_V = 2048
COL_RW = 3072
COL_GATE = 6144
COL_LORA = 8192

LANE = 128
SUBLANES = 8
VMEM_LIMIT = 56 * 1024 * 1024


class Tiles(NamedTuple):
    proj_rows: int
    proj_cols: int
    attn_q: int
    rwkv: int
    merge: int
    topk: int
    gather: int


def _tiles(seq_len):
    return Tiles(proj_rows=min(1024, seq_len), proj_cols=768, attn_q=min(256, seq_len), rwkv=min(256, seq_len),
                 merge=min(256, seq_len), topk=min(1024, seq_len), gather=min(512, seq_len))


def _cparams(sem):
    return pltpu.CompilerParams(dimension_semantics=sem, vmem_limit_bytes=VMEM_LIMIT)


def _adaln_kernel(c_ref, w_ref, b_ref, o_ref):
    a_hi, a_lo = _split_bf16(jax.nn.silu(c_ref[...]))
    w_hi, w_lo = _split_bf16(w_ref[...])
    dot = functools.partial(jnp.dot, preferred_element_type=jnp.float32)
    o_ref[...] = dot(a_hi, w_hi) + dot(a_hi, w_lo) + dot(a_lo, w_hi) + b_ref[...]


def _adaln(c_rows, w_mod, b_mod, tn):
    R, D = c_rows.shape
    N = w_mod.shape[1]
    return pl.pallas_call(
        _adaln_kernel,
        grid=(N // tn,),
        in_specs=[pl.BlockSpec((R, D), lambda j: (0, 0)), pl.BlockSpec((D, tn), lambda j: (0, j)),
                  pl.BlockSpec((1, tn), lambda j: (0, j))],
        out_specs=pl.BlockSpec((R, tn), lambda j: (0, j)),
        out_shape=jax.ShapeDtypeStruct((R, N), jnp.float32),
        compiler_params=_cparams(("parallel",)),
        name="adaln",
    )(c_rows, w_mod, b_mod)


def _norm_matmul_kernel(x_ref, sc_ref, sh_ref, w_ref, z_ref, hn_ref):
    @pl.when(pl.program_id(2) == 0)
    def _():
        x = x_ref[0]
        ms = jnp.mean(x * x, axis=-1, keepdims=True)
        h = x * lax.rsqrt(ms + EPS) * sc_ref[0] + sh_ref[0]
        hn_ref[...] = h.astype(jnp.bfloat16)

    z_ref[0] = jnp.dot(hn_ref[...], w_ref[...], preferred_element_type=jnp.float32)


def _norm_matmul(x, sc, sh, w, tm, tn):
    B, T, D = x.shape
    N = w.shape[1]
    bm = (lambda b, i, j: (b, 0, 0)) if sc.shape[0] == B else (lambda b, i, j: (0, 0, 0))
    return pl.pallas_call(
        _norm_matmul_kernel,
        grid=(B, T // tm, N // tn),
        in_specs=[
            pl.BlockSpec((1, tm, D), lambda b, i, j: (b, i, 0)),
            pl.BlockSpec((1, 1, D), bm),
            pl.BlockSpec((1, 1, D), bm),
            pl.BlockSpec((D, tn), lambda b, i, j: (0, j)),
        ],
        out_specs=pl.BlockSpec((1, tm, tn), lambda b, i, j: (b, i, j)),
        out_shape=jax.ShapeDtypeStruct((B, T, N), jnp.float32),
        scratch_shapes=[pltpu.VMEM((tm, D), jnp.bfloat16)],
        compiler_params=_cparams(("parallel", "parallel", "arbitrary")),
        name="norm_matmul",
    )(x, sc, sh, w)


def _group_mean_sq(x, avg):
    sq = x * x
    hi = sq.astype(jnp.bfloat16)
    lo = (sq - hi.astype(jnp.float32)).astype(jnp.bfloat16)
    return (jnp.dot(hi, avg, preferred_element_type=jnp.float32)
            + jnp.dot(lo, avg, preferred_element_type=jnp.float32))


def _head_sum(x, head_sel):
    f32 = jnp.float32
    hi, lo = _split_bf16(x)
    sums = jnp.dot(hi, head_sel, preferred_element_type=f32) + jnp.dot(lo, head_sel, preferred_element_type=f32)
    s_hi, s_lo = _split_bf16(sums)
    nt = (((1,), (1,)), ((), ()))
    return (lax.dot_general(s_hi, head_sel, nt, preferred_element_type=f32)
            + lax.dot_general(s_lo, head_sel, nt, preferred_element_type=f32))


def _rope(x, cos, sin_signed, first_half):
    partner = jnp.where(first_half, pltpu.roll(x, LANE - ROPE_FREQS, axis=1), pltpu.roll(x, ROPE_FREQS, axis=1))
    return x * cos + partner * sin_signed


def _diff_attn_kernel(lam_ref, q_ref, kx_ref, vx_ref, kc_ref, vc_ref, cosq_ref, sinq_ref, cosk_ref, sink_ref,
                      qg_ref, kg_ref, og_ref, avg64_ref, o_ref, k_s, v_s, *, n_ctx, out_scale):
    lane = lax.broadcasted_iota(jnp.int32, (1, LANE), 1)
    first_half = (lane % (2 * ROPE_FREQS)) < ROPE_FREQS
    avg64 = avg64_ref[...]

    @pl.when(pl.program_id(2) == 0)
    def _():
        kc = kc_ref[0]
        kc = kc * lax.rsqrt(_group_mean_sq(kc, avg64) + EPS) * kg_ref[...]
        k_s[0:n_ctx, :] = kc.astype(jnp.bfloat16)
        kx = kx_ref[0]
        kx = kx * lax.rsqrt(_group_mean_sq(kx, avg64) + EPS) * kg_ref[...]
        kx = _rope(kx, cosk_ref[...], sink_ref[...], first_half)
        k_s[n_ctx:, :] = kx.astype(jnp.bfloat16)
        v_s[0:n_ctx, :] = vc_ref[0].astype(jnp.bfloat16)
        v_s[n_ctx:, :] = vx_ref[0].astype(jnp.bfloat16)

    q = q_ref[0]
    q = q * lax.rsqrt(_group_mean_sq(q, avg64) + EPS) * qg_ref[...]
    q = _rope(q, cosq_ref[...], sinq_ref[...], first_half) * (DA_QK ** -0.5 * math.log2(math.e))
    k = k_s[...]
    ps = []
    for m in range(2):
        in_map = (lane // DA_QK) == m
        qm = jnp.where(in_map, q, 0.0).astype(jnp.bfloat16)
        s = lax.dot_general(qm, k, (((1,), (1,)), ((), ())), preferred_element_type=jnp.float32)
        p = jnp.exp2(s - jnp.max(s, axis=-1, keepdims=True))
        scale = (1.0 if m == 0 else lam_ref[0]) / jnp.sum(p, axis=-1, keepdims=True)
        ps.append(p * scale)
    pd = (ps[0] - ps[1]).astype(jnp.bfloat16)
    o = jnp.dot(pd, v_s[...], preferred_element_type=jnp.float32)
    o = o * lax.rsqrt(jnp.mean(o * o, axis=-1, keepdims=True) + EPS) * og_ref[...] * out_scale
    o_ref[0] = o.astype(o_ref.dtype)


def _diff_attention(lam, zx, zc, cos, sin_signed, qg, kg, og, avg64, tq, out_scale):
    B, S, _ = zx.shape
    C = zc.shape[1]
    qb, kb, vb = COL_Q // LANE, COL_K // LANE, COL_V // LANE
    kern = functools.partial(_diff_attn_kernel, n_ctx=C, out_scale=out_scale)
    const = lambda b, h, i: (0, 0)
    return pl.pallas_call(
        kern,
        grid=(B, DA_HEADS, S // tq),
        in_specs=[
            pl.BlockSpec(memory_space=pltpu.SMEM),
            pl.BlockSpec((1, tq, LANE), lambda b, h, i: (b, i, qb + h)),
            pl.BlockSpec((1, S, LANE), lambda b, h, i: (b, 0, kb + h)),
            pl.BlockSpec((1, S, LANE), lambda b, h, i: (b, 0, vb + h)),
            pl.BlockSpec((1, C, LANE), lambda b, h, i: (b, 0, kb + h)),
            pl.BlockSpec((1, C, LANE), lambda b, h, i: (b, 0, vb + h)),
            pl.BlockSpec((tq, LANE), lambda b, h, i: (i, 0)),
            pl.BlockSpec((tq, LANE), lambda b, h, i: (i, 0)),
            pl.BlockSpec((S, LANE), const),
            pl.BlockSpec((S, LANE), const),
            pl.BlockSpec((1, LANE), const),
            pl.BlockSpec((1, LANE), const),
            pl.BlockSpec((1, LANE), const),
            pl.BlockSpec((LANE, LANE), const),
        ],
        out_specs=pl.BlockSpec((1, tq, LANE), lambda b, h, i: (b, i, h)),
        out_shape=jax.ShapeDtypeStruct((B, S, DA_HEADS * DA_V), jnp.bfloat16),
        scratch_shapes=[pltpu.VMEM((C + S, LANE), jnp.bfloat16), pltpu.VMEM((C + S, LANE), jnp.bfloat16)],
        compiler_params=_cparams(("parallel", "parallel", "arbitrary")),
        name="diff_attention",
    )(lam, zx, zx, zx, zc, zc, cos, sin_signed, cos, sin_signed, qg, kg, og, avg64)


def _merge_kernel(x_ref, oa_ref, y0_ref, y1_ref, bonus_ref, g_ref, lng_ref, lnb_ref, ones_ref, ga_ref, gb_ref, g1_ref,
                  wa_ref, wb_ref, wo_ref, sc2_ref, sh2_ref, wq_ref, x1_ref, h2_ref, qp_ref):
    ones_bd = ones_ref[...]
    yy = y0_ref[0, 0] + y1_ref[0, 0]
    dlt = yy - _head_sum(yy, ones_bd) * (1.0 / RW_HEAD)
    var = _head_sum(dlt * dlt, ones_bd) * (1.0 / RW_HEAD)
    ob = (dlt * lax.rsqrt(var + RW_LN_EPS) * lng_ref[...] + lnb_ref[...] + bonus_ref[0]) * g_ref[0]
    ta = jnp.dot(oa_ref[0], wa_ref[...], preferred_element_type=jnp.float32)
    tb = jnp.dot(ob.astype(jnp.bfloat16), wb_ref[...], preferred_element_type=jnp.float32)
    mix = jax.nn.sigmoid(ga_ref[0]) * ta + jax.nn.sigmoid(gb_ref[0]) * tb
    y = jnp.dot(mix.astype(jnp.bfloat16), wo_ref[...], preferred_element_type=jnp.float32)
    x1 = x_ref[0] + g1_ref[0] * y
    x1_ref[0] = x1
    ms = jnp.mean(x1 * x1, axis=-1, keepdims=True)
    h2 = x1 * lax.rsqrt(ms + EPS) * sc2_ref[0] + sh2_ref[0]
    h2_ref[0] = h2.astype(h2_ref.dtype)
    qp_ref[0] = jnp.dot(h2.astype(jnp.bfloat16), wq_ref[...], preferred_element_type=jnp.float32)


def _merge(x, oa, y, bonus, g, ln_g, ln_b, ones_bd, zx, g1, wa, wb, wo, sc2, sh2, wq, tm, h2_dtype):
    B, S, D = x.shape
    NQ = wq.shape[1]
    tok = lambda b, i: (b, i, 0)
    per_b = lambda b, i: (b, 0, 0)
    const = lambda b, i: (0, 0)
    ga_blk, gb_blk = COL_GATE // D, COL_GATE // D + 1
    return pl.pallas_call(
        _merge_kernel,
        grid=(B, S // tm),
        in_specs=[
            pl.BlockSpec((1, tm, D), tok),
            pl.BlockSpec((1, tm, D), tok),
            pl.BlockSpec((1, 1, tm, D), lambda b, i: (0, b, i, 0)),
            pl.BlockSpec((1, 1, tm, D), lambda b, i: (1, b, i, 0)),
            pl.BlockSpec((1, tm, D), tok),
            pl.BlockSpec((1, tm, D), tok),
            pl.BlockSpec((1, D), const),
            pl.BlockSpec((1, D), const),
            pl.BlockSpec((D, LANE), const),
            pl.BlockSpec((1, tm, D), lambda b, i: (b, i, ga_blk)),
            pl.BlockSpec((1, tm, D), lambda b, i: (b, i, gb_blk)),
            pl.BlockSpec((1, 1, D), per_b),
            pl.BlockSpec((D, D), const),
            pl.BlockSpec((D, D), const),
            pl.BlockSpec((D, D), const),
            pl.BlockSpec((1, 1, D), per_b),
            pl.BlockSpec((1, 1, D), per_b),
            pl.BlockSpec((D, NQ), const),
        ],
        out_specs=[
            pl.BlockSpec((1, tm, D), tok),
            pl.BlockSpec((1, tm, D), tok),
            pl.BlockSpec((1, tm, NQ), tok),
        ],
        out_shape=[
            jax.ShapeDtypeStruct((B, S, D), jnp.float32),
            jax.ShapeDtypeStruct((B, S, D), h2_dtype),
            jax.ShapeDtypeStruct((B, S, NQ), jnp.float32),
        ],
        compiler_params=_cparams(("parallel", "parallel")),
        name="merge_peerq",
    )(x, oa, y, y, bonus, g, ln_g, ln_b, ones_bd, zx, zx, g1, wa, wb, wo, sc2, sh2, wq)


CHUNK = 64
QUAD = 4
QW = QUAD * RW_HEAD
SCAN_QUADS = 4
SCAN_BATCH = 2


def _split3(a):
    hi = a.astype(jnp.bfloat16)
    r1 = a - hi.astype(jnp.float32)
    mid = r1.astype(jnp.bfloat16)
    lo = (r1 - mid.astype(jnp.float32)).astype(jnp.bfloat16)
    return hi, mid, lo


def _dot3(m, parts):
    acc = jnp.dot(m, parts[0], preferred_element_type=jnp.float32)
    acc = acc + jnp.dot(m, parts[1], preferred_element_type=jnp.float32)
    return acc + jnp.dot(m, parts[2], preferred_element_type=jnp.float32)


def _shift3(z, prev_row, next_row, w):
    n = z.shape[0]
    row = lax.broadcasted_iota(jnp.int32, z.shape, 0)
    zm = jnp.where(row == 0, prev_row, pltpu.roll(z, 1, axis=0))
    zp = jnp.where(row == n - 1, next_row, pltpu.roll(z, n - 1, axis=0))
    return w[0:1] * zm + w[1:2] * z + w[2:3] * zp


def _rwkv_prep_kernel(z_ref, zp_ref, zn_ref, l_ref, lp_ref, ln_ref, sw_ref, swl_ref, w0_ref, a0_ref, kk_ref, ka_ref,
                      rk_ref, wl_ref, gup_ref, ones_ref, tril_ref, triu_ref,
                      at_ref, bt_ref, kt_ref, rt_ref, bh_ref, kh_ref, v_ref, pl_ref, bonus_ref, g_ref, *, n_tiles):
    i = pl.program_id(1)
    inner_lo = 1.0 - (i == 0).astype(jnp.float32)
    inner_hi = 1.0 - (i == n_tiles - 1).astype(jnp.float32)
    zs = _shift3(z_ref[0], zp_ref[0, 7:8, :] * inner_lo, zn_ref[0, 0:1, :] * inner_hi, sw_ref[...])
    ls = _shift3(l_ref[0], lp_ref[0, 7:8, :] * inner_lo, ln_ref[0, 0:1, :] * inner_hi, swl_ref[...])
    r, k, v = zs[:, 0:RW_WIDTH], zs[:, RW_WIDTH:2 * RW_WIDTH], zs[:, 2 * RW_WIDTH:3 * RW_WIDTH]
    lane = lax.broadcasted_iota(jnp.int32, (1, LANE), 1)
    wa_in = jnp.where(lane < RW_W_LORA, jnp.tanh(ls[:, 0:LANE]), ls[:, 0:LANE]).astype(jnp.bfloat16)
    lora = jnp.dot(wa_in, wl_ref[...], preferred_element_type=jnp.float32)
    ones_bd = ones_ref[...]
    kn = k * kk_ref[...]
    kk = kn * lax.rsqrt(_head_sum(kn * kn, ones_bd) + L2_EPS)
    bonus_ref[0] = _head_sum(r * k * rk_ref[...], ones_bd) * v
    g_ref[0] = jnp.dot(jax.nn.sigmoid(ls[:, LANE:2 * LANE]).astype(jnp.bfloat16), gup_ref[...],
                       preferred_element_type=jnp.float32)
    v_ref[0] = v.astype(v_ref.dtype)
    tris = (tril_ref[...], triu_ref[...])
    for d in range(2):
        lw = lora[:, d * RW_WIDTH:(d + 1) * RW_WIDTH] + w0_ref[d:d + 1, :]
        logw = -jnp.exp(-jax.nn.softplus(-lw) - 0.5)
        a = jax.nn.sigmoid(a0_ref[d:d + 1, :] + lora[:, (2 + d) * RW_WIDTH:(3 + d) * RW_WIDTH])
        k_eff = k * (1.0 + (a - 1.0) * ka_ref[...])
        b = kk * a
        parts = _split3(logw)
        incl = _dot3(tris[d], parts)
        rest = _dot3(tris[1 - d], parts) - logw
        at_ref[d, 0] = (kk * jnp.exp(incl - logw)).astype(at_ref.dtype)
        rt_ref[d, 0] = (r * jnp.exp(incl)).astype(rt_ref.dtype)
        inv = jnp.exp(-incl)
        bt_ref[d, 0] = (b * inv).astype(bt_ref.dtype)
        kt_ref[d, 0] = (k_eff * inv).astype(kt_ref.dtype)
        to_end = jnp.exp(rest)
        bh_ref[d, 0] = (b * to_end).astype(bh_ref.dtype)
        kh_ref[d, 0] = (k_eff * to_end).astype(kh_ref.dtype)
        tot = incl + rest
        for c in range(tot.shape[0] // CHUNK):
            pl_ref[d, 0, c] = jnp.exp(tot[c * CHUNK:c * CHUNK + 1])


def _rwkv_prepare(z, sw, swl, w0, a0, k_k, k_a, r_k, wl, g_up, ones_bd, tril, triu, tt):
    B, T, _ = z.shape
    nt = T // tt
    rb, lb = COL_RW // (3 * RW_WIDTH), COL_LORA // (2 * LANE)
    tb = tt // 8
    prev = lambda i: jnp.maximum(i * tb - 1, 0)
    nxt = lambda i: jnp.minimum((i + 1) * tb, T // 8 - 1)
    const2 = lambda b, i: (0, 0)
    tok = lambda b, i: (b, i, 0)
    dtok = lambda b, i: (0, b, i, 0)
    bf = jnp.bfloat16
    kern = functools.partial(_rwkv_prep_kernel, n_tiles=nt)
    return pl.pallas_call(
        kern,
        grid=(B, nt),
        in_specs=[
            pl.BlockSpec((1, tt, 3 * RW_WIDTH), lambda b, i: (b, i, rb)),
            pl.BlockSpec((1, 8, 3 * RW_WIDTH), lambda b, i: (b, prev(i), rb)),
            pl.BlockSpec((1, 8, 3 * RW_WIDTH), lambda b, i: (b, nxt(i), rb)),
            pl.BlockSpec((1, tt, 2 * LANE), lambda b, i: (b, i, lb)),
            pl.BlockSpec((1, 8, 2 * LANE), lambda b, i: (b, prev(i), lb)),
            pl.BlockSpec((1, 8, 2 * LANE), lambda b, i: (b, nxt(i), lb)),
            pl.BlockSpec((3, 3 * RW_WIDTH), const2),
            pl.BlockSpec((3, 2 * LANE), const2),
            pl.BlockSpec((2, RW_WIDTH), const2),
            pl.BlockSpec((2, RW_WIDTH), const2),
            pl.BlockSpec((1, RW_WIDTH), const2),
            pl.BlockSpec((1, RW_WIDTH), const2),
            pl.BlockSpec((1, RW_WIDTH), const2),
            pl.BlockSpec((LANE, 4 * RW_WIDTH), const2),
            pl.BlockSpec((LANE, RW_WIDTH), const2),
            pl.BlockSpec((RW_WIDTH, LANE), const2),
            pl.BlockSpec((tt, tt), const2),
            pl.BlockSpec((tt, tt), const2),
        ],
        out_specs=[pl.BlockSpec((2, 1, tt, RW_WIDTH), dtok)] * 6 + [
            pl.BlockSpec((1, tt, RW_WIDTH), tok),
            pl.BlockSpec((2, 1, tt // CHUNK, 1, RW_WIDTH), lambda b, i: (0, b, i, 0, 0)),
            pl.BlockSpec((1, tt, RW_WIDTH), tok),
            pl.BlockSpec((1, tt, RW_WIDTH), tok),
        ],
        out_shape=[jax.ShapeDtypeStruct((2, B, T, RW_WIDTH), bf)] * 6 + [
            jax.ShapeDtypeStruct((B, T, RW_WIDTH), bf),
            jax.ShapeDtypeStruct((2, B, T // CHUNK, 1, RW_WIDTH), jnp.float32),
            jax.ShapeDtypeStruct((B, T, RW_WIDTH), jnp.float32),
            jax.ShapeDtypeStruct((B, T, RW_WIDTH), jnp.float32),
        ],
        compiler_params=_cparams(("parallel", "parallel")),
        name="rwkv_prepare",
    )(z, z, z, z, z, z, sw, swl, w0, a0, k_k, k_a, r_k, wl, g_up, ones_bd, tril, triu)


_NT = (((1,), (1,)), ((), ()))
_TN = (((0,), (0,)), ((), ()))


def _wkv_scan_kernel(at_ref, bt_ref, kt_ref, rt_ref, bh_ref, kh_ref, v_ref, pl_ref, s0_ref, y_ref, sf_ref, s_scr,
                     *, n_chunks):
    d = pl.program_id(0)
    c = pl.program_id(3)
    f32, bf = jnp.float32, jnp.bfloat16

    @pl.when(c == 0)
    def _():
        s_scr[...] = s0_ref[0]

    row = lax.broadcasted_iota(jnp.int32, (QW, QW), 0)
    col = lax.broadcasted_iota(jnp.int32, (QW, QW), 1)
    same_head = (row // RW_HEAD) == (col // RW_HEAD)
    tok = lax.broadcasted_iota(jnp.int32, (CHUNK, QW), 0)
    src = lax.broadcasted_iota(jnp.int32, (CHUNK, QW), 1) % RW_HEAD
    order = (tok - src) * (1 - 2 * d)
    strict = order > 0
    incl = order >= 0
    eye = (tok == src).astype(f32)
    blk_sizes = [4 << i for i in range(CHUNK.bit_length() - 2)]
    same_blk = [(tok // bs) == (src // bs) for bs in blk_sizes]

    def spread(t):
        return jnp.where(same_head, jnp.concatenate([t] * QUAD, axis=0), jnp.zeros((), t.dtype))

    def mm(lhs, rhs):
        return jnp.dot(lhs.astype(bf), rhs.astype(bf), preferred_element_type=f32)

    for bi, q in [(bi, q) for bi in range(at_ref.shape[1]) for q in range(SCAN_QUADS)]:
        sl = slice(q * QW, (q + 1) * QW)
        a, b, k, r = at_ref[0, bi, :, sl], bt_ref[0, bi, :, sl], kt_ref[0, bi, :, sl], rt_ref[0, bi, :, sl]
        bh, kh, v = bh_ref[0, bi, :, sl], kh_ref[0, bi, :, sl], v_ref[bi, :, sl]
        ar = jnp.concatenate([a, r], axis=0)
        xb = lax.dot_general(ar, spread(b), _NT, preferred_element_type=f32)
        xk = lax.dot_general(ar, spread(k), _NT, preferred_element_type=f32)
        m_ab = jnp.where(strict, xb[:CHUNK], 0.0)
        m_ak = jnp.where(strict, xk[:CHUNK], 0.0)
        m_rb = jnp.where(incl, xb[CHUNK:], 0.0).astype(bf)
        m_rk = jnp.where(incl, xk[CHUNK:], 0.0)
        n0 = jnp.where(same_blk[0], m_ab, 0.0).astype(bf)
        x = eye - n0.astype(f32)
        x = x + mm(x, spread(mm(n0, spread(n0)).astype(bf)))
        for lvl in range(1, len(same_blk)):
            e = jnp.where(same_blk[lvl] & ~same_blk[lvl - 1], m_ab, 0.0).astype(bf)
            x = x - mm(mm(x, spread(e)), spread(x.astype(bf)))
        tb = x.astype(bf)
        w = mm(jnp.concatenate([m_ak, m_rk], axis=0), spread(v))
        uu = mm(tb, jnp.concatenate([spread(a), spread(w[:CHUNK].astype(bf))], axis=1))
        ua, u0 = uu[:, :QW].astype(bf), uu[:, QW:].astype(bf)
        rr = mm(m_rb, jnp.concatenate([spread(ua), spread(u0)], axis=1))
        ry = r.astype(f32) - rr[:, :QW]
        y0 = w[CHUNK:] - rr[:, QW:]
        gm = jnp.where(same_head, lax.dot_general(ua, bh, _TN, preferred_element_type=f32), 0.0)
        hm = jnp.where(same_head, lax.dot_general(v, kh, _TN, preferred_element_type=f32)
                       - lax.dot_general(u0, bh, _TN, preferred_element_type=f32), 0.0)
        s = s_scr[bi, q]
        sb = s.astype(bf)
        y_ref[0, bi, :, sl] = lax.dot_general(ry.astype(bf), sb, _NT, preferred_element_type=f32) + y0
        s_scr[bi, q] = s * pl_ref[0, bi, 0, :, sl] - jnp.dot(sb, gm.astype(bf), preferred_element_type=f32) + hm

    @pl.when(c == n_chunks - 1)
    def _():
        sf_ref[0] = s_scr[...]


def _wkv_scan(ops, v, pl_arr, s0):
    _, B, T, W = ops[0].shape
    nc = T // CHUNK
    hw = SCAN_QUADS * QW
    ng = W // hw
    nb = SCAN_BATCH if B % SCAN_BATCH == 0 else 1
    cidx = lambda d, c: c + d * (nc - 1 - 2 * c)
    op_spec = pl.BlockSpec((1, nb, CHUNK, hw), lambda d, b, g, c: (d, b, cidx(d, c), g))
    st_spec = pl.BlockSpec((1, nb, SCAN_QUADS, QW, QW), lambda d, b, g, c: (d, b, g, 0, 0))
    kern = functools.partial(_wkv_scan_kernel, n_chunks=nc)
    return pl.pallas_call(
        kern,
        grid=(2, B // nb, ng, nc),
        in_specs=[op_spec] * 6 + [
            pl.BlockSpec((nb, CHUNK, hw), lambda d, b, g, c: (b, cidx(d, c), g)),
            pl.BlockSpec((1, nb, 1, 1, hw), lambda d, b, g, c: (d, b, cidx(d, c), 0, g)),
            st_spec,
        ],
        out_specs=[op_spec, st_spec],
        out_shape=[
            jax.ShapeDtypeStruct((2, B, T, W), jnp.float32),
            jax.ShapeDtypeStruct(s0.shape, jnp.float32),
        ],
        scratch_shapes=[pltpu.VMEM((nb, SCAN_QUADS, QW, QW), jnp.float32)],
        compiler_params=_cparams(("parallel", "parallel", "parallel", "arbitrary")),
        name="wkv_scan",
    )(*ops, v, pl_arr, s0)


def _rwkv_constants(rw_shift, rw_w_up, rw_a_up, tt):
    bf = jnp.bfloat16
    sw, swl = rw_shift[:, :3 * RW_WIDTH], rw_shift[:, 3 * RW_WIDTH:]
    zero = jnp.zeros((RW_W_LORA, RW_WIDTH), jnp.float32)
    wl = jnp.concatenate([
        jnp.concatenate([rw_w_up[0], rw_w_up[1], zero, zero], axis=1),
        jnp.concatenate([zero, zero, rw_a_up[0], rw_a_up[1]], axis=1)], axis=0).astype(bf)
    t = jnp.arange(tt)
    same = (t[:, None] // CHUNK) == (t[None, :] // CHUNK)
    tril = (same & (t[None, :] <= t[:, None])).astype(bf)
    triu = (same & (t[None, :] >= t[:, None])).astype(bf)
    return sw, swl, wl, tril, triu


def _split_bf16(a):
    hi = a.astype(jnp.bfloat16)
    return hi, (a - hi.astype(jnp.float32)).astype(jnp.bfloat16)


def _top_rows(s, k):
    n = s.shape[0]
    rev = (n - 1 - lax.broadcasted_iota(jnp.int32, s.shape, 0)).astype(jnp.float32)
    vals, revs = [], []
    for _ in range(k):
        m = jnp.max(s, axis=0, keepdims=True)
        best = jnp.max(jnp.where(s == m, rev, -1.0), axis=0, keepdims=True)
        vals.append(m)
        revs.append(best)
        s = jnp.where(rev == best, -jnp.inf, s)
    return jnp.concatenate(vals, axis=0), n - 1 - jnp.concatenate(revs, axis=0).astype(jnp.int32)


def _take_rows(table, idx):
    out = jnp.zeros(idx.shape, table.dtype)
    for a in range(table.shape[0]):
        out = jnp.where(idx == a, table[a:a + 1], out)
    return out


def _peer_topk_kernel(q_ref, keys_ref, eidx_ref, gate_ref):
    nt_dims = (((1,), (1,)), ((), ()))
    sv, si = [], []
    for p in range(2):
        q_hi, q_lo = _split_bf16(q_ref[0, :, p * N_KEYS:(p + 1) * N_KEYS])
        k_hi, k_lo = _split_bf16(keys_ref[0, p])
        s = (lax.dot_general(k_hi, q_hi, nt_dims, preferred_element_type=jnp.float32)
             + lax.dot_general(k_hi, q_lo, nt_dims, preferred_element_type=jnp.float32)
             + lax.dot_general(k_lo, q_hi, nt_dims, preferred_element_type=jnp.float32))
        v, i = _top_rows(s, PEER_TOPK)
        sv.append(v)
        si.append(i)
    pairs = [(a, b) for a in range(PEER_TOPK) for b in range(PEER_TOPK) if (a + 1) * (b + 1) <= PEER_TOPK]
    pad = -len(pairs) % SUBLANES
    neg = jnp.full((pad, sv[0].shape[1]), -jnp.inf, jnp.float32)
    cand = jnp.concatenate([sv[0][a:a + 1] + sv[1][b:b + 1] for a, b in pairs] + [neg], axis=0)
    cidx = jnp.concatenate([si[0][a:a + 1] * N_KEYS + si[1][b:b + 1] for a, b in pairs], axis=0)
    top_s, pos = _top_rows(cand, PEER_TOPK)
    eidx_ref[...] = _take_rows(cidx, pos)
    p = jnp.exp(top_s - top_s[0:1])
    gate_ref[...] = p / jnp.sum(p, axis=0, keepdims=True)


def _peer_topk(qp, keys, tt):
    B, S, _ = qp.shape
    nt = S // tt
    rows = PEER_HEADS * PEER_TOPK
    out_map = lambda b, i, h: (h, b * nt + i)
    return pl.pallas_call(
        _peer_topk_kernel,
        grid=(B, nt, PEER_HEADS),
        in_specs=[
            pl.BlockSpec((1, tt, PEER_QDIM), lambda b, i, h: (b, i, h)),
            pl.BlockSpec((1, 2, N_KEYS, PEER_QDIM // 2), lambda b, i, h: (h, 0, 0, 0)),
        ],
        out_specs=[pl.BlockSpec((PEER_TOPK, tt), out_map), pl.BlockSpec((PEER_TOPK, tt), out_map)],
        out_shape=[jax.ShapeDtypeStruct((rows, B * S), jnp.int32), jax.ShapeDtypeStruct((rows, B * S), jnp.float32)],
        compiler_params=_cparams(("parallel", "parallel", "arbitrary")),
        name="peer_topk",
    )(qp, keys)


PEER_PICKS = PEER_HEADS * PEER_TOPK
GATHER_SLOTS = 4


def _gelu_tanh(a):
    return 0.5 * a * (1.0 + jnp.tanh(math.sqrt(2.0 / math.pi) * (a + 0.044715 * (a * a * a))))


def _peer_gather_kernel(eidx_hbm, h2_ref, gate_ref, x1_ref, g2_ref, tab_hbm, out_ref, idx_smem, buf, idx_sem, row_sem,
                        *, tt):
    n_seg = tab_hbm.shape[1]
    tile = pl.program_id(0) * pl.num_programs(1) + pl.program_id(1)
    idx_copy = pltpu.make_async_copy(eidx_hbm.at[tile], idx_smem, idx_sem)
    idx_copy.start()
    idx_copy.wait()

    def issue(t, slot):
        for k in range(PEER_PICKS):
            e = idx_smem[t * PEER_PICKS + k]
            pltpu.make_async_copy(tab_hbm.at[e], buf.at[slot, pl.ds(k * n_seg, n_seg), :],
                                  row_sem.at[slot]).start(priority=k % 2)

    def wait_rows(slot):
        pltpu.make_async_copy(buf.at[slot], buf.at[slot], row_sem.at[slot]).wait()

    lane = lax.broadcasted_iota(jnp.int32, (PEER_PICKS, tt), 1)

    def words(slot, s):
        return buf[slot, pl.ds(s, PEER_PICKS, stride=n_seg), :]

    def compute(t, slot):
        xrow = h2_ref[0, pl.ds(t, 1), :]
        prod = None
        for s in range(n_seg):
            u = pltpu.bitcast(words(slot, s) & jnp.uint32(0xFFFF0000), jnp.float32)
            term = u * xrow[:, s * LANE:(s + 1) * LANE]
            prod = term if prod is None else prod + term
        act = jnp.sum(prod, axis=1, keepdims=True)
        gate = jnp.sum(jnp.where(lane == t, gate_ref[...], 0.0), axis=1, keepdims=True)
        coef = gate * _gelu_tanh(act)
        ys = [jnp.sum(coef * pltpu.bitcast(words(slot, s) << 16, jnp.float32), axis=0, keepdims=True)
              for s in range(n_seg)]
        out_ref[0, pl.ds(t, 1), :] = x1_ref[0, pl.ds(t, 1), :] + g2_ref[0] * jnp.concatenate(ys, axis=1)

    for j in range(GATHER_SLOTS):
        issue(j, j)

    def step(t, j, prefetch):
        wait_rows(j)
        compute(t, j)
        if prefetch:
            issue(t + GATHER_SLOTS, j)

    def group(g, carry):
        for j in range(GATHER_SLOTS):
            step(g * GATHER_SLOTS + j, j, True)
        return carry

    n_groups = tt // GATHER_SLOTS
    lax.fori_loop(0, n_groups - 1, group, 0)
    for j in range(GATHER_SLOTS):
        step((n_groups - 1) * GATHER_SLOTS + j, j, False)


def _peer_gather(eidx_t, h2, gate_t, x1, g2, tab, tt):
    B, S, D = x1.shape
    nt = S // tt
    tok = lambda b, i: (b, i, 0)
    kern = functools.partial(_peer_gather_kernel, tt=tt)
    return pl.pallas_call(
        kern,
        grid=(B, nt),
        in_specs=[
            pl.BlockSpec(memory_space=pl.ANY),
            pl.BlockSpec((1, tt, D), tok),
            pl.BlockSpec((PEER_PICKS, tt), lambda b, i: (0, b * nt + i)),
            pl.BlockSpec((1, tt, D), tok),
            pl.BlockSpec((1, 1, D), lambda b, i: (b, 0, 0)),
            pl.BlockSpec(memory_space=pl.ANY),
        ],
        out_specs=pl.BlockSpec((1, tt, D), tok),
        out_shape=jax.ShapeDtypeStruct((B, S, D), jnp.float32),
        scratch_shapes=[
            pltpu.SMEM((tt * PEER_PICKS,), jnp.int32),
            pltpu.VMEM((GATHER_SLOTS, PEER_PICKS * (D // LANE), LANE), jnp.uint32),
            pltpu.SemaphoreType.DMA(()),
            pltpu.SemaphoreType.DMA((GATHER_SLOTS,)),
        ],
        compiler_params=_cparams(("arbitrary", "arbitrary")),
        name="peer_gather",
    )(eidx_t, h2, gate_t, x1, g2, tab)


def _pack_expert_table(peer_u, peer_v):
    u16 = lax.bitcast_convert_type(peer_u.astype(jnp.bfloat16), jnp.uint16).astype(jnp.uint32)
    v16 = lax.bitcast_convert_type(peer_v.astype(jnp.bfloat16), jnp.uint16).astype(jnp.uint32)
    return ((u16 << 16) | v16).reshape(peer_u.shape[0], peer_u.shape[1] // LANE, LANE)


def _rope_tables(n_tokens):
    rows = n_tokens // GRID_W
    row = jnp.repeat(jnp.arange(rows, dtype=jnp.float32), GRID_W)
    col = jnp.tile(jnp.arange(GRID_W, dtype=jnp.float32), rows)
    inv = ROPE_THETA ** (-jnp.arange(ROPE_FREQS, dtype=jnp.float32) / ROPE_FREQS)
    ang = jnp.stack([row[:, None] * inv, col[:, None] * inv], axis=1)
    cos = jnp.cos(ang)[:, None, :, None, :]
    sin = jnp.sin(ang)[:, None, :, None, :]
    cos = jnp.broadcast_to(cos, (n_tokens, 2, 2, 2, ROPE_FREQS)).reshape(n_tokens, LANE)
    sgn = jnp.array([-1.0, 1.0], jnp.float32)[None, None, None, :, None]
    sin = jnp.broadcast_to(sin * sgn, (n_tokens, 2, 2, 2, ROPE_FREQS)).reshape(n_tokens, LANE)
    return cos, sin


def kernel(x, c, ctx, c_ctx, w_mod, b_mod, norm1_g, w_in, q_norm_g, k_norm_g, diff_lambda, diff_out_g, rw_shift,
           rw_w0, rw_w_up, rw_a0, rw_a_up, rw_g_up, rw_k_k, rw_k_a, rw_r_k, rw_ln_g, rw_ln_b, w_branch_a,
           w_branch_b, w_out, norm2_g, peer_wq, peer_keys, peer_u, peer_v):
    assert w_mod.shape[0] == 1, "single-layer trunk only"
    B, S, D = x.shape
    C = ctx.shape[1]
    bf = jnp.bfloat16
    lam_init = 0.8 - 0.6 * math.exp(-0.3 * 0)

    tiles = _tiles(S)

    c_rows = jnp.concatenate([c, c_ctx[None, :]], axis=0)
    c_rows = jnp.pad(c_rows, ((0, -c_rows.shape[0] % SUBLANES), (0, 0)))
    mod = _adaln(c_rows, w_mod[0], b_mod[0][None, :], tiles.proj_cols)
    mod_x = mod[:B].reshape(B, N_MOD, 1, D)
    mod_c = mod[B:B + 1].reshape(1, N_MOD, 1, D)
    sc1x, sh1x = norm1_g[0] * (1 + mod_x[:, 1]), mod_x[:, 0]
    sc1c, sh1c = norm1_g[0] * (1 + mod_c[:, 1]), mod_c[:, 0]
    sc2x, sh2x = norm2_g[0] * (1 + mod_x[:, 4]), mod_x[:, 3]
    g1x, g2x = mod_x[:, 2], mod_x[:, 5]

    w_in_p = jnp.concatenate([w_in[0][:, :6144], w_in[0][:, 6400:], w_in[0][:, 6144:6400]], axis=1).astype(bf)
    zx = _norm_matmul(x, sc1x, sh1x, w_in_p, tiles.proj_rows, tiles.proj_cols)
    zc = _norm_matmul(ctx, sc1c, sh1c, w_in_p, min(tiles.proj_rows, C), tiles.proj_cols)

    lam = (jnp.exp(jnp.sum(diff_lambda[0, 0] * diff_lambda[0, 1])) - jnp.exp(jnp.sum(diff_lambda[0, 2] * diff_lambda[0, 3]))
           + lam_init).reshape(1)
    cos, sin_signed = _rope_tables(S)
    qg = jnp.tile(q_norm_g[0], 2).reshape(1, LANE)
    kg = jnp.tile(k_norm_g[0], 2).reshape(1, LANE)
    og = diff_out_g[0].reshape(1, LANE)
    grp = jnp.arange(LANE) // DA_QK
    avg64 = jnp.where(grp[:, None] == grp[None, :], 1.0 / DA_QK, 0.0).astype(bf)
    o_a = _diff_attention(lam, zx, zc, cos, sin_signed, qg, kg, og, avg64, tiles.attn_q, 1.0 - lam_init)

    hd = jnp.arange(RW_WIDTH) // RW_HEAD
    ones_bd = (hd[:, None] == jnp.arange(LANE)[None, :]).astype(bf)
    row1 = lambda t: t.reshape(1, RW_WIDTH)
    prep = {}
    for name, z, T in (("ctx", zc, C), ("x", zx, S)):
        tt = min(tiles.rwkv, T)
        sw, swl, wl, tril, triu = _rwkv_constants(rw_shift[0], rw_w_up[0], rw_a_up[0], tt)
        prep[name] = _rwkv_prepare(z, sw, swl, rw_w0[0], rw_a0[0], row1(rw_k_k[0]), row1(rw_k_a[0]), row1(rw_r_k[0]),
                                   wl, rw_g_up[0].astype(bf), ones_bd, tril, triu, tt)
    s0 = jnp.zeros((2, B, RW_WIDTH // QW, QW, QW), jnp.float32)
    pc, px = prep["ctx"], prep["x"]
    _, s_ctx = _wkv_scan(pc[0:6], pc[6], pc[7], s0)
    y_rw, _ = _wkv_scan(px[0:6], px[6], px[7], s_ctx)

    x1, h2, qp = _merge(x, o_a, y_rw, px[8], px[9], row1(rw_ln_g[0]), row1(rw_ln_b[0]), ones_bd, zx, g1x,
                        w_branch_a[0].astype(bf), w_branch_b[0].astype(bf), w_out[0].astype(bf),
                        sc2x, sh2x, peer_wq[0].astype(bf), tiles.merge, jnp.float32)

    eidx_t, gate_t = _peer_topk(qp, peer_keys[0], tiles.topk)
    eidx = eidx_t.T.reshape(B * S // tiles.gather, tiles.gather * PEER_PICKS)
    tab = _pack_expert_table(peer_u[0], peer_v[0])
    return _peer_gather(eidx, h2, gate_t, x1, g2x, tab, tiles.gather)
```
